```python
import math
import jax, jax.numpy as jnp
from jax import lax
import numpy as np

D_MODEL = 2048
BATCH = 2
SEQ = 4096
DEPTH = 4
DEC_BATCH = 8
DEC_SEQ = 8
PAST_LEN = 16384
PAGE_SIZE = 128

PLE_DIM = 256
CONV_WIDTH = 3
CONV_DIM = D_MODEL // 2
FOX_HEADS = 8
FOX_HEAD_DIM = D_MODEL // 16
FOX_DIM = FOX_HEADS * FOX_HEAD_DIM
Q_BLOCK = 128
HGRN_HEADS = 8
HGRN_DK = 128
HGRN_DV = D_MODEL // 16
HGRN_DIM = HGRN_HEADS * HGRN_DK
HGRN_CHUNK = 64
D_FF = -(-8 * D_MODEL // (3 * 256)) * 256
RMS_EPS = 1e-6
FOX_FORGET_BIAS_INIT = 4.0
IN_SIZES = (CONV_DIM,) * 3 + (FOX_DIM,) * 3 + (FOX_HEADS,) + (HGRN_DIM, HGRN_DIM, HGRN_HEADS * HGRN_DV, HGRN_HEADS * HGRN_DV) + (D_MODEL,) * 3
D_IN = sum(IN_SIZES)

kernel_name = 'hybrid_conv_fox_hgrn2_decode_step'


def _rmsnorm(x, g):
    xf = x.astype(jnp.float32)
    y = xf * lax.rsqrt(jnp.mean(xf * xf, axis=-1, keepdims=True) + RMS_EPS)
    return (y * g.astype(jnp.float32)).astype(x.dtype)


def _split_cols(z):
    idx = [int(v) for v in np.cumsum(IN_SIZES)[:-1]]
    return jnp.split(z, idx, axis=-1)


def _fox_attend(q, c_q, qpos, k, v, c_k, kpos):
    s = jnp.einsum('bqhd,bkhd->bhqk', q, k).astype(jnp.float32) * (FOX_HEAD_DIM ** -0.5)
    bias = c_q.transpose(0, 2, 1)[..., :, None] - c_k.transpose(0, 2, 1)[..., None, :]
    mask = kpos[None, :] <= qpos[:, None]
    p = jax.nn.softmax(jnp.where(mask, s + bias, -jnp.inf), axis=-1)
    return jnp.einsum('bhqk,bkhd->bqhd', p.astype(v.dtype), v)


def _fox_prompt(q, k, v, log_f):
    b, s, h, d = q.shape
    nb = s // Q_BLOCK
    c = lax.cumsum(log_f, axis=1)
    kpos = jnp.arange(s)
    qb = q.reshape(b, nb, Q_BLOCK, h, d).swapaxes(0, 1)
    cb = c.reshape(b, nb, Q_BLOCK, h).swapaxes(0, 1)
    starts = jnp.arange(nb) * Q_BLOCK

    def one_block(args):
        q_blk, c_blk, start = args
        return _fox_attend(q_blk, c_blk, start + jnp.arange(Q_BLOCK), k, v, c, kpos)

    o = lax.map(one_block, (qb, cb, starts))
    return o.swapaxes(0, 1).reshape(b, s, h, d)


def _hgrn2(q, k, log_f, v, s0):
    b, L, h, dk = q.shape
    dv = v.shape[-1]
    C = math.gcd(L, HGRN_CHUNK)
    n = L // C

    def chunks(t):
        return t.astype(jnp.float32).reshape(b, n, C, *t.shape[2:]).swapaxes(0, 1)

    tri = jnp.tril(jnp.ones((C, C), dtype=bool))[None, :, :, None, None]

    def step(S, inp):
        qc, kc, gc, vc = inp
        bcum = lax.cumsum(gc, axis=1)
        o_inter = jnp.einsum('bthk,bhkv->bthv', qc * jnp.exp(bcum), S)
        diff = jnp.where(tri, bcum[:, :, None] - bcum[:, None, :], -jnp.inf)
        att = jnp.einsum('bthk,bshk,btshk->bhts', qc, kc, jnp.exp(diff))
        o_intra = jnp.einsum('bhts,bshv->bthv', att, vc)
        b_last = bcum[:, -1]
        S_new = jnp.exp(b_last)[..., None] * S + jnp.einsum('bshk,bshv->bhkv', kc * jnp.exp(b_last[:, None] - bcum), vc)
        return S_new, o_inter + o_intra

    s_fin, o = lax.scan(step, s0.astype(jnp.float32), (chunks(q), chunks(k), chunks(log_f), chunks(v)))
    return o.swapaxes(0, 1).reshape(b, L, h, dv), s_fin


def _layer(x, p, conv_prev, past, s0, lb, norm1, w_in, b_fox, conv_w, hgrn_norm, w_a, w_b, w_c, w_o,
           norm2, w_gate_up, w_down, ple_norm, ple_gate, ple_proj):
    b, L, _ = x.shape
    f32 = jnp.float32
    h = _rmsnorm(x, norm1)
    (a_b, a_c, a_v, f_q, f_k, f_v, f_f, r_q, r_f, r_i, r_g, g_a, g_b, g_c) = _split_cols(h @ w_in)

    u = a_c * a_v
    u_ext = jnp.concatenate([conv_prev.astype(u.dtype), u], axis=1)
    conv = sum(conv_w[j] * u_ext[:, j:j + L] for j in range(CONV_WIDTH))
    y_a = (a_b * conv) @ w_a
    new_conv = u_ext[:, L:]

    q = f_q.reshape(b, L, FOX_HEADS, FOX_HEAD_DIM)
    k = f_k.reshape(b, L, FOX_HEADS, FOX_HEAD_DIM)
    v = f_v.reshape(b, L, FOX_HEADS, FOX_HEAD_DIM)
    log_f = jax.nn.log_sigmoid((f_f + b_fox).astype(f32))
    if past is None:
        o_b = _fox_prompt(q, k, v, log_f)
    else:
        k_past, v_past, lf_past = past
        P = k_past.shape[1]
        k_all = jnp.concatenate([k_past.astype(k.dtype), k], axis=1)
        v_all = jnp.concatenate([v_past.astype(v.dtype), v], axis=1)
        c_all = lax.cumsum(jnp.concatenate([lf_past.astype(f32), log_f], axis=1), axis=1)
        o_b = _fox_attend(q, c_all[:, P:], P + jnp.arange(L), k_all, v_all, c_all, jnp.arange(P + L))
    y_b = o_b.reshape(b, L, FOX_DIM) @ w_b

    z_f = r_f.astype(f32).reshape(b, L, HGRN_HEADS, HGRN_DK)
    lb_h = lb.astype(f32).reshape(HGRN_HEADS, HGRN_DK)
    log_fg = jnp.logaddexp(jnp.log(lb_h), jnp.log1p(-lb_h) + jax.nn.log_sigmoid(z_f))
    k_r = (1.0 - lb_h) * jax.nn.sigmoid(-z_f)
    o_c, s_new = _hgrn2(r_q.reshape(b, L, HGRN_HEADS, HGRN_DK), k_r, log_fg,
                        r_i.reshape(b, L, HGRN_HEADS, HGRN_DV), s0)
    o_c = _rmsnorm(o_c, hgrn_norm.reshape(HGRN_HEADS, HGRN_DV)).astype(x.dtype) * jax.nn.silu(r_g.reshape(b, L, HGRN_HEADS, HGRN_DV))
    y_c = o_c.reshape(b, L, HGRN_HEADS * HGRN_DV) @ w_c

    mix = jax.nn.sigmoid(g_a) * y_a + jax.nn.sigmoid(g_b) * y_b + jax.nn.sigmoid(g_c) * y_c
    x = x + mix @ w_o

    gate, up = jnp.split(_rmsnorm(x, norm2) @ w_gate_up, 2, axis=-1)
    x = x + (jax.nn.silu(gate) * up) @ w_down

    x = x + jax.nn.sigmoid(_rmsnorm(x, ple_norm) @ ple_gate) * (p @ ple_proj)
    return x, new_conv, k, v, log_f, s_new


def setup_inputs(seed: int = 0) -> dict:
    key = jax.random.key(seed)
    ks = jax.random.split(key, 32)
    n_pages = PAST_LEN // PAGE_SIZE
    n_used = DEC_BATCH * n_pages
    n_pool = n_used + max(1, n_used // 4)
    nrm = jax.random.normal
    D = D_MODEL
    return {
        'x_prompt': nrm(ks[0], (BATCH, SEQ, D), jnp.float32),
        'x_sample': nrm(ks[1], (DEC_BATCH, DEC_SEQ, D), jnp.float32),
        'cache_k': nrm(ks[2], (DEPTH, n_pool, PAGE_SIZE, FOX_HEADS, FOX_HEAD_DIM), jnp.float32),
        'cache_v': nrm(ks[3], (DEPTH, n_pool, PAGE_SIZE, FOX_HEADS, FOX_HEAD_DIM), jnp.float32),
        'cache_logf': jax.nn.log_sigmoid(FOX_FORGET_BIAS_INIT + nrm(ks[4], (DEPTH, n_pool, PAGE_SIZE, FOX_HEADS), jnp.float32)),
        'state_conv': nrm(ks[5], (DEPTH, DEC_BATCH, CONV_WIDTH - 1, CONV_DIM), jnp.float32),
        'state_hgrn': 0.3 * nrm(ks[6], (DEPTH, DEC_BATCH, HGRN_HEADS, HGRN_DK, HGRN_DV), jnp.float32),
        'page_table': jax.random.permutation(ks[7], n_pool)[:n_used].reshape(DEC_BATCH, n_pages).astype(jnp.int32),
        'p_prompt': nrm(ks[8], (DEPTH, BATCH, SEQ, PLE_DIM), jnp.float32),
        'p_sample': nrm(ks[9], (DEPTH, DEC_BATCH, DEC_SEQ, PLE_DIM), jnp.float32),
        'norm1': 1.0 + 0.05 * nrm(ks[10], (DEPTH, D), jnp.float32),
        'w_in': nrm(ks[11], (DEPTH, D, D_IN), jnp.float32) * D ** -0.5,
        'b_fox': FOX_FORGET_BIAS_INIT + 0.5 * nrm(ks[12], (DEPTH, FOX_HEADS), jnp.float32),
        'conv_w': nrm(ks[13], (DEPTH, CONV_WIDTH, CONV_DIM), jnp.float32) * CONV_WIDTH ** -0.5,
        'hgrn_lb': nrm(ks[14], (DEPTH, HGRN_DIM), jnp.float32),
        'hgrn_norm': 1.0 + 0.05 * nrm(ks[15], (DEPTH, HGRN_HEADS * HGRN_DV), jnp.float32),
        'w_a': nrm(ks[16], (DEPTH, CONV_DIM, D), jnp.float32) * CONV_DIM ** -0.5,
        'w_b': nrm(ks[17], (DEPTH, FOX_DIM, D), jnp.float32) * FOX_DIM ** -0.5,
        'w_c': nrm(ks[18], (DEPTH, HGRN_HEADS * HGRN_DV, D), jnp.float32) * (HGRN_HEADS * HGRN_DV) ** -0.5,
        'w_o': nrm(ks[19], (DEPTH, D, D), jnp.float32) * D ** -0.5,
        'norm2': 1.0 + 0.05 * nrm(ks[20], (DEPTH, D), jnp.float32),
        'w_gate_up': nrm(ks[21], (DEPTH, D, 2 * D_FF), jnp.float32) * D ** -0.5,
        'w_down': nrm(ks[22], (DEPTH, D_FF, D), jnp.float32) * D_FF ** -0.5,
        'ple_norm': 1.0 + 0.05 * nrm(ks[23], (DEPTH, D), jnp.float32),
        'ple_gate': nrm(ks[24], (DEPTH, D, D), jnp.float32) * D ** -0.5,
        'ple_proj': nrm(ks[25], (DEPTH, PLE_DIM, D), jnp.float32) * PLE_DIM ** -0.5,
        'final_norm': 1.0 + 0.05 * nrm(ks[26], (D,), jnp.float32),
    }


def reference(x_prompt, x_sample, cache_k, cache_v, cache_logf, state_conv, state_hgrn, page_table,
              p_prompt, p_sample, norm1, w_in, b_fox, conv_w, hgrn_lb, hgrn_norm, w_a, w_b, w_c, w_o,
              norm2, w_gate_up, w_down, ple_norm, ple_gate, ple_proj, final_norm):
    lb_all = lax.cumsum(jax.nn.softmax(hgrn_lb.astype(jnp.float32), axis=0), axis=0)
    lb_all = lb_all - lb_all[0:1]
    bp = x_prompt.shape[0]
    db = x_sample.shape[0]
    past_len = page_table.shape[1] * cache_k.shape[2]
    conv0 = jnp.zeros((bp, CONV_WIDTH - 1, CONV_DIM), x_prompt.dtype)
    s0 = jnp.zeros((bp, HGRN_HEADS, HGRN_DK, HGRN_DV), jnp.float32)
    xp, xs = x_prompt, x_sample
    kp_l, vp_l, lfp_l, cp_l, sp_l = [], [], [], [], []
    ks_l, vs_l, lfs_l, cs_l, ss_l = [], [], [], [], []
    for i in range(DEPTH):
        w = (norm1[i], w_in[i], b_fox[i], conv_w[i], hgrn_norm[i], w_a[i], w_b[i], w_c[i], w_o[i],
             norm2[i], w_gate_up[i], w_down[i], ple_norm[i], ple_gate[i], ple_proj[i])
        xp, conv_p, k_p, v_p, lf_p, st_p = _layer(xp, p_prompt[i], conv0, None, s0, lb_all[i], *w)
        past = (cache_k[i, page_table].reshape(db, past_len, FOX_HEADS, FOX_HEAD_DIM),
                cache_v[i, page_table].reshape(db, past_len, FOX_HEADS, FOX_HEAD_DIM),
                cache_logf[i, page_table].reshape(db, past_len, FOX_HEADS))
        xs, conv_s, k_s, v_s, lf_s, st_s = _layer(xs, p_sample[i], state_conv[i], past, state_hgrn[i], lb_all[i], *w)
        kp_l.append(k_p); vp_l.append(v_p); lfp_l.append(lf_p); cp_l.append(conv_p); sp_l.append(st_p)
        ks_l.append(k_s); vs_l.append(v_s); lfs_l.append(lf_s); cs_l.append(conv_s); ss_l.append(st_s)
    y_prompt = _rmsnorm(xp, final_norm)
    y_sample = _rmsnorm(xs, final_norm)
    new_k_prompt = jnp.stack(kp_l)
    new_v_prompt = jnp.stack(vp_l)
    new_logf_prompt = jnp.stack(lfp_l)
    new_conv_prompt = jnp.stack(cp_l)
    new_hgrn_prompt = jnp.stack(sp_l)
    new_k_sample = jnp.stack(ks_l)
    new_v_sample = jnp.stack(vs_l)
    new_logf_sample = jnp.stack(lfs_l)
    new_conv_sample = jnp.stack(cs_l)
    new_hgrn_sample = jnp.stack(ss_l)
    return (y_prompt, y_sample, new_k_prompt, new_v_prompt, new_logf_prompt, new_conv_prompt, new_hgrn_prompt,
            new_k_sample, new_v_sample, new_logf_sample, new_conv_sample, new_hgrn_sample)
```

```python
import functools

import numpy as np
import jax
import jax.numpy as jnp
from jax import lax
from jax.experimental import pallas as pl
from jax.experimental.pallas import tpu as pltpu

F32 = jnp.float32
BF16 = jnp.bfloat16
RMS_EPS = 1e-6
LANES = 128
HGRN_CHUNK = 128
VMEM_LIMIT = 56 * 1024 * 1024


def _tile(n, target, align):
    best = None
    for t in range(align, min(n, target) + 1, align):
        if n % t == 0:
            best = t
    assert best is not None, (n, target, align)
    return best


def _params(*sem):
    return pltpu.CompilerParams(dimension_semantics=sem, vmem_limit_bytes=VMEM_LIMIT)


def _sigmoid(x):
    return 1.0 / (1.0 + jnp.exp(-x))


def _log_sigmoid(x):
    return jnp.minimum(x, 0.0) - jnp.log1p(jnp.exp(-jnp.abs(x)))


def _rms_rows(x, g):
    ms = jnp.mean(x * x, axis=-1, keepdims=True)
    return x * lax.rsqrt(ms + RMS_EPS) * g


def _split3(x):
    hi = x.astype(BF16)
    r1 = x - hi.astype(F32)
    mid = r1.astype(BF16)
    lo = (r1 - mid.astype(F32)).astype(BF16)
    return hi, mid, lo


def _dot(a, b):
    return jnp.dot(a, b, preferred_element_type=F32)


def _dot_nt(a, b):
    return lax.dot_general(a, b, (((1,), (1,)), ((), ())), preferred_element_type=F32)


def _dot_tn(a, b):
    return lax.dot_general(a, b, (((0,), (0,)), ((), ())), preferred_element_type=F32)


def _in_proj_kernel(x_ref, g_ref, w_ref, wff_ref, z_ref, ff_ref, h_ref):
    @pl.when(pl.program_id(1) == 0)
    def _():
        h = _rms_rows(x_ref[...], g_ref[...]).astype(BF16)
        h_ref[...] = h
        ff_ref[...] = _dot(h, wff_ref[...])

    z_ref[...] = _dot(h_ref[...], w_ref[...])


def _in_proj(x, norm, w_main, w_ff, layer, tm, tn):
    m, d = x.shape
    n = w_main.shape[2]
    return pl.pallas_call(
        _in_proj_kernel,
        grid=(m // tm, n // tn),
        in_specs=[
            pl.BlockSpec((tm, d), lambda i, j: (i, 0)),
            pl.BlockSpec((None, 1, d), lambda i, j: (layer, 0, 0)),
            pl.BlockSpec((None, d, tn), lambda i, j: (layer, 0, j)),
            pl.BlockSpec((None, d, LANES), lambda i, j: (layer, 0, 0)),
        ],
        out_specs=[
            pl.BlockSpec((tm, tn), lambda i, j: (i, j)),
            pl.BlockSpec((tm, LANES), lambda i, j: (i, 0)),
        ],
        out_shape=[jax.ShapeDtypeStruct((m, n), F32), jax.ShapeDtypeStruct((m, LANES), F32)],
        scratch_shapes=[pltpu.VMEM((tm, d), BF16)],
        compiler_params=_params("arbitrary", "arbitrary"),
        name="in_proj",
    )(x, norm, w_main, w_ff)


def _logf_prompt_kernel(ff_ref, b_ref, tri_ref, lf_ref, c_ref, ct_ref, *, blk):
    s = ff_ref.shape[0]
    lf = _log_sigmoid(ff_ref[...] + b_ref[...])
    lf_ref[...] = lf
    tri3 = tri_ref[...]
    carry = jnp.zeros((1, LANES), F32)
    for i in range(s // blk):
        hi, mid, lo = _split3(lf[i * blk:(i + 1) * blk])
        cb = _dot(tri3, jnp.concatenate([hi, mid, lo], axis=0)) + carry
        c_ref[i * blk:(i + 1) * blk, :] = cb
        carry = cb[blk - 1:blk, :]
    ct = c_ref[...].T
    ct_ref[...] = ct[:ct_ref.shape[0], :]


def _logf_prompt(ff, b_fox_pad, layer, batch, seq, heads):
    blk = _tile(seq, 256, LANES)
    tri = np.tril(np.ones((blk, blk), np.float32))
    tri3 = jnp.asarray(np.concatenate([tri, tri, tri], axis=1), BF16)
    return pl.pallas_call(
        functools.partial(_logf_prompt_kernel, blk=blk),
        grid=(batch,),
        in_specs=[
            pl.BlockSpec((seq, LANES), lambda b: (b, 0)),
            pl.BlockSpec((None, 1, LANES), lambda b: (layer, 0, 0)),
            pl.BlockSpec((blk, 3 * blk), lambda b: (0, 0)),
        ],
        out_specs=[
            pl.BlockSpec((None, seq, LANES), lambda b: (b, 0, 0)),
            pl.BlockSpec((None, seq, LANES), lambda b: (b, 0, 0)),
            pl.BlockSpec((None, heads, seq), lambda b: (b, 0, 0)),
        ],
        out_shape=[
            jax.ShapeDtypeStruct((batch, seq, LANES), F32),
            jax.ShapeDtypeStruct((batch, seq, LANES), F32),
            jax.ShapeDtypeStruct((batch, heads, seq), F32),
        ],
        compiler_params=_params("arbitrary"),
        name="logf_prompt",
    )(ff, b_fox_pad, tri3)


def _conv_kernel(ab_ref, ac_ref, av_ref, w_ref, prev_ref, m_ref, newc_ref, carry_ref):
    i = pl.program_id(1)
    t = ac_ref.shape[0]

    @pl.when(i == 0)
    def _():
        carry_ref[...] = prev_ref[...]

    u = ac_ref[...] * av_ref[...]
    row = lax.broadcasted_iota(jnp.int32, u.shape, 0)
    p1 = carry_ref[1:2, :]
    p2 = carry_ref[0:1, :]
    u1 = jnp.where(row == 0, p1, pltpu.roll(u, 1, 0))
    u2 = pltpu.roll(u, 2, 0)
    u2 = jnp.where(row == 0, p2, jnp.where(row == 1, p1, u2))
    w = w_ref[...]
    conv = w[0:1, :] * u2 + w[1:2, :] * u1 + w[2:3, :] * u
    m_ref[...] = (ab_ref[...] * conv).astype(m_ref.dtype)
    last2 = u[t - 2:t, :]
    carry_ref[...] = last2
    newc_ref[...] = last2


def _conv(z, conv_w, prev, layer, row0, n_seq, seq, tile, width):
    nt = seq // tile
    rb0 = row0 // tile
    return pl.pallas_call(
        _conv_kernel,
        grid=(n_seq, nt),
        in_specs=[
            pl.BlockSpec((tile, width), lambda n, i: (rb0 + n * nt + i, 0)),
            pl.BlockSpec((tile, width), lambda n, i: (rb0 + n * nt + i, 1)),
            pl.BlockSpec((tile, width), lambda n, i: (rb0 + n * nt + i, 2)),
            pl.BlockSpec((None, 3, width), lambda n, i: (layer, 0, 0)),
            pl.BlockSpec((None, 2, width), lambda n, i: (n, 0, 0)),
        ],
        out_specs=[
            pl.BlockSpec((None, tile, width), lambda n, i: (n, i, 0)),
            pl.BlockSpec((None, 2, width), lambda n, i: (n, 0, 0)),
        ],
        out_shape=[
            jax.ShapeDtypeStruct((n_seq, seq, width), BF16),
            jax.ShapeDtypeStruct((n_seq, 2, width), F32),
        ],
        scratch_shapes=[pltpu.VMEM((2, width), F32)],
        compiler_params=_params("arbitrary", "arbitrary"),
        name="gated_conv",
    )(z, z, z, conv_w, prev)


def _fox_prompt_kernel(q_ref, k_ref, v_ref, cq_ref, ck_ref, o_ref, qs_ref, m_ref, l_ref, acc_ref,
                       *, heads, hd, scale):
    qi = pl.program_id(1)
    ki = pl.program_id(2)
    tq = q_ref.shape[0]
    tk = k_ref.shape[0]

    @pl.when(ki == 0)
    def _():
        qs_ref[...] = (q_ref[...] * scale).astype(BF16)
        m_ref[...] = jnp.full(m_ref.shape, -jnp.inf, F32)
        l_ref[...] = jnp.zeros(l_ref.shape, F32)
        acc_ref[...] = jnp.zeros(acc_ref.shape, F32)

    def step(masked):
        kb = k_ref[...].astype(BF16)
        vb = v_ref[...].astype(BF16)
        if masked:
            rows = lax.broadcasted_iota(jnp.int32, (tq, tk), 0)
            cols = lax.broadcasted_iota(jnp.int32, (tq, tk), 1)
            keep = cols <= rows
        for h in range(heads):
            hs = slice(h * hd, (h + 1) * hd)
            s = _dot_nt(qs_ref[:, hs], kb[:, hs])
            s = s + cq_ref[:, h:h + 1] - ck_ref[h:h + 1, :]
            if masked:
                s = jnp.where(keep, s, -jnp.inf)
            m_prev = m_ref[h]
            m_new = jnp.maximum(m_prev, jnp.max(s, axis=-1, keepdims=True))
            alpha = jnp.exp(m_prev - m_new)
            p = jnp.exp(s - m_new)
            l_ref[h] = alpha * l_ref[h] + jnp.sum(p, axis=-1, keepdims=True)
            acc_ref[:, hs] = alpha * acc_ref[:, hs] + _dot(p.astype(BF16), vb[:, hs])
            m_ref[h] = m_new

    @pl.when(ki < qi)
    def _():
        step(False)

    @pl.when(ki == qi)
    def _():
        step(True)
        for h in range(heads):
            hs = slice(h * hd, (h + 1) * hd)
            o_ref[:, hs] = (acc_ref[:, hs] / l_ref[h]).astype(o_ref.dtype)


def _fox_prompt(z, c_col, c_row, batch, seq, heads, hd, tq, qcol):
    nq = seq // tq
    width = heads * hd
    kern = functools.partial(_fox_prompt_kernel, heads=heads, hd=hd, scale=hd ** -0.5)
    return pl.pallas_call(
        kern,
        grid=(batch, nq, nq),
        in_specs=[
            pl.BlockSpec((tq, width), lambda b, qi, ki: (b * nq + qi, qcol)),
            pl.BlockSpec((tq, width), lambda b, qi, ki: (b * nq + jnp.minimum(ki, qi), qcol + 1)),
            pl.BlockSpec((tq, width), lambda b, qi, ki: (b * nq + jnp.minimum(ki, qi), qcol + 2)),
            pl.BlockSpec((None, tq, LANES), lambda b, qi, ki: (b, qi, 0)),
            pl.BlockSpec((None, heads, tq), lambda b, qi, ki: (b, 0, jnp.minimum(ki, qi))),
        ],
        out_specs=pl.BlockSpec((tq, width), lambda b, qi, ki: (b * nq + qi, 0)),
        out_shape=jax.ShapeDtypeStruct((batch * seq, width), BF16),
        scratch_shapes=[
            pltpu.VMEM((tq, width), BF16),
            pltpu.VMEM((heads, tq, 1), F32),
            pltpu.VMEM((heads, tq, 1), F32),
            pltpu.VMEM((tq, width), F32),
        ],
        compiler_params=_params("arbitrary", "arbitrary", "arbitrary"),
        name="fox_prompt",
    )(z, z, z, c_col, c_row)


def _gather_rows_kernel(pt_ref, *refs):
    n_in = len(refs) - 1
    out_ref = refs[-1]
    for r in range(n_in):
        out_ref[:, r, :] = refs[r][:, 0, :]


def _gather_logf_pages(cache_logf_flat, page_table, group):
    depth, _, _, width = cache_logf_flat.shape
    nb, n_pages = page_table.shape
    ng = n_pages // group

    def in_map(r):
        return lambda b, g, pt: (0, pt[b, g * group + r], 0, 0)

    return pl.pallas_call(
        _gather_rows_kernel,
        grid_spec=pltpu.PrefetchScalarGridSpec(
            num_scalar_prefetch=1,
            grid=(nb, ng),
            in_specs=[pl.BlockSpec((depth, None, 1, width), in_map(r)) for r in range(group)],
            out_specs=pl.BlockSpec((depth, None, group, width), lambda b, g, pt: (0, b, g, 0)),
        ),
        out_shape=jax.ShapeDtypeStruct((depth, nb, n_pages, width), F32),
        compiler_params=_params("arbitrary", "arbitrary"),
        name="gather_logf",
    )(page_table, *([cache_logf_flat] * group))


def _past_cumsum_kernel(x_ref, t3_ref, s3_ref, l3_ref, e3_ref, c_ref, tot_ref):
    x3 = jnp.concatenate(_split3(x_ref[...]), axis=1)
    y = _dot(x3, t3_ref[...])
    tot = _dot(x3, s3_ref[...])
    offs = _dot(l3_ref[...], jnp.concatenate(_split3(tot), axis=0))
    c_ref[...] = y + _dot(jnp.concatenate(_split3(offs), axis=1), e3_ref[...])
    n = tot.shape[0]
    tot_ref[...] = offs[n - 1:n, :] + tot[n - 1:n, :]


def _past_cumsum(lf_pages, page, heads):
    depth, nb, n_pages, width = lf_pages.shape
    pos = np.arange(width) // heads
    hd = np.arange(width) % heads
    oh = np.arange(width) // page
    op = np.arange(width) % page
    t = ((hd[:, None] == oh[None, :]) & (pos[:, None] <= op[None, :])).astype(np.float32)
    ssum = (hd[:, None] == np.arange(LANES)[None, :]).astype(np.float32)
    lstrict = np.tril(np.ones((n_pages, n_pages), np.float32), -1)
    e = (np.arange(LANES)[:, None] == oh[None, :]).astype(np.float32)
    t3 = jnp.asarray(np.concatenate([t, t, t], axis=0), BF16)
    s3 = jnp.asarray(np.concatenate([ssum, ssum, ssum], axis=0), BF16)
    l3 = jnp.asarray(np.concatenate([lstrict, lstrict, lstrict], axis=1), BF16)
    e3 = jnp.asarray(np.concatenate([e, e, e], axis=0), BF16)
    x = lf_pages.reshape(depth * nb, n_pages, width)
    full = lambda a: pl.BlockSpec(a.shape, lambda g: (0,) * a.ndim)
    return pl.pallas_call(
        _past_cumsum_kernel,
        grid=(depth * nb,),
        in_specs=[pl.BlockSpec((None, n_pages, width), lambda g: (g, 0, 0)),
                  full(t3), full(s3), full(l3), full(e3)],
        out_specs=[pl.BlockSpec((None, n_pages, width), lambda g: (g, 0, 0)),
                   pl.BlockSpec((None, 1, LANES), lambda g: (g, 0, 0))],
        out_shape=[jax.ShapeDtypeStruct((depth * nb, n_pages, width), F32),
                   jax.ShapeDtypeStruct((depth * nb, 1, LANES), F32)],
        compiler_params=_params("arbitrary"),
        name="past_cumsum",
    )(x, t3, s3, l3, e3)


def _pad_rows(a, n):
    if a.shape[0] == n:
        return a
    return jnp.concatenate([a, jnp.zeros((n - a.shape[0], a.shape[1]), a.dtype)], axis=0)


def _fox_sample_kernel(pt_ref, q_ref, kn_ref, vn_ref, ff_ref, b_ref, ctot_ref, cp_ref, *refs,
                       group, heads, hd, scale):
    k_refs = refs[:group]
    v_refs = refs[group:2 * group]
    o_ref, lf_ref = refs[2 * group], refs[2 * group + 1]
    qs_ref, cq_ref, m_ref, l_ref, acc_ref = refs[2 * group + 2:]
    g = pl.program_id(1)
    ng = pl.num_programs(1)
    nt = q_ref.shape[0]
    nr = qs_ref.shape[0]

    def update(h, s, vb):
        m_prev = m_ref[h]
        m_new = jnp.maximum(m_prev, jnp.max(s, axis=-1, keepdims=True))
        alpha = jnp.exp(m_prev - m_new)
        p = jnp.exp(s - m_new)
        l_ref[h] = alpha * l_ref[h] + jnp.sum(p, axis=-1, keepdims=True)
        acc_ref[h] = alpha * acc_ref[h] + _dot(p.astype(BF16), vb)
        m_ref[h] = m_new

    @pl.when(g == 0)
    def _():
        lf = _log_sigmoid(ff_ref[...] + b_ref[...])
        lf_ref[...] = lf
        trow = lax.broadcasted_iota(jnp.int32, lf.shape, 0)
        cs = lf
        sh = 1
        while sh < nt:
            cs = cs + jnp.where(trow >= sh, pltpu.roll(cs, sh, 0), 0.0)
            sh *= 2
        cq = _pad_rows(ctot_ref[...] + cs, nr)
        cq_ref[...] = cq
        cqt = _pad_rows(cq, LANES).T
        qs_ref[...] = _pad_rows((q_ref[...] * scale).astype(BF16), nr)
        kb = _pad_rows(kn_ref[...], LANES).astype(BF16)
        vb = _pad_rows(vn_ref[...], LANES).astype(BF16)
        rows = lax.broadcasted_iota(jnp.int32, (nr, LANES), 0)
        cols = lax.broadcasted_iota(jnp.int32, (nr, LANES), 1)
        for h in range(heads):
            hs = slice(h * hd, (h + 1) * hd)
            s = _dot_nt(qs_ref[:, hs], kb[:, hs])
            s = s + cq[:, h:h + 1] - cqt[h:h + 1, :]
            s = jnp.where(cols <= rows, s, -jnp.inf)
            m0 = jnp.max(s, axis=-1, keepdims=True)
            p = jnp.exp(s - m0)
            m_ref[h] = m0
            l_ref[h] = jnp.sum(p, axis=-1, keepdims=True)
            acc_ref[h] = _dot(p.astype(BF16), vb[:, hs])

    kbs = [k_refs[r][...].astype(BF16) for r in range(group)]
    vbs = [v_refs[r][...].astype(BF16) for r in range(group)]
    for h in range(heads):
        hs = slice(h * hd, (h + 1) * hd)
        kh = jnp.concatenate([kb[:, hs] for kb in kbs], axis=0)
        vh = jnp.concatenate([vb[:, hs] for vb in vbs], axis=0)
        ck = jnp.concatenate([cp_ref[r, h:h + 1, :] for r in range(group)], axis=1)
        s = _dot_nt(qs_ref[:, hs], kh) + cq_ref[:, h:h + 1] - ck
        update(h, s, vh)

    @pl.when(g == ng - 1)
    def _():
        for h in range(heads):
            o = acc_ref[h] / l_ref[h]
            o_ref[:, h * hd:(h + 1) * hd] = o[:nt].astype(o_ref.dtype)


def _fox_sample(z, ff, b_fox_pad, ctot, c_past, cache_k, cache_v, page_table, layer,
                row0, nt, heads, hd, group, qcol):
    nb, n_pages = page_table.shape
    page = cache_k.shape[2]
    width = heads * hd
    ng = n_pages // group
    rb0 = row0 // nt
    nr = max(nt, 16)

    def page_map(r):
        return lambda b, g, pt: (layer, pt[b, g * group + r], 0, 0)

    row_map = lambda c: (lambda b, g, pt: (rb0 + b, c))
    kern = functools.partial(_fox_sample_kernel, group=group, heads=heads, hd=hd, scale=hd ** -0.5)
    in_specs = [
        pl.BlockSpec((nt, width), row_map(qcol)),
        pl.BlockSpec((nt, width), row_map(qcol + 1)),
        pl.BlockSpec((nt, width), row_map(qcol + 2)),
        pl.BlockSpec((nt, LANES), row_map(0)),
        pl.BlockSpec((None, 1, LANES), lambda b, g, pt: (layer, 0, 0)),
        pl.BlockSpec((None, 1, LANES), lambda b, g, pt: (layer * nb + b, 0, 0)),
        pl.BlockSpec((None, group, heads, page), lambda b, g, pt: (layer * nb + b, g, 0, 0)),
    ]
    in_specs += [pl.BlockSpec((None, None, page, width), page_map(r)) for r in range(group)]
    in_specs += [pl.BlockSpec((None, None, page, width), page_map(r)) for r in range(group)]
    return pl.pallas_call(
        kern,
        grid_spec=pltpu.PrefetchScalarGridSpec(
            num_scalar_prefetch=1,
            grid=(nb, ng),
            in_specs=in_specs,
            out_specs=[pl.BlockSpec((None, nt, width), lambda b, g, pt: (b, 0, 0)),
                       pl.BlockSpec((None, nt, LANES), lambda b, g, pt: (b, 0, 0))],
            scratch_shapes=[
                pltpu.VMEM((nr, width), BF16),
                pltpu.VMEM((nr, LANES), F32),
                pltpu.VMEM((heads, nr, 1), F32),
                pltpu.VMEM((heads, nr, 1), F32),
                pltpu.VMEM((heads, nr, hd), F32),
            ],
        ),
        out_shape=[jax.ShapeDtypeStruct((nb, nt, width), BF16),
                   jax.ShapeDtypeStruct((nb, nt, LANES), F32)],
        compiler_params=_params("arbitrary", "arbitrary"),
        name="fox_sample",
    )(page_table, z, z, z, ff, b_fox_pad, ctot, c_past,
      *([cache_k] * group), *([cache_v] * group))


def _hgrn_operators(c, real):
    t = np.arange(c)[:, None]
    j = np.arange(c)[None, :]
    mats = [(j <= t), (j > t)]
    masks = []
    m = 1
    while m < real:
        blk = t // (2 * m)
        mid = blk * (2 * m) + m - 1
        second = t > mid
        mats.append(np.where(second, (j > mid) & (j <= t), (j > t) & (j <= mid)))
        s = np.arange(c)[None, :]
        masks.append((blk == (s // (2 * m))) & second & (s <= (s // (2 * m)) * (2 * m) + m - 1))
        m *= 2
    a = np.concatenate(mats, axis=0).astype(np.float32)
    a3 = np.concatenate([a, a, a], axis=1)
    return jnp.asarray(a3, BF16), jnp.asarray(np.stack(masks).astype(np.float32)), len(masks)


def _hgrn_kernel(q_ref, f_ref, i_ref, g_ref, lbraw_ref, gn_ref, a3_ref, msk_ref, s0_ref,
                 o_ref, sfin_ref, st_ref, *, layer, heads, dk, n_lev, real):
    c = HGRN_CHUNK
    ci = pl.program_id(1)
    nc = pl.num_programs(1)

    @pl.when(ci == 0)
    def _():
        for h in range(heads):
            st_ref[h] = s0_ref[h].T

    raw = lbraw_ref[...]
    e = jnp.exp(raw - jnp.max(raw, axis=0, keepdims=True))
    sm = e / jnp.sum(e, axis=0, keepdims=True)
    cs = sm[0:1]
    for r in range(1, layer + 1):
        cs = cs + sm[r:r + 1]
    lb = cs - sm[0:1]
    log_lb = jnp.log(lb)
    log_1m = jnp.log1p(-lb)

    def rows(ref):
        x = ref[...]
        if real < c:
            x = jnp.concatenate([x, jnp.zeros((c - real, x.shape[1]), x.dtype)], axis=0)
        return x

    zf = rows(f_ref)
    q = rows(q_ref)
    v = rows(i_ref)
    b = log_1m + _log_sigmoid(zf)
    mx = jnp.maximum(log_lb, b)
    g = mx + jnp.log1p(jnp.exp(-jnp.abs(log_lb - b)))
    kr = (1.0 - lb) * _sigmoid(-zf)
    if real < c:
        valid = lax.broadcasted_iota(jnp.int32, g.shape, 0) < real
        g = jnp.where(valid, g, 0.0)
        kr = jnp.where(valid, kr, 0.0)

    x = jnp.exp(_dot(a3_ref[...], jnp.concatenate(_split3(g), axis=0)))
    xq = x[0:c]
    xk = x[c:2 * c]
    xlast = xq[c - 1:c]
    qe = (q * xq).astype(BF16)
    ke = (kr * xk).astype(BF16)
    vb = v.astype(BF16)
    for h in range(heads):
        hs = slice(h * dk, (h + 1) * dk)
        st = st_ref[h]
        o = _dot_nt(qe[:, hs], st.astype(BF16))
        att = jnp.zeros((c, c), F32)
        for l in range(n_lev):
            xl = x[(2 + l) * c:(3 + l) * c, hs]
            p = _dot_nt((q[:, hs] * xl).astype(BF16), (kr[:, hs] * xl).astype(BF16))
            att = att + p * msk_ref[l]
        o = o + _dot(att.astype(BF16), vb[:, hs])
        o = o + jnp.sum(q[:, hs] * kr[:, hs], axis=-1, keepdims=True) * v[:, hs]
        st_ref[h] = st * xlast[:, hs] + _dot_tn(vb[:, hs], ke[:, hs])
        on = _rms_rows(o, gn_ref[:, hs])
        gg = g_ref[:, hs]
        o_ref[:, hs] = (on[:real] * (gg * _sigmoid(gg))).astype(o_ref.dtype)

    @pl.when(ci == nc - 1)
    def _():
        for h in range(heads):
            sfin_ref[h] = st_ref[h].T


def _hgrn(z, hgrn_lb, hgrn_norm, s0, layer, row0, n_seq, seq, heads, dk, col0):
    c = HGRN_CHUNK
    real = min(seq, c)
    assert seq % real == 0
    nc = seq // real
    width = heads * dk
    rb0 = row0 // real
    a3, masks, n_lev = _hgrn_operators(c, real)
    depth = hgrn_lb.shape[0]
    blk = lambda col: pl.BlockSpec((real, width), lambda n, i: (rb0 + n * nc + i, col))
    kern = functools.partial(_hgrn_kernel, layer=layer, heads=heads, dk=dk, n_lev=n_lev, real=real)
    return pl.pallas_call(
        kern,
        grid=(n_seq, nc),
        in_specs=[
            blk(col0), blk(col0 + 1), blk(col0 + 2), blk(col0 + 3),
            pl.BlockSpec((depth, width), lambda n, i: (0, 0)),
            pl.BlockSpec((None, 1, width), lambda n, i: (layer, 0, 0)),
            pl.BlockSpec(a3.shape, lambda n, i: (0, 0)),
            pl.BlockSpec(masks.shape, lambda n, i: (0, 0, 0)),
            pl.BlockSpec((None, heads, dk, dk), lambda n, i: (n, 0, 0, 0)),
        ],
        out_specs=[
            pl.BlockSpec((None, real, width), lambda n, i: (n, i, 0)),
            pl.BlockSpec((None, heads, dk, dk), lambda n, i: (n, 0, 0, 0)),
        ],
        out_shape=[
            jax.ShapeDtypeStruct((n_seq, seq, width), BF16),
            jax.ShapeDtypeStruct((n_seq, heads, dk, dk), F32),
        ],
        scratch_shapes=[pltpu.VMEM((heads, dk, dk), F32)],
        compiler_params=_params("arbitrary", "arbitrary"),
        name="hgrn",
    )(z, z, z, z, hgrn_lb, hgrn_norm, a3, masks, s0)


def _merge_kernel(ma_ref, ob_ref, oc_ref, wa_ref, wb_ref, wc_ref, ga_ref, gb_ref, gc_ref, mix_ref):
    ya = _dot(ma_ref[...], wa_ref[...])
    yb = _dot(ob_ref[...], wb_ref[...])
    yc = _dot(oc_ref[...], wc_ref[...])
    mix = _sigmoid(ga_ref[...]) * ya + _sigmoid(gb_ref[...]) * yb + _sigmoid(gc_ref[...]) * yc
    mix_ref[...] = mix.astype(mix_ref.dtype)


def _merge(m_a, o_b, o_c, w_a, w_b, w_c, z, layer, tm, tn, gcol0):
    m, k = m_a.shape
    d = w_a.shape[2]
    nj = d // tn
    g0 = gcol0 // tn
    act = pl.BlockSpec((tm, k), lambda i, j: (i, 0))
    wsp = pl.BlockSpec((None, k, tn), lambda i, j: (layer, 0, j))
    gate = lambda r: pl.BlockSpec((tm, tn), lambda i, j: (i, g0 + r * nj + j))
    return pl.pallas_call(
        _merge_kernel,
        grid=(m // tm, nj),
        in_specs=[act, act, act, wsp, wsp, wsp, gate(0), gate(1), gate(2)],
        out_specs=pl.BlockSpec((tm, tn), lambda i, j: (i, j)),
        out_shape=jax.ShapeDtypeStruct((m, d), BF16),
        compiler_params=_params("arbitrary", "arbitrary"),
        name="merge",
    )(m_a, o_b, o_c, w_a, w_b, w_c, z, z, z)


def _residual_matmul_kernel(a_ref, w_ref, x_ref, o_ref):
    o_ref[...] = x_ref[...] + _dot(a_ref[...], w_ref[...])


def _residual_matmul(a, w, x, layer, tm, tn):
    m, k = a.shape
    d = w.shape[2]
    return pl.pallas_call(
        _residual_matmul_kernel,
        grid=(m // tm, d // tn),
        in_specs=[
            pl.BlockSpec((tm, k), lambda i, j: (i, 0)),
            pl.BlockSpec((None, k, tn), lambda i, j: (layer, 0, j)),
            pl.BlockSpec((tm, tn), lambda i, j: (i, j)),
        ],
        out_specs=pl.BlockSpec((tm, tn), lambda i, j: (i, j)),
        out_shape=jax.ShapeDtypeStruct((m, d), F32),
        compiler_params=_params("arbitrary", "arbitrary"),
        name="residual_matmul",
    )(a, w, x)


def _swiglu_up_kernel(x_ref, g_ref, wg_ref, wu_ref, a_ref, h_ref):
    @pl.when(pl.program_id(1) == 0)
    def _():
        h_ref[...] = _rms_rows(x_ref[...], g_ref[...]).astype(BF16)

    h = h_ref[...]
    gate = _dot(h, wg_ref[...])
    up = _dot(h, wu_ref[...])
    a_ref[...] = (gate * _sigmoid(gate) * up).astype(a_ref.dtype)


def _swiglu_up(x, norm, w_gate_up, layer, tm, tn):
    m, d = x.shape
    f = w_gate_up.shape[2] // 2
    nj = f // tn
    return pl.pallas_call(
        _swiglu_up_kernel,
        grid=(m // tm, nj),
        in_specs=[
            pl.BlockSpec((tm, d), lambda i, j: (i, 0)),
            pl.BlockSpec((None, 1, d), lambda i, j: (layer, 0, 0)),
            pl.BlockSpec((None, d, tn), lambda i, j: (layer, 0, j)),
            pl.BlockSpec((None, d, tn), lambda i, j: (layer, 0, nj + j)),
        ],
        out_specs=pl.BlockSpec((tm, tn), lambda i, j: (i, j)),
        out_shape=jax.ShapeDtypeStruct((m, f), BF16),
        scratch_shapes=[pltpu.VMEM((tm, d), BF16)],
        compiler_params=_params("arbitrary", "arbitrary"),
        name="swiglu_up",
    )(x, norm, w_gate_up, w_gate_up)


def _ple_kernel(x_ref, g_ref, wg_ref, p_ref, wp_ref, o_ref, h_ref, *, tn):
    j = pl.program_id(1)

    @pl.when(j == 0)
    def _():
        h_ref[...] = _rms_rows(x_ref[...], g_ref[...]).astype(BF16)

    gate = _sigmoid(_dot(h_ref[...], wg_ref[...]))
    emb = _dot(p_ref[...].astype(BF16), wp_ref[...])
    xs = x_ref[:, pl.ds(pl.multiple_of(j * tn, tn), tn)]
    o_ref[...] = xs + gate * emb


def _ple(x, norm, w_gate, p, w_proj, layer, tm, tn):
    m, d = x.shape
    pd = p.shape[2]
    return pl.pallas_call(
        functools.partial(_ple_kernel, tn=tn),
        grid=(m // tm, d // tn),
        in_specs=[
            pl.BlockSpec((tm, d), lambda i, j: (i, 0)),
            pl.BlockSpec((None, 1, d), lambda i, j: (layer, 0, 0)),
            pl.BlockSpec((None, d, tn), lambda i, j: (layer, 0, j)),
            pl.BlockSpec((None, tm, pd), lambda i, j: (layer, i, 0)),
            pl.BlockSpec((None, pd, tn), lambda i, j: (layer, 0, j)),
        ],
        out_specs=pl.BlockSpec((tm, tn), lambda i, j: (i, j)),
        out_shape=jax.ShapeDtypeStruct((m, d), F32),
        scratch_shapes=[pltpu.VMEM((tm, d), BF16)],
        compiler_params=_params("arbitrary", "arbitrary"),
        name="ple",
    )(x, norm, w_gate, p, w_proj)


def _final_norm_kernel(x_ref, g_ref, o_ref):
    o_ref[...] = _rms_rows(x_ref[...], g_ref[...])


def _final_norm(x, g, tm):
    m, d = x.shape
    return pl.pallas_call(
        _final_norm_kernel,
        grid=(m // tm,),
        in_specs=[pl.BlockSpec((tm, d), lambda i: (i, 0)), pl.BlockSpec((1, d), lambda i: (0, 0))],
        out_specs=pl.BlockSpec((tm, d), lambda i: (i, 0)),
        out_shape=jax.ShapeDtypeStruct((m, d), F32),
        compiler_params=_params("arbitrary"),
        name="final_norm",
    )(x, g)


def kernel(x_prompt, x_sample, cache_k, cache_v, cache_logf, state_conv, state_hgrn, page_table,
           p_prompt, p_sample, norm1, w_in, b_fox, conv_w, hgrn_lb, hgrn_norm, w_a, w_b, w_c, w_o,
           norm2, w_gate_up, w_down, ple_norm, ple_gate, ple_proj, final_norm):
    bp, seq, d = x_prompt.shape
    db, dseq, _ = x_sample.shape
    depth = w_in.shape[0]
    heads, hd = cache_k.shape[3], cache_k.shape[4]
    page = cache_k.shape[2]
    n_pool = cache_k.shape[1]
    cw = w_a.shape[1]
    fw = w_b.shape[1]
    rw = w_c.shape[1]
    rheads = state_hgrn.shape[2]
    dk = state_hgrn.shape[3]
    assert cw == fw == rw == heads * hd == rheads * dk and state_hgrn.shape[4] == dk
    mp = bp * seq
    ms = db * dseq
    m = mp + ms

    o_ff = 3 * cw + 3 * fw
    w_main = jnp.concatenate([w_in[:, :, :o_ff], w_in[:, :, o_ff + heads:]], axis=2).astype(BF16)
    w_ff = jnp.pad(w_in[:, :, o_ff:o_ff + heads], ((0, 0), (0, 0), (0, LANES - heads))).astype(BF16)
    gcol0 = 3 * cw + 3 * fw + 4 * rw
    assert w_main.shape[2] == gcol0 + 3 * d

    bf = lambda a: a.astype(BF16)
    w_a_b, w_b_b, w_c_b, w_o_b = bf(w_a), bf(w_b), bf(w_c), bf(w_o)
    w_gu_b, w_dn_b, w_pg_b, w_pp_b = bf(w_gate_up), bf(w_down), bf(ple_gate), bf(ple_proj)
    row3 = lambda a: a.reshape(a.shape[0], 1, a.shape[1])
    norm1_r, norm2_r, ple_norm_r, hnorm_r = row3(norm1), row3(norm2), row3(ple_norm), row3(hgrn_norm)
    b_fox_pad = row3(jnp.pad(b_fox, ((0, 0), (0, LANES - heads))))

    x = jnp.concatenate([x_prompt.reshape(mp, d), x_sample.reshape(ms, d)], axis=0)
    p_all = jnp.concatenate([p_prompt.reshape(depth, mp, -1), p_sample.reshape(depth, ms, -1)], axis=1)
    ck = cache_k.reshape(depth, n_pool, page, heads * hd)
    cv = cache_v.reshape(depth, n_pool, page, heads * hd)

    n_pages = page_table.shape[1]
    group = _tile(n_pages, 8, 1)
    lf_flat = cache_logf.reshape(depth, n_pool, 1, page * heads)
    lf_pages = _gather_logf_pages(lf_flat, page_table, group)
    c_past, c_tot = _past_cumsum(lf_pages, page, heads)
    c_past = c_past.reshape(depth * db, n_pages, heads, page)

    tm = _tile(m, 704, 16)
    tq = _tile(seq, 512, LANES)
    conv_tile = _tile(seq, 512, 8)
    zeros_conv = jnp.zeros((bp, 2, cw), F32)
    zeros_state = jnp.zeros((bp, rheads, dk, dk), F32)

    outs = {k: [] for k in ("kp", "vp", "lfp", "cp", "sp", "ks", "vs", "lfs", "cs", "ss")}
    for i in range(depth):
        z, ff = _in_proj(x, norm1_r, w_main, w_ff, i, tm, 1024)

        ma_p, conv_p = _conv(z, conv_w, zeros_conv, i, 0, bp, seq, conv_tile, cw)
        ma_s, conv_s = _conv(z, conv_w, state_conv[i], i, mp, db, dseq, dseq, cw)

        lf_p, c_col, c_row = _logf_prompt(ff, b_fox_pad, i, bp, seq, heads)
        ob_p = _fox_prompt(z, c_col, c_row, bp, seq, heads, hd, tq, 3 * cw // fw)
        ob_s, lf_s = _fox_sample(z, ff, b_fox_pad, c_tot, c_past, ck, cv, page_table, i,
                                 mp, dseq, heads, hd, group, 3 * cw // fw)

        rcol = (3 * cw + 3 * fw) // rw
        oc_p, st_p = _hgrn(z, hgrn_lb, hnorm_r, zeros_state, i, 0, bp, seq, rheads, dk, rcol)
        oc_s, st_s = _hgrn(z, hgrn_lb, hnorm_r, state_hgrn[i], i, mp, db, dseq, rheads, dk, rcol)

        m_a = jnp.concatenate([ma_p.reshape(mp, cw), ma_s.reshape(ms, cw)], axis=0)
        o_b = jnp.concatenate([ob_p, ob_s.reshape(ms, fw)], axis=0)
        o_c = jnp.concatenate([oc_p.reshape(mp, rw), oc_s.reshape(ms, rw)], axis=0)
        mix = _merge(m_a, o_b, o_c, w_a_b, w_b_b, w_c_b, z, i, tm, 512, gcol0)
        x = _residual_matmul(mix, w_o_b, x, i, tm, 1024)

        act = _swiglu_up(x, norm2_r, w_gu_b, i, tm, _tile(w_gate_up.shape[2] // 2, 512, LANES))
        x = _residual_matmul(act, w_dn_b, x, i, tm, 512)
        x = _ple(x, ple_norm_r, w_pg_b, p_all, w_pp_b, i, tm, 1024)

        kcol = 3 * cw + fw
        outs["kp"].append(z[:mp, kcol:kcol + fw].reshape(bp, seq, heads, hd))
        outs["vp"].append(z[:mp, kcol + fw:kcol + 2 * fw].reshape(bp, seq, heads, hd))
        outs["ks"].append(z[mp:, kcol:kcol + fw].reshape(db, dseq, heads, hd))
        outs["vs"].append(z[mp:, kcol + fw:kcol + 2 * fw].reshape(db, dseq, heads, hd))
        outs["lfp"].append(lf_p[:, :, :heads])
        outs["lfs"].append(lf_s[:, :, :heads])
        outs["cp"].append(conv_p)
        outs["cs"].append(conv_s)
        outs["sp"].append(st_p)
        outs["ss"].append(st_s)

    y = _final_norm(x, final_norm.reshape(1, d), tm)
    st = lambda k: jnp.stack(outs[k])
    return (y[:mp].reshape(bp, seq, d), y[mp:].reshape(db, dseq, d),
            st("kp"), st("vp"), st("lfp"), st("cp"), st("sp"),
            st("ks"), st("vs"), st("lfs"), st("cs"), st("ss"))
```

```python
import functools

import numpy as np
import jax
import jax.numpy as jnp
from jax import lax
from jax.experimental import pallas as pl
from jax.experimental.pallas import tpu as pltpu

F32 = jnp.float32
BF16 = jnp.bfloat16
RMS_EPS = 1e-6
LANES = 128
HGRN_CHUNK = 128
VMEM_LIMIT = 56 * 1024 * 1024


def _tile(n, target, align):
    best = None
    for t in range(align, min(n, target) + 1, align):
        if n % t == 0:
            best = t
    assert best is not None, (n, target, align)
    return best


def _params(*sem):
    return pltpu.CompilerParams(dimension_semantics=sem, vmem_limit_bytes=VMEM_LIMIT)


def _sigmoid(x):
    return 1.0 / (1.0 + jnp.exp(-x))


def _log_sigmoid(x):
    return jnp.minimum(x, 0.0) - jnp.log1p(jnp.exp(-jnp.abs(x)))


def _rms_rows(x, g):
    ms = jnp.mean(x * x, axis=-1, keepdims=True)
    return x * lax.rsqrt(ms + RMS_EPS) * g


def _split3(x):
    hi = x.astype(BF16)
    r1 = x - hi.astype(F32)
    mid = r1.astype(BF16)
    lo = (r1 - mid.astype(F32)).astype(BF16)
    return hi, mid, lo


def _dot(a, b):
    return jnp.dot(a, b, preferred_element_type=F32)


def _dot_nt(a, b):
    return lax.dot_general(a, b, (((1,), (1,)), ((), ())), preferred_element_type=F32)


def _dot_tn(a, b):
    return lax.dot_general(a, b, (((0,), (0,)), ((), ())), preferred_element_type=F32)


def _pad_rows(a, n):
    if a.shape[0] == n:
        return a
    return jnp.concatenate([a, jnp.zeros((n - a.shape[0], a.shape[1]), a.dtype)], axis=0)


def _any_spec():
    return pl.BlockSpec(memory_space=pl.ANY)


def _in_proj_kernel(x_ref, g_ref, w_ref, wff_ref,
                    zc_ref, q_ref, k32_ref, kb_ref, v32_ref, vb_ref, zr_ref, zg_ref, ff_ref, h_ref,
                    *, bounds, heads, hd, scale):
    j = pl.program_id(1)
    tm = x_ref.shape[0]
    tn = w_ref.shape[1]
    jq, jk, jv, jr, jg = bounds

    @pl.when(j == 0)
    def _():
        h = _rms_rows(x_ref[...], g_ref[...]).astype(BF16)
        h_ref[...] = h
        ff_ref[...] = _dot(h, wff_ref[...])

    def mm():
        return _dot(h_ref[...], w_ref[...])

    def heads_out(z, o32_ref, ob_ref, j0):
        ob_ref[...] = z.astype(BF16)
        per_tile = tn // hd
        for t in range(jv - jk):
            @pl.when(j == j0 + t)
            def _():
                for hh in range(per_tile):
                    o32_ref[pl.ds(t * per_tile + hh, tm, stride=heads), :] = z[:, hh * hd:(hh + 1) * hd]

    @pl.when(j < jq)
    def _():
        zc_ref[...] = mm()

    @pl.when((j >= jq) & (j < jk))
    def _():
        q_ref[...] = mm() * scale

    @pl.when((j >= jk) & (j < jv))
    def _():
        heads_out(mm(), k32_ref, kb_ref, jk)

    @pl.when((j >= jv) & (j < jr))
    def _():
        heads_out(mm(), v32_ref, vb_ref, jv)

    @pl.when((j >= jr) & (j < jg))
    def _():
        zr_ref[...] = mm()

    @pl.when(j >= jg)
    def _():
        zg_ref[...] = mm().astype(BF16)


def _in_proj(x, norm, w_main, w_ff, layer, tm, tn, widths, heads, hd):
    m, d = x.shape
    n = w_main.shape[2]
    assert all(w % tn == 0 for w in widths) and sum(widths) == n
    assert widths[1] == widths[2] == widths[3] == heads * hd and tn % hd == 0
    cuts = np.cumsum([w // tn for w in widths])
    jq, jk, jv, jr, jg = (int(c) for c in cuts[:5])
    nj = int(cuts[5])
    clip = lambda j, lo, cnt: jnp.clip(j - lo, 0, cnt - 1)
    kern = functools.partial(_in_proj_kernel, bounds=(jq, jk, jv, jr, jg), heads=heads, hd=hd,
                             scale=hd ** -0.5)
    fw = heads * hd
    return pl.pallas_call(
        kern,
        grid=(m // tm, nj),
        in_specs=[
            pl.BlockSpec((tm, d), lambda i, j: (i, 0)),
            pl.BlockSpec((None, 1, d), lambda i, j: (layer, 0, 0)),
            pl.BlockSpec((None, d, tn), lambda i, j: (layer, 0, j)),
            pl.BlockSpec((None, d, LANES), lambda i, j: (layer, 0, 0)),
        ],
        out_specs=[
            pl.BlockSpec((tm, tn), lambda i, j: (i, clip(j, 0, jq))),
            pl.BlockSpec((tm, tn), lambda i, j: (i, clip(j, jq, jk - jq))),
            pl.BlockSpec((tm * heads, hd), lambda i, j: (i, 0)),
            pl.BlockSpec((tm, tn), lambda i, j: (i, clip(j, jk, jv - jk))),
            pl.BlockSpec((tm * heads, hd), lambda i, j: (i, 0)),
            pl.BlockSpec((tm, tn), lambda i, j: (i, clip(j, jv, jr - jv))),
            pl.BlockSpec((tm, tn), lambda i, j: (i, clip(j, jr, jg - jr))),
            pl.BlockSpec((tm, tn), lambda i, j: (i, clip(j, jg, nj - jg))),
            pl.BlockSpec((tm, LANES), lambda i, j: (i, 0)),
        ],
        out_shape=[
            jax.ShapeDtypeStruct((m, widths[0]), F32),
            jax.ShapeDtypeStruct((m, fw), F32),
            jax.ShapeDtypeStruct((m * heads, hd), F32),
            jax.ShapeDtypeStruct((m, fw), BF16),
            jax.ShapeDtypeStruct((m * heads, hd), F32),
            jax.ShapeDtypeStruct((m, fw), BF16),
            jax.ShapeDtypeStruct((m, widths[4]), F32),
            jax.ShapeDtypeStruct((m, widths[5]), BF16),
            jax.ShapeDtypeStruct((m, LANES), F32),
        ],
        scratch_shapes=[pltpu.VMEM((tm, d), BF16)],
        compiler_params=_params("arbitrary", "arbitrary"),
        name="in_proj",
    )(x, norm, w_main, w_ff)


def _bias_operators(heads, width):
    hd = width // heads
    selq = np.zeros((3 * LANES, width), np.float32)
    selk = np.zeros((3 * LANES, width), np.float32)
    cq = np.zeros((1, width), np.float32)
    ck = np.zeros((1, width), np.float32)
    for h in range(heads):
        for r in range(3):
            selq[r * LANES + h, h * hd + r] = 1.0
            selk[r * LANES + h, h * hd + 3 + r] = -1.0
            cq[0, h * hd + 3 + r] = 1.0
            ck[0, h * hd + r] = 1.0
    return jnp.asarray(selq, BF16), jnp.asarray(selk, BF16), jnp.asarray(cq), jnp.asarray(ck)


def _logf_prompt_kernel(ff_ref, b_ref, tri_ref, selq_ref, selk_ref, cq_ref, ck_ref,
                        lf_ref, qb_ref, kb_ref, *, blk):
    s = ff_ref.shape[0]
    lf = _log_sigmoid(ff_ref[...] + b_ref[...])
    lf_ref[...] = lf
    tri3 = tri_ref[...]
    carry = jnp.zeros((1, LANES), F32)
    for i in range(s // blk):
        rows = slice(i * blk, (i + 1) * blk)
        cb = _dot(tri3, jnp.concatenate(_split3(lf[rows]), axis=0)) + carry
        carry = cb[blk - 1:blk, :]
        c3 = jnp.concatenate(_split3(cb), axis=1)
        qb_ref[rows, :] = (_dot(c3, selq_ref[...]) + cq_ref[...]).astype(BF16)
        kb_ref[rows, :] = (_dot(c3, selk_ref[...]) + ck_ref[...]).astype(BF16)


def _logf_prompt(ff, b_fox_pad, layer, batch, seq, heads, width):
    blk = _tile(seq, 256, LANES)
    tri = np.tril(np.ones((blk, blk), np.float32))
    tri3 = jnp.asarray(np.concatenate([tri, tri, tri], axis=1), BF16)
    ops = _bias_operators(heads, width)
    full = lambda a: pl.BlockSpec(a.shape, lambda b: (0,) * a.ndim)
    return pl.pallas_call(
        functools.partial(_logf_prompt_kernel, blk=blk),
        grid=(batch,),
        in_specs=[
            pl.BlockSpec((seq, LANES), lambda b: (b, 0)),
            pl.BlockSpec((None, 1, LANES), lambda b: (layer, 0, 0)),
            full(tri3), full(ops[0]), full(ops[1]), full(ops[2]), full(ops[3]),
        ],
        out_specs=[
            pl.BlockSpec((None, seq, LANES), lambda b: (b, 0, 0)),
            pl.BlockSpec((seq, width), lambda b: (b, 0)),
            pl.BlockSpec((seq, width), lambda b: (b, 0)),
        ],
        out_shape=[
            jax.ShapeDtypeStruct((batch, seq, LANES), F32),
            jax.ShapeDtypeStruct((batch * seq, width), BF16),
            jax.ShapeDtypeStruct((batch * seq, width), BF16),
        ],
        compiler_params=_params("arbitrary"),
        name="logf_prompt",
    )(ff, b_fox_pad, tri3, *ops)


def _conv_kernel(*refs, aliased):
    if aliased:
        refs = refs[1:]
    ab_ref, ac_ref, av_ref, w_ref, prev_ref, m_ref, newc_ref, carry_ref = refs
    i = pl.program_id(1)
    t = ac_ref.shape[0]

    @pl.when(i == 0)
    def _():
        carry_ref[...] = prev_ref[...]

    u = ac_ref[...] * av_ref[...]
    row = lax.broadcasted_iota(jnp.int32, u.shape, 0)
    p1 = carry_ref[1:2, :]
    p2 = carry_ref[0:1, :]
    u1 = jnp.where(row == 0, p1, pltpu.roll(u, 1, 0))
    u2 = pltpu.roll(u, 2, 0)
    u2 = jnp.where(row == 0, p2, jnp.where(row == 1, p1, u2))
    w = w_ref[...]
    conv = w[0:1, :] * u2 + w[1:2, :] * u1 + w[2:3, :] * u
    m_ref[...] = ab_ref[...] * conv
    last2 = u[t - 2:t, :]
    carry_ref[...] = last2
    newc_ref[...] = last2


def _conv(zc, conv_w, prev, layer, row0, n_seq, seq, tile, width, base=None):
    nt = seq // tile
    rb0 = row0 // tile
    rows = lambda c: pl.BlockSpec((tile, width), lambda n, i: (rb0 + n * nt + i, c))
    aliased = base is not None
    in_specs = [rows(0), rows(1), rows(2),
                pl.BlockSpec((None, 3, width), lambda n, i: (layer, 0, 0)),
                pl.BlockSpec((None, 2, width), lambda n, i: (n, 0, 0))]
    args = [zc, zc, zc, conv_w, prev]
    if aliased:
        in_specs = [_any_spec()] + in_specs
        args = [base] + args
    return pl.pallas_call(
        functools.partial(_conv_kernel, aliased=aliased),
        grid=(n_seq, nt),
        in_specs=in_specs,
        out_specs=[
            rows(0),
            pl.BlockSpec((None, 2, width), lambda n, i: (n, 0, 0)),
        ],
        out_shape=[
            jax.ShapeDtypeStruct((zc.shape[0], width), F32),
            jax.ShapeDtypeStruct((n_seq, 2, width), F32),
        ],
        scratch_shapes=[pltpu.VMEM((2, width), F32)],
        input_output_aliases={0: 0} if aliased else {},
        compiler_params=_params("arbitrary", "arbitrary"),
        name="gated_conv",
    )(*args)


def _fox_prompt_kernel(q_ref, qb_ref, k_ref, kb_ref, v_ref, o_ref, qs_ref, m_ref, l_ref, acc_ref,
                       *, heads, hd):
    qi = pl.program_id(1)
    ki = pl.program_id(2)
    tq = q_ref.shape[0]
    tk = k_ref.shape[0]

    @pl.when(ki == 0)
    def _():
        qs_ref[...] = q_ref[...].astype(BF16)
        m_ref[...] = jnp.full(m_ref.shape, -jnp.inf, F32)
        l_ref[...] = jnp.zeros(l_ref.shape, F32)
        acc_ref[...] = jnp.zeros(acc_ref.shape, F32)

    def step(masked):
        if masked:
            krow = lax.broadcasted_iota(jnp.int32, (tk, tq), 0)
            qcol = lax.broadcasted_iota(jnp.int32, (tk, tq), 1)
            keep = krow <= qcol
        for h in range(heads):
            hs = slice(h * hd, (h + 1) * hd)
            ka = jnp.concatenate([k_ref[:, hs], kb_ref[:, hs]], axis=1)
            qa = jnp.concatenate([qs_ref[:, hs], qb_ref[:, hs]], axis=1)
            st = _dot_nt(ka, qa)
            if masked:
                st = jnp.where(keep, st, -jnp.inf)
            m_prev = m_ref[h]
            m_new = jnp.maximum(m_prev, jnp.max(st, axis=0, keepdims=True))
            alpha = jnp.exp(m_prev - m_new)
            p = jnp.exp(st - m_new)
            l_ref[h] = alpha * l_ref[h] + jnp.sum(p, axis=0, keepdims=True)
            acc_ref[h] = alpha * acc_ref[h] + _dot_tn(v_ref[:, hs], p.astype(BF16))
            m_ref[h] = m_new

    @pl.when(ki < qi)
    def _():
        step(False)

    @pl.when(ki == qi)
    def _():
        step(True)
        for h in range(heads):
            o_ref[:, h * hd:(h + 1) * hd] = (acc_ref[h] / l_ref[h]).T


def _fox_prompt(q, qbias, k, kbias, v, batch, seq, heads, hd, tq):
    nq = seq // tq
    width = heads * hd
    kern = functools.partial(_fox_prompt_kernel, heads=heads, hd=hd)
    qmap = lambda b, qi, ki: (b * nq + qi, 0)
    kmap = lambda b, qi, ki: (b * nq + jnp.minimum(ki, qi), 0)
    return pl.pallas_call(
        kern,
        grid=(batch, nq, nq),
        in_specs=[
            pl.BlockSpec((tq, width), qmap),
            pl.BlockSpec((tq, width), qmap),
            pl.BlockSpec((tq, width), kmap),
            pl.BlockSpec((tq, width), kmap),
            pl.BlockSpec((tq, width), kmap),
        ],
        out_specs=pl.BlockSpec((tq, width), qmap),
        out_shape=jax.ShapeDtypeStruct((q.shape[0], width), F32),
        scratch_shapes=[
            pltpu.VMEM((tq, width), BF16),
            pltpu.VMEM((heads, 1, tq), F32),
            pltpu.VMEM((heads, 1, tq), F32),
            pltpu.VMEM((heads, hd, tq), F32),
        ],
        compiler_params=_params("arbitrary", "arbitrary", "arbitrary"),
        name="fox_prompt",
    )(q, qbias, k, kbias, v)


def _gather_rows_kernel(pt_ref, *refs):
    n_in = len(refs) - 1
    out_ref = refs[-1]
    for r in range(n_in):
        out_ref[:, r, :] = refs[r][:, 0, :]


def _gather_logf_pages(cache_logf_flat, page_table, group):
    depth, _, _, width = cache_logf_flat.shape
    nb, n_pages = page_table.shape
    ng = n_pages // group

    def in_map(r):
        return lambda b, g, pt: (0, pt[b, g * group + r], 0, 0)

    return pl.pallas_call(
        _gather_rows_kernel,
        grid_spec=pltpu.PrefetchScalarGridSpec(
            num_scalar_prefetch=1,
            grid=(nb, ng),
            in_specs=[pl.BlockSpec((depth, None, 1, width), in_map(r)) for r in range(group)],
            out_specs=pl.BlockSpec((depth, None, group, width), lambda b, g, pt: (0, b, g, 0)),
        ),
        out_shape=jax.ShapeDtypeStruct((depth, nb, n_pages, width), F32),
        compiler_params=_params("arbitrary", "arbitrary"),
        name="gather_logf",
    )(page_table, *([cache_logf_flat] * group))


def _past_cumsum_kernel(x_ref, t3_ref, s3_ref, l3_ref, e3_ref, c_ref, tot_ref):
    x3 = jnp.concatenate(_split3(x_ref[...]), axis=1)
    y = _dot(x3, t3_ref[...])
    tot = _dot(x3, s3_ref[...])
    offs = _dot(l3_ref[...], jnp.concatenate(_split3(tot), axis=0))
    c_ref[...] = y + _dot(jnp.concatenate(_split3(offs), axis=1), e3_ref[...])
    n = tot.shape[0]
    tot_ref[...] = offs[n - 1:n, :] + tot[n - 1:n, :]


def _past_cumsum(lf_pages, heads):
    depth, nb, n_pages, width = lf_pages.shape
    pos = np.arange(width) // heads
    hd = np.arange(width) % heads
    t = ((hd[:, None] == hd[None, :]) & (pos[:, None] <= pos[None, :])).astype(np.float32)
    ssum = (hd[:, None] == np.arange(LANES)[None, :]).astype(np.float32)
    lstrict = np.tril(np.ones((n_pages, n_pages), np.float32), -1)
    e = (np.arange(LANES)[:, None] == hd[None, :]).astype(np.float32)
    t3 = jnp.asarray(np.concatenate([t, t, t], axis=0), BF16)
    s3 = jnp.asarray(np.concatenate([ssum, ssum, ssum], axis=0), BF16)
    l3 = jnp.asarray(np.concatenate([lstrict, lstrict, lstrict], axis=1), BF16)
    e3 = jnp.asarray(np.concatenate([e, e, e], axis=0), BF16)
    x = lf_pages.reshape(depth * nb, n_pages, width)
    full = lambda a: pl.BlockSpec(a.shape, lambda g: (0,) * a.ndim)
    return pl.pallas_call(
        _past_cumsum_kernel,
        grid=(depth * nb,),
        in_specs=[pl.BlockSpec((None, n_pages, width), lambda g: (g, 0, 0)),
                  full(t3), full(s3), full(l3), full(e3)],
        out_specs=[pl.BlockSpec((None, n_pages, width), lambda g: (g, 0, 0)),
                   pl.BlockSpec((None, 1, LANES), lambda g: (g, 0, 0))],
        out_shape=[jax.ShapeDtypeStruct((depth * nb, n_pages, width), F32),
                   jax.ShapeDtypeStruct((depth * nb, 1, LANES), F32)],
        compiler_params=_params("arbitrary"),
        name="past_cumsum",
    )(x, t3, s3, l3, e3)


def _fox_sample_kernel(pt_ref, base_ref, q_ref, kn_ref, vn_ref, ff_ref, b_ref, ctot_ref, cp_ref, *refs,
                       group, heads, hd, nt):
    k_refs = refs[:group]
    v_refs = refs[group:2 * group]
    o_ref, lf_ref = refs[2 * group], refs[2 * group + 1]
    qall_ref, cq_ref, m_ref, l_ref, acc_ref = refs[2 * group + 2:]
    g = pl.program_id(1)
    ng = pl.num_programs(1)
    nrow = heads * nt
    assert heads & (heads - 1) == 0 and nt & (nt - 1) == 0
    hbits = heads.bit_length() - 1
    tbits = nt.bit_length() - 1

    def head_of_col(cols):
        return jnp.bitwise_and(cols, heads - 1)

    def update(s, vb):
        m_prev = m_ref[...]
        m_new = jnp.maximum(m_prev, jnp.max(s, axis=-1, keepdims=True))
        alpha = jnp.exp(m_prev - m_new)
        p = jnp.exp(s - m_new)
        l_ref[...] = alpha * l_ref[...] + jnp.sum(p, axis=-1, keepdims=True)
        acc_ref[...] = alpha * acc_ref[...] + _dot(p.astype(BF16), vb)
        m_ref[...] = m_new

    @pl.when(g == 0)
    def _():
        lf = _log_sigmoid(ff_ref[...] + b_ref[...])
        lf_ref[...] = lf
        trow = lax.broadcasted_iota(jnp.int32, lf.shape, 0)
        tlane = lax.broadcasted_iota(jnp.int32, lf.shape, 1)
        cs = lf
        sh = 1
        while sh < nt:
            cs = cs + jnp.where(trow >= sh, pltpu.roll(cs, sh, 0), 0.0)
            sh *= 2
        cn = ctot_ref[...] + cs
        cq = jnp.concatenate([cn[:, h:h + 1] for h in range(heads)], axis=0)
        cq_ref[...] = cq
        qf = q_ref[...]
        qall = jnp.concatenate([qf[:, h * hd:(h + 1) * hd] for h in range(heads)], axis=0)
        qall_ref[...] = qall.astype(BF16)
        cnm = jnp.where(tlane < heads, cn, 0.0)
        ck = jnp.zeros((1, LANES), F32)
        for t in range(nt):
            rolled = pltpu.roll(cnm, t * heads, 1)
            ck = ck + jnp.sum(jnp.where(trow == t, rolled, 0.0), axis=0, keepdims=True)
        kn = _pad_rows(kn_ref[...], LANES).astype(BF16)
        vn = _pad_rows(vn_ref[...], LANES).astype(BF16)
        s = _dot_nt(qall_ref[...], kn) + cq - ck
        rows = lax.broadcasted_iota(jnp.int32, s.shape, 0)
        cols = lax.broadcasted_iota(jnp.int32, s.shape, 1)
        ok = (head_of_col(cols) == jnp.right_shift(rows, tbits))
        ok = ok & (jnp.right_shift(cols, hbits) <= jnp.bitwise_and(rows, nt - 1)) & (cols < nrow)
        s = jnp.where(ok, s, -jnp.inf)
        m0 = jnp.max(s, axis=-1, keepdims=True)
        p = jnp.exp(s - m0)
        m_ref[...] = m0
        l_ref[...] = jnp.sum(p, axis=-1, keepdims=True)
        acc_ref[...] = _dot(p.astype(BF16), vn)

    prow = k_refs[0].shape[0]
    rows = lax.broadcasted_iota(jnp.int32, (nrow, prow), 0)
    cols = lax.broadcasted_iota(jnp.int32, (nrow, prow), 1)
    same_head = head_of_col(cols) == jnp.right_shift(rows, tbits)
    qall = qall_ref[...]
    cq = cq_ref[...]
    for r in range(group):
        kb = k_refs[r][...].astype(BF16)
        vb = v_refs[r][...].astype(BF16)
        s = _dot_nt(qall, kb) + (cq - cp_ref[r:r + 1, :])
        update(jnp.where(same_head, s, -jnp.inf), vb)

    @pl.when(g == ng - 1)
    def _():
        o = acc_ref[...] / l_ref[...]
        for h in range(heads):
            o_ref[:, h * hd:(h + 1) * hd] = o[h * nt:(h + 1) * nt, :]


def _fox_sample(base, q, k32, v32, ff, b_fox_pad, ctot, c_past, cache_k, cache_v, page_table, layer,
                row0, nt, heads, hd, group):
    nb, n_pages = page_table.shape
    prow = cache_k.shape[2]
    width = heads * hd
    ng = n_pages // group
    rb0 = row0 // nt
    nrow = heads * nt

    def page_map(r):
        return lambda b, g, pt: (layer, pt[b, g * group + r], 0, 0)

    tok_rows = lambda b, g, pt: (rb0 + b, 0)
    kern = functools.partial(_fox_sample_kernel, group=group, heads=heads, hd=hd, nt=nt)
    in_specs = [
        _any_spec(),
        pl.BlockSpec((nt, width), tok_rows),
        pl.BlockSpec((nrow, hd), tok_rows),
        pl.BlockSpec((nrow, hd), tok_rows),
        pl.BlockSpec((nt, LANES), tok_rows),
        pl.BlockSpec((None, 1, LANES), lambda b, g, pt: (layer, 0, 0)),
        pl.BlockSpec((None, 1, LANES), lambda b, g, pt: (layer * nb + b, 0, 0)),
        pl.BlockSpec((None, group, c_past.shape[2]), lambda b, g, pt: (layer * nb + b, g, 0)),
    ]
    in_specs += [pl.BlockSpec((None, None, prow, hd), page_map(r)) for r in range(group)]
    in_specs += [pl.BlockSpec((None, None, prow, hd), page_map(r)) for r in range(group)]
    return pl.pallas_call(
        kern,
        grid_spec=pltpu.PrefetchScalarGridSpec(
            num_scalar_prefetch=1,
            grid=(nb, ng),
            in_specs=in_specs,
            out_specs=[pl.BlockSpec((nt, width), tok_rows),
                       pl.BlockSpec((None, nt, LANES), lambda b, g, pt: (b, 0, 0))],
            scratch_shapes=[
                pltpu.VMEM((nrow, hd), BF16),
                pltpu.VMEM((nrow, 1), F32),
                pltpu.VMEM((nrow, 1), F32),
                pltpu.VMEM((nrow, 1), F32),
                pltpu.VMEM((nrow, hd), F32),
            ],
        ),
        out_shape=[jax.ShapeDtypeStruct(base.shape, F32),
                   jax.ShapeDtypeStruct((nb, nt, LANES), F32)],
        input_output_aliases={1: 0},
        compiler_params=_params("arbitrary", "arbitrary"),
        name="fox_sample",
    )(page_table, base, q, k32, v32, ff, b_fox_pad, ctot, c_past,
      *([cache_k] * group), *([cache_v] * group))


def _hgrn_operators(c, real):
    t = np.arange(c)[:, None]
    j = np.arange(c)[None, :]
    mats = [(j <= t), (j > t)]
    masks = []
    m = 1
    while m < real:
        blk = t // (2 * m)
        mid = blk * (2 * m) + m - 1
        second = t > mid
        mats.append(np.where(second, (j > mid) & (j <= t), (j > t) & (j <= mid)))
        s = np.arange(c)[None, :]
        masks.append((blk == (s // (2 * m))) & second & (s <= (s // (2 * m)) * (2 * m) + m - 1))
        m *= 2
    a = np.concatenate(mats, axis=0).astype(np.float32)
    a3 = np.concatenate([a, a, a], axis=1)
    return jnp.asarray(a3, BF16), jnp.asarray(np.stack(masks).astype(np.float32)), len(masks)


def _hgrn_kernel(*refs, layer, heads, dk, n_lev, real, aliased):
    if aliased:
        refs = refs[1:]
    (q_ref, f_ref, i_ref, g_ref, lbraw_ref, gn_ref, a3_ref, msk_ref, s0_ref,
     o_ref, sfin_ref, st_ref) = refs
    c = HGRN_CHUNK
    ci = pl.program_id(1)
    nc = pl.num_programs(1)

    @pl.when(ci == 0)
    def _():
        for h in range(heads):
            st_ref[h] = s0_ref[h].T

    raw = lbraw_ref[...]
    e = jnp.exp(raw - jnp.max(raw, axis=0, keepdims=True))
    sm = e / jnp.sum(e, axis=0, keepdims=True)
    cs = sm[0:1]
    for r in range(1, layer + 1):
        cs = cs + sm[r:r + 1]
    lb = cs - sm[0:1]
    log_lb = jnp.log(lb)
    log_1m = jnp.log1p(-lb)

    zf = _pad_rows(f_ref[...], c)
    q = _pad_rows(q_ref[...], c)
    v = _pad_rows(i_ref[...], c)
    b = log_1m + _log_sigmoid(zf)
    mx = jnp.maximum(log_lb, b)
    g = mx + jnp.log1p(jnp.exp(-jnp.abs(log_lb - b)))
    kr = (1.0 - lb) * _sigmoid(-zf)
    if real < c:
        valid = lax.broadcasted_iota(jnp.int32, g.shape, 0) < real
        g = jnp.where(valid, g, 0.0)
        kr = jnp.where(valid, kr, 0.0)

    x = jnp.exp(_dot(a3_ref[...], jnp.concatenate(_split3(g), axis=0)))
    xq = x[0:c]
    xk = x[c:2 * c]
    xlast = xq[c - 1:c]
    qe = (q * xq).astype(BF16)
    ke = (kr * xk).astype(BF16)
    vb = v.astype(BF16)
    for h in range(heads):
        hs = slice(h * dk, (h + 1) * dk)
        st = st_ref[h]
        o = _dot_nt(qe[:, hs], st.astype(BF16))
        att = jnp.zeros((c, c), F32)
        for l in range(n_lev):
            xl = x[(2 + l) * c:(3 + l) * c, hs]
            p = _dot_nt((q[:, hs] * xl).astype(BF16), (kr[:, hs] * xl).astype(BF16))
            att = att + p * msk_ref[l]
        o = o + _dot(att.astype(BF16), vb[:, hs])
        o = o + jnp.sum(q[:, hs] * kr[:, hs], axis=-1, keepdims=True) * v[:, hs]
        st_ref[h] = st * xlast[:, hs] + _dot_tn(vb[:, hs], ke[:, hs])
        on = _rms_rows(o, gn_ref[:, hs])
        gg = g_ref[:, hs]
        o_ref[:, hs] = on[:real] * (gg * _sigmoid(gg))

    @pl.when(ci == nc - 1)
    def _():
        for h in range(heads):
            sfin_ref[h] = st_ref[h].T


def _hgrn(zr, hgrn_lb, hgrn_norm, s0, layer, row0, n_seq, seq, heads, dk, base=None):
    c = HGRN_CHUNK
    real = min(seq, c)
    assert seq % real == 0
    nc = seq // real
    width = heads * dk
    rb0 = row0 // real
    a3, masks, n_lev = _hgrn_operators(c, real)
    depth = hgrn_lb.shape[0]
    blk = lambda col: pl.BlockSpec((real, width), lambda n, i: (rb0 + n * nc + i, col))
    aliased = base is not None
    kern = functools.partial(_hgrn_kernel, layer=layer, heads=heads, dk=dk, n_lev=n_lev, real=real,
                             aliased=aliased)
    in_specs = [
        blk(0), blk(1), blk(2), blk(3),
        pl.BlockSpec((depth, width), lambda n, i: (0, 0)),
        pl.BlockSpec((None, 1, width), lambda n, i: (layer, 0, 0)),
        pl.BlockSpec(a3.shape, lambda n, i: (0, 0)),
        pl.BlockSpec(masks.shape, lambda n, i: (0, 0, 0)),
        pl.BlockSpec((None, heads, dk, dk), lambda n, i: (n, 0, 0, 0)),
    ]
    args = [zr, zr, zr, zr, hgrn_lb, hgrn_norm, a3, masks, s0]
    if aliased:
        in_specs = [_any_spec()] + in_specs
        args = [base] + args
    return pl.pallas_call(
        kern,
        grid=(n_seq, nc),
        in_specs=in_specs,
        out_specs=[
            blk(0),
            pl.BlockSpec((None, heads, dk, dk), lambda n, i: (n, 0, 0, 0)),
        ],
        out_shape=[
            jax.ShapeDtypeStruct((zr.shape[0], width), F32),
            jax.ShapeDtypeStruct((n_seq, heads, dk, dk), F32),
        ],
        scratch_shapes=[pltpu.VMEM((heads, dk, dk), F32)],
        input_output_aliases={0: 0} if aliased else {},
        compiler_params=_params("arbitrary", "arbitrary"),
        name="hgrn",
    )(*args)


def _merge_kernel(ma_ref, ob_ref, oc_ref, wa_ref, wb_ref, wc_ref, ga_ref, gb_ref, gc_ref, mix_ref):
    ya = _dot(ma_ref[...].astype(BF16), wa_ref[...])
    yb = _dot(ob_ref[...].astype(BF16), wb_ref[...])
    yc = _dot(oc_ref[...].astype(BF16), wc_ref[...])
    gate = lambda r: _sigmoid(r[...].astype(F32))
    mix = gate(ga_ref) * ya + gate(gb_ref) * yb + gate(gc_ref) * yc
    mix_ref[...] = mix.astype(mix_ref.dtype)


def _merge(m_a, o_b, o_c, w_a, w_b, w_c, zg, layer, tm, tn):
    m, k = m_a.shape
    d = w_a.shape[2]
    nj = d // tn
    act = pl.BlockSpec((tm, k), lambda i, j: (i, 0))
    wsp = pl.BlockSpec((None, k, tn), lambda i, j: (layer, 0, j))
    gate = lambda r: pl.BlockSpec((tm, tn), lambda i, j: (i, r * nj + j))
    return pl.pallas_call(
        _merge_kernel,
        grid=(m // tm, nj),
        in_specs=[act, act, act, wsp, wsp, wsp, gate(0), gate(1), gate(2)],
        out_specs=pl.BlockSpec((tm, tn), lambda i, j: (i, j)),
        out_shape=jax.ShapeDtypeStruct((m, d), BF16),
        compiler_params=_params("arbitrary", "arbitrary"),
        name="merge",
    )(m_a, o_b, o_c, w_a, w_b, w_c, zg, zg, zg)


def _residual_matmul_kernel(a_ref, w_ref, x_ref, o_ref):
    o_ref[...] = x_ref[...] + _dot(a_ref[...], w_ref[...])


def _residual_matmul(a, w, x, layer, tm, tn):
    m, k = a.shape
    d = w.shape[2]
    return pl.pallas_call(
        _residual_matmul_kernel,
        grid=(m // tm, d // tn),
        in_specs=[
            pl.BlockSpec((tm, k), lambda i, j: (i, 0)),
            pl.BlockSpec((None, k, tn), lambda i, j: (layer, 0, j)),
            pl.BlockSpec((tm, tn), lambda i, j: (i, j)),
        ],
        out_specs=pl.BlockSpec((tm, tn), lambda i, j: (i, j)),
        out_shape=jax.ShapeDtypeStruct((m, d), F32),
        compiler_params=_params("arbitrary", "arbitrary"),
        name="residual_matmul",
    )(a, w, x)


def _swiglu_up_kernel(x_ref, g_ref, wg_ref, wu_ref, a_ref, h_ref):
    @pl.when(pl.program_id(1) == 0)
    def _():
        h_ref[...] = _rms_rows(x_ref[...], g_ref[...]).astype(BF16)

    h = h_ref[...]
    gate = _dot(h, wg_ref[...])
    up = _dot(h, wu_ref[...])
    a_ref[...] = (gate * _sigmoid(gate) * up).astype(a_ref.dtype)


def _swiglu_up(x, norm, w_gate_up, layer, tm, tn):
    m, d = x.shape
    f = w_gate_up.shape[2] // 2
    nj = f // tn
    return pl.pallas_call(
        _swiglu_up_kernel,
        grid=(m // tm, nj),
        in_specs=[
            pl.BlockSpec((tm, d), lambda i, j: (i, 0)),
            pl.BlockSpec((None, 1, d), lambda i, j: (layer, 0, 0)),
            pl.BlockSpec((None, d, tn), lambda i, j: (layer, 0, j)),
            pl.BlockSpec((None, d, tn), lambda i, j: (layer, 0, nj + j)),
        ],
        out_specs=pl.BlockSpec((tm, tn), lambda i, j: (i, j)),
        out_shape=jax.ShapeDtypeStruct((m, f), BF16),
        scratch_shapes=[pltpu.VMEM((tm, d), BF16)],
        compiler_params=_params("arbitrary", "arbitrary"),
        name="swiglu_up",
    )(x, norm, w_gate_up, w_gate_up)


def _ple_kernel(x_ref, g_ref, wg_ref, p_ref, wp_ref, o_ref, h_ref, *, tn):
    j = pl.program_id(1)

    @pl.when(j == 0)
    def _():
        h_ref[...] = _rms_rows(x_ref[...], g_ref[...]).astype(BF16)

    gate = _sigmoid(_dot(h_ref[...], wg_ref[...]))
    emb = _dot(p_ref[...].astype(BF16), wp_ref[...])
    xs = x_ref[:, pl.ds(pl.multiple_of(j * tn, tn), tn)]
    o_ref[...] = xs + gate * emb


def _ple(x, norm, w_gate, p, w_proj, layer, tm, tn):
    m, d = x.shape
    pd = p.shape[2]
    return pl.pallas_call(
        functools.partial(_ple_kernel, tn=tn),
        grid=(m // tm, d // tn),
        in_specs=[
            pl.BlockSpec((tm, d), lambda i, j: (i, 0)),
            pl.BlockSpec((None, 1, d), lambda i, j: (layer, 0, 0)),
            pl.BlockSpec((None, d, tn), lambda i, j: (layer, 0, j)),
            pl.BlockSpec((None, tm, pd), lambda i, j: (layer, i, 0)),
            pl.BlockSpec((None, pd, tn), lambda i, j: (layer, 0, j)),
        ],
        out_specs=pl.BlockSpec((tm, tn), lambda i, j: (i, j)),
        out_shape=jax.ShapeDtypeStruct((m, d), F32),
        scratch_shapes=[pltpu.VMEM((tm, d), BF16)],
        compiler_params=_params("arbitrary", "arbitrary"),
        name="ple",
    )(x, norm, w_gate, p, w_proj)


def _final_norm_kernel(x_ref, g_ref, o_ref):
    o_ref[...] = _rms_rows(x_ref[...], g_ref[...])


def _final_norm(x, g, tm):
    m, d = x.shape
    return pl.pallas_call(
        _final_norm_kernel,
        grid=(m // tm,),
        in_specs=[pl.BlockSpec((tm, d), lambda i: (i, 0)), pl.BlockSpec((1, d), lambda i: (0, 0))],
        out_specs=pl.BlockSpec((tm, d), lambda i: (i, 0)),
        out_shape=jax.ShapeDtypeStruct((m, d), F32),
        compiler_params=_params("arbitrary"),
        name="final_norm",
    )(x, g)


def kernel(x_prompt, x_sample, cache_k, cache_v, cache_logf, state_conv, state_hgrn, page_table,
           p_prompt, p_sample, norm1, w_in, b_fox, conv_w, hgrn_lb, hgrn_norm, w_a, w_b, w_c, w_o,
           norm2, w_gate_up, w_down, ple_norm, ple_gate, ple_proj, final_norm):
    bp, seq, d = x_prompt.shape
    db, dseq, _ = x_sample.shape
    depth = w_in.shape[0]
    heads, hd = cache_k.shape[3], cache_k.shape[4]
    page = cache_k.shape[2]
    n_pool = cache_k.shape[1]
    cw = w_a.shape[1]
    fw = w_b.shape[1]
    rw = w_c.shape[1]
    rheads = state_hgrn.shape[2]
    dk = state_hgrn.shape[3]
    assert fw == heads * hd and rw == rheads * dk and state_hgrn.shape[4] == dk
    mp = bp * seq
    ms = db * dseq
    m = mp + ms

    o_ff = 3 * cw + 3 * fw
    widths = (3 * cw, fw, fw, fw, 4 * rw, 3 * d)
    w_main = jnp.concatenate([w_in[:, :, :o_ff], w_in[:, :, o_ff + heads:]], axis=2).astype(BF16)
    w_ff = jnp.pad(w_in[:, :, o_ff:o_ff + heads], ((0, 0), (0, 0), (0, LANES - heads))).astype(BF16)
    assert w_main.shape[2] == sum(widths)

    bf = lambda a: a.astype(BF16)
    w_a_b, w_b_b, w_c_b, w_o_b = bf(w_a), bf(w_b), bf(w_c), bf(w_o)
    w_gu_b, w_dn_b, w_pg_b, w_pp_b = bf(w_gate_up), bf(w_down), bf(ple_gate), bf(ple_proj)
    row3 = lambda a: a.reshape(a.shape[0], 1, a.shape[1])
    norm1_r, norm2_r, ple_norm_r, hnorm_r = row3(norm1), row3(norm2), row3(ple_norm), row3(hgrn_norm)
    b_fox_pad = row3(jnp.pad(b_fox, ((0, 0), (0, LANES - heads))))

    x = jnp.concatenate([x_prompt.reshape(mp, d), x_sample.reshape(ms, d)], axis=0)
    p_all = jnp.concatenate([p_prompt.reshape(depth, mp, -1), p_sample.reshape(depth, ms, -1)], axis=1)
    ck = cache_k.reshape(depth, n_pool, page * heads, hd)
    cv = cache_v.reshape(depth, n_pool, page * heads, hd)

    n_pages = page_table.shape[1]
    group = _tile(n_pages, 8, 1)
    lf_flat = cache_logf.reshape(depth, n_pool, 1, page * heads)
    lf_pages = _gather_logf_pages(lf_flat, page_table, group)
    c_past, c_tot = _past_cumsum(lf_pages, heads)

    tm = _tile(m, 704, 16)
    tq = _tile(seq, 512, LANES)
    conv_tile = _tile(seq, 512, 8)
    zeros_conv = jnp.zeros((bp, 2, cw), F32)
    zeros_state = jnp.zeros((bp, rheads, dk, dk), F32)

    outs = {k: [] for k in ("kp", "vp", "lfp", "cp", "sp", "ks", "vs", "lfs", "cs", "ss")}
    for i in range(depth):
        zc, q, k32, kb, v32, vb, zr, zg, ff = _in_proj(x, norm1_r, w_main, w_ff, i, tm, 512,
                                                      widths, heads, hd)

        m_a, conv_p = _conv(zc, conv_w, zeros_conv, i, 0, bp, seq, conv_tile, cw)
        m_a, conv_s = _conv(zc, conv_w, state_conv[i], i, mp, db, dseq, dseq, cw, base=m_a)

        lf_p, qbias, kbias = _logf_prompt(ff, b_fox_pad, i, bp, seq, heads, fw)
        o_b = _fox_prompt(q, qbias, kb, kbias, vb, bp, seq, heads, hd, tq)
        o_b, lf_s = _fox_sample(o_b, q, k32, v32, ff, b_fox_pad, c_tot, c_past, ck, cv, page_table, i,
                                mp, dseq, heads, hd, group)

        o_c, st_p = _hgrn(zr, hgrn_lb, hnorm_r, zeros_state, i, 0, bp, seq, rheads, dk)
        o_c, st_s = _hgrn(zr, hgrn_lb, hnorm_r, state_hgrn[i], i, mp, db, dseq, rheads, dk, base=o_c)

        mix = _merge(m_a, o_b, o_c, w_a_b, w_b_b, w_c_b, zg, i, tm, 512)
        x = _residual_matmul(mix, w_o_b, x, i, tm, 1024)

        act = _swiglu_up(x, norm2_r, w_gu_b, i, tm, _tile(w_gate_up.shape[2] // 2, 512, LANES))
        x = _residual_matmul(act, w_dn_b, x, i, tm, 512)
        x = _ple(x, ple_norm_r, w_pg_b, p_all, w_pp_b, i, tm, 1024)

        outs["kp"].append(k32[:mp * heads].reshape(bp, seq, heads, hd))
        outs["vp"].append(v32[:mp * heads].reshape(bp, seq, heads, hd))
        outs["ks"].append(k32[mp * heads:].reshape(db, dseq, heads, hd))
        outs["vs"].append(v32[mp * heads:].reshape(db, dseq, heads, hd))
        outs["lfp"].append(lf_p[:, :, :heads])
        outs["lfs"].append(lf_s[:, :, :heads])
        outs["cp"].append(conv_p)
        outs["cs"].append(conv_s)
        outs["sp"].append(st_p)
        outs["ss"].append(st_s)

    y = _final_norm(x, final_norm.reshape(1, d), tm)
    st = lambda k: jnp.stack(outs[k])
    return (y[:mp].reshape(bp, seq, d), y[mp:].reshape(db, dseq, d),
            st("kp"), st("vp"), st("lfp"), st("cp"), st("sp"),
            st("ks"), st("vs"), st("lfs"), st("cs"), st("ss"))
```

```python
import functools

import numpy as np
import jax
import jax.numpy as jnp
from jax import lax
from jax.experimental import pallas as pl
from jax.experimental.pallas import tpu as pltpu

F32 = jnp.float32
BF16 = jnp.bfloat16
RMS_EPS = 1e-6
LANES = 128
HGRN_CHUNK = 128
VMEM_LIMIT = 56 * 1024 * 1024


def _tile(n, target, align):
    best = None
    for t in range(align, min(n, target) + 1, align):
        if n % t == 0:
            best = t
    assert best is not None, (n, target, align)
    return best


def _params(*sem):
    return pltpu.CompilerParams(dimension_semantics=sem, vmem_limit_bytes=VMEM_LIMIT)


def _sigmoid(x):
    return 1.0 / (1.0 + jnp.exp(-x))


def _log_sigmoid(x):
    return jnp.minimum(x, 0.0) - jnp.log1p(jnp.exp(-jnp.abs(x)))


def _rms_rows(x, g):
    ms = jnp.mean(x * x, axis=-1, keepdims=True)
    return x * lax.rsqrt(ms + RMS_EPS) * g


def _split3(x):
    hi = x.astype(BF16)
    r1 = x - hi.astype(F32)
    mid = r1.astype(BF16)
    lo = (r1 - mid.astype(F32)).astype(BF16)
    return hi, mid, lo


def _dot(a, b):
    return jnp.dot(a, b, preferred_element_type=F32)


def _dot_nt(a, b):
    return lax.dot_general(a, b, (((1,), (1,)), ((), ())), preferred_element_type=F32)


def _dot_tn(a, b):
    return lax.dot_general(a, b, (((0,), (0,)), ((), ())), preferred_element_type=F32)


def _pad_rows(a, n):
    if a.shape[0] == n:
        return a
    return jnp.concatenate([a, jnp.zeros((n - a.shape[0], a.shape[1]), a.dtype)], axis=0)


def _any_spec():
    return pl.BlockSpec(memory_space=pl.ANY)


def _copy_kernel(*refs, aliased):
    a_ref, o_ref = refs[1:] if aliased else refs
    o_ref[...] = a_ref[...]


def _stack_rows(a, b):
    nl, ra, c = a.shape
    rb = b.shape[1]
    assert ra % rb == 0
    ta = _tile(ra, 1024, 8)
    shape = jax.ShapeDtypeStruct((nl, ra + rb, c), a.dtype)
    out = pl.pallas_call(
        functools.partial(_copy_kernel, aliased=False),
        grid=(nl, ra // ta),
        in_specs=[pl.BlockSpec((None, ta, c), lambda l, i: (l, i, 0))],
        out_specs=pl.BlockSpec((None, ta, c), lambda l, i: (l, i, 0)),
        out_shape=shape,
        compiler_params=_params("arbitrary", "arbitrary"),
        name="stack_rows_a",
    )(a)
    return pl.pallas_call(
        functools.partial(_copy_kernel, aliased=True),
        grid=(nl,),
        in_specs=[_any_spec(), pl.BlockSpec((None, rb, c), lambda l: (l, 0, 0))],
        out_specs=pl.BlockSpec((None, rb, c), lambda l: (l, ra // rb, 0)),
        out_shape=shape,
        input_output_aliases={0: 0},
        compiler_params=_params("arbitrary"),
        name="stack_rows_b",
    )(out, b)


def _in_proj_kernel(x_ref, g_ref, w_ref, wff_ref, zf_ref, kvb_ref, zg_ref, ff_ref, h_ref,
                    *, bounds, scale):
    j = pl.program_id(1)
    jq, jk, jr, jg = bounds

    @pl.when(j == 0)
    def _():
        h = _rms_rows(x_ref[...], g_ref[...]).astype(BF16)
        h_ref[...] = h
        ff_ref[...] = _dot(h, wff_ref[...])

    def mm():
        return _dot(h_ref[...], w_ref[...])

    @pl.when((j < jq) | ((j >= jr) & (j < jg)))
    def _():
        zf_ref[...] = mm()

    @pl.when((j >= jq) & (j < jk))
    def _():
        zf_ref[...] = mm() * scale

    @pl.when((j >= jk) & (j < jr))
    def _():
        z = mm()
        zf_ref[...] = z
        kvb_ref[...] = z.astype(BF16)

    @pl.when(j >= jg)
    def _():
        zg_ref[...] = mm().astype(BF16)


def _in_proj(x, norm, w_main, w_ff, layer, tm, tn, widths, hd):
    m, d = x.shape
    n = w_main.shape[2]
    assert all(w % tn == 0 for w in widths) and sum(widths) == n
    cuts = np.cumsum([w // tn for w in widths])
    jq, jk, jr, jg, nj = (int(c) for c in cuts)
    kern = functools.partial(_in_proj_kernel, bounds=(jq, jk, jr, jg), scale=hd ** -0.5)
    return pl.pallas_call(
        kern,
        grid=(m // tm, nj),
        in_specs=[
            pl.BlockSpec((tm, d), lambda i, j: (i, 0), pipeline_mode=pl.Buffered(1)),
            pl.BlockSpec((None, 1, d), lambda i, j: (layer, 0, 0)),
            pl.BlockSpec((None, d, tn), lambda i, j: (layer, 0, j)),
            pl.BlockSpec((None, d, LANES), lambda i, j: (layer, 0, 0)),
        ],
        out_specs=[
            pl.BlockSpec((tm, tn), lambda i, j: (i, jnp.minimum(j, jg - 1))),
            pl.BlockSpec((tm, tn), lambda i, j: (i, jnp.clip(j - jk, 0, jr - jk - 1))),
            pl.BlockSpec((tm, tn), lambda i, j: (i, jnp.clip(j - jg, 0, nj - jg - 1))),
            pl.BlockSpec((tm, LANES), lambda i, j: (i, 0)),
        ],
        out_shape=[
            jax.ShapeDtypeStruct((m, jg * tn), F32),
            jax.ShapeDtypeStruct((m, widths[2]), BF16),
            jax.ShapeDtypeStruct((m, widths[4]), BF16),
            jax.ShapeDtypeStruct((m, LANES), F32),
        ],
        scratch_shapes=[pltpu.VMEM((tm, d), BF16)],
        compiler_params=_params("arbitrary", "arbitrary"),
        name="in_proj",
    )(x, norm, w_main, w_ff)


def _bias_operators(heads, width):
    hd = width // heads
    selq = np.zeros((3 * LANES, width), np.float32)
    selk = np.zeros((3 * LANES, width), np.float32)
    cq = np.zeros((1, width), np.float32)
    ck = np.zeros((1, width), np.float32)
    for h in range(heads):
        for r in range(3):
            selq[r * LANES + h, h * hd + r] = 1.0
            selk[r * LANES + h, h * hd + 3 + r] = -1.0
            cq[0, h * hd + 3 + r] = 1.0
            ck[0, h * hd + r] = 1.0
    return jnp.asarray(selq, BF16), jnp.asarray(selk, BF16), jnp.asarray(cq), jnp.asarray(ck)


def _logf_prompt_kernel(*refs, heads, hd, aliased):
    if aliased:
        refs = refs[2:]
    (ff_ref, b_ref, q_ref, k_ref, k32_ref, v32_ref, tri_ref, selq_ref, selk_ref, cq_ref, ck_ref,
     lf_ref, qa_ref, ka_ref, kp_ref, vp_ref, carry_ref) = refs
    blk = ff_ref.shape[0]
    for h in range(heads):
        kp_ref[pl.ds(h, blk, stride=heads), :] = k32_ref[:, h * hd:(h + 1) * hd]
        vp_ref[pl.ds(h, blk, stride=heads), :] = v32_ref[:, h * hd:(h + 1) * hd]

    @pl.when(pl.program_id(1) == 0)
    def _():
        carry_ref[...] = jnp.zeros(carry_ref.shape, F32)

    lf = _log_sigmoid(ff_ref[...] + b_ref[...])
    lf_ref[...] = lf
    cb = _dot(tri_ref[...], jnp.concatenate(_split3(lf), axis=0)) + carry_ref[...]
    carry_ref[...] = cb[blk - 1:blk, :]
    c3 = jnp.concatenate(_split3(cb), axis=1)
    qb = (_dot(c3, selq_ref[...]) + cq_ref[...]).astype(BF16)
    kb = (_dot(c3, selk_ref[...]) + ck_ref[...]).astype(BF16)
    q = q_ref[...].astype(BF16)
    k = k_ref[...]
    for h in range(heads):
        hs = slice(h * hd, (h + 1) * hd)
        qa_ref[:, 2 * h * hd:(2 * h + 1) * hd] = q[:, hs]
        qa_ref[:, (2 * h + 1) * hd:(2 * h + 2) * hd] = qb[:, hs]
        ka_ref[:, 2 * h * hd:(2 * h + 1) * hd] = k[:, hs]
        ka_ref[:, (2 * h + 1) * hd:(2 * h + 2) * hd] = kb[:, hs]


def _logf_prompt(ff, b_fox_pad, zf, kvb, layer, depth, batch, seq, heads, hd, qcol, prev):
    width = heads * hd
    blk = _tile(seq, 512, LANES)
    nb = seq // blk
    tri = np.tril(np.ones((blk, blk), np.float32))
    tri3 = jnp.asarray(np.concatenate([tri, tri, tri], axis=1), BF16)
    ops = _bias_operators(heads, width)
    full = lambda a: pl.BlockSpec(a.shape, lambda b, i: (0,) * a.ndim)
    rows = lambda c, w: pl.BlockSpec((blk, w), lambda b, i: (b * nb + i, c))
    cache = pl.BlockSpec((None, blk * heads, hd), lambda b, i: (layer, b * nb + i, 0))
    aliased = prev is not None
    in_specs = [
        rows(0, LANES),
        pl.BlockSpec((None, 1, LANES), lambda b, i: (layer, 0, 0)),
        rows(qcol, width),
        rows(0, width),
        rows(qcol + 1, width),
        rows(qcol + 2, width),
        full(tri3), full(ops[0]), full(ops[1]), full(ops[2]), full(ops[3]),
    ]
    args = [ff, b_fox_pad, zf, kvb, zf, zf, tri3, *ops]
    if aliased:
        in_specs = [_any_spec(), _any_spec()] + in_specs
        args = list(prev) + args
    cache_shape = jax.ShapeDtypeStruct((depth, batch * seq * heads, hd), F32)
    return pl.pallas_call(
        functools.partial(_logf_prompt_kernel, heads=heads, hd=hd, aliased=aliased),
        grid=(batch, nb),
        in_specs=in_specs,
        out_specs=[
            pl.BlockSpec((None, blk, LANES), lambda b, i: (b, i, 0)),
            rows(0, 2 * width),
            rows(0, 2 * width),
            cache,
            cache,
        ],
        out_shape=[
            jax.ShapeDtypeStruct((batch, seq, LANES), F32),
            jax.ShapeDtypeStruct((batch * seq, 2 * width), BF16),
            jax.ShapeDtypeStruct((batch * seq, 2 * width), BF16),
            cache_shape,
            cache_shape,
        ],
        scratch_shapes=[pltpu.VMEM((1, LANES), F32)],
        input_output_aliases={0: 3, 1: 4} if aliased else {},
        compiler_params=_params("arbitrary", "arbitrary"),
        name="logf_prompt",
    )(*args)


def _conv_kernel(*refs, aliased):
    if aliased:
        refs = refs[1:]
    ab_ref, ac_ref, av_ref, w_ref, prev_ref, m_ref, newc_ref, carry_ref = refs
    i = pl.program_id(1)
    t = ac_ref.shape[0]

    @pl.when(i == 0)
    def _():
        carry_ref[...] = prev_ref[...]

    u = ac_ref[...] * av_ref[...]
    row = lax.broadcasted_iota(jnp.int32, u.shape, 0)
    p1 = carry_ref[1:2, :]
    p2 = carry_ref[0:1, :]
    u1 = jnp.where(row == 0, p1, pltpu.roll(u, 1, 0))
    u2 = pltpu.roll(u, 2, 0)
    u2 = jnp.where(row == 0, p2, jnp.where(row == 1, p1, u2))
    w = w_ref[...]
    conv = w[0:1, :] * u2 + w[1:2, :] * u1 + w[2:3, :] * u
    m_ref[...] = ab_ref[...] * conv
    last2 = u[t - 2:t, :]
    carry_ref[...] = last2
    newc_ref[...] = last2


def _conv(zf, conv_w, prev, layer, row0, n_seq, seq, tile, width, base=None):
    nt = seq // tile
    rb0 = row0 // tile
    rows = lambda c: pl.BlockSpec((tile, width), lambda n, i: (rb0 + n * nt + i, c))
    aliased = base is not None
    in_specs = [rows(0), rows(1), rows(2),
                pl.BlockSpec((None, 3, width), lambda n, i: (layer, 0, 0)),
                pl.BlockSpec((None, 2, width), lambda n, i: (n, 0, 0))]
    args = [zf, zf, zf, conv_w, prev]
    if aliased:
        in_specs = [_any_spec()] + in_specs
        args = [base] + args
    return pl.pallas_call(
        functools.partial(_conv_kernel, aliased=aliased),
        grid=(n_seq, nt),
        in_specs=in_specs,
        out_specs=[
            rows(0),
            pl.BlockSpec((None, 2, width), lambda n, i: (n, 0, 0)),
        ],
        out_shape=[
            jax.ShapeDtypeStruct((zf.shape[0], width), F32),
            jax.ShapeDtypeStruct((n_seq, 2, width), F32),
        ],
        scratch_shapes=[pltpu.VMEM((2, width), F32)],
        input_output_aliases={0: 0} if aliased else {},
        compiler_params=_params("arbitrary", "arbitrary"),
        name="gated_conv",
    )(*args)


def _fox_prompt_kernel(qa_ref, ka_ref, v_ref, o_ref, m_ref, l_ref, acc_ref, *, heads, hd):
    qi = pl.program_id(1)
    ki = pl.program_id(2)
    tq = qa_ref.shape[0]
    tk = ka_ref.shape[0]

    @pl.when(ki == 0)
    def _():
        m_ref[...] = jnp.full(m_ref.shape, -jnp.inf, F32)
        l_ref[...] = jnp.zeros(l_ref.shape, F32)
        acc_ref[...] = jnp.zeros(acc_ref.shape, F32)

    def step(masked):
        if masked:
            krow = lax.broadcasted_iota(jnp.int32, (tk, tq), 0)
            qcol = lax.broadcasted_iota(jnp.int32, (tk, tq), 1)
            keep = krow <= qcol
        for h in range(heads):
            aug = slice(2 * h * hd, (2 * h + 2) * hd)
            st = _dot_nt(ka_ref[:, aug], qa_ref[:, aug])
            if masked:
                st = jnp.where(keep, st, -jnp.inf)
            m_prev = m_ref[h]
            m_new = jnp.maximum(m_prev, jnp.max(st, axis=0, keepdims=True))
            alpha = jnp.exp(m_prev - m_new)
            p = jnp.exp(st - m_new)
            l_ref[h] = alpha * l_ref[h] + jnp.sum(p, axis=0, keepdims=True)
            acc_ref[h] = alpha * acc_ref[h] + _dot_tn(v_ref[:, h * hd:(h + 1) * hd], p.astype(BF16))
            m_ref[h] = m_new

    @pl.when(ki < qi)
    def _():
        step(False)

    @pl.when(ki == qi)
    def _():
        step(True)
        for h in range(heads):
            o_ref[:, h * hd:(h + 1) * hd] = (acc_ref[h] / l_ref[h]).T


def _fox_prompt(qaug, kaug, kvb, m_rows, batch, seq, heads, hd, tq):
    nq = seq // tq
    width = heads * hd
    kern = functools.partial(_fox_prompt_kernel, heads=heads, hd=hd)
    qmap = lambda b, qi, ki: (b * nq + qi, 0)
    kmap = lambda b, qi, ki: (b * nq + jnp.minimum(ki, qi), 0)
    vmap = lambda b, qi, ki: (b * nq + jnp.minimum(ki, qi), 1)
    return pl.pallas_call(
        kern,
        grid=(batch, nq, nq),
        in_specs=[
            pl.BlockSpec((tq, 2 * width), qmap),
            pl.BlockSpec((tq, 2 * width), kmap),
            pl.BlockSpec((tq, width), vmap),
        ],
        out_specs=pl.BlockSpec((tq, width), qmap),
        out_shape=jax.ShapeDtypeStruct((m_rows, width), F32),
        scratch_shapes=[
            pltpu.VMEM((heads, 1, tq), F32),
            pltpu.VMEM((heads, 1, tq), F32),
            pltpu.VMEM((heads, hd, tq), F32),
        ],
        compiler_params=_params("arbitrary", "arbitrary", "arbitrary"),
        name="fox_prompt",
    )(qaug, kaug, kvb)


def _gather_rows_kernel(pt_ref, *refs):
    n_in = len(refs) - 1
    out_ref = refs[-1]
    for r in range(n_in):
        out_ref[:, r, :] = refs[r][:, 0, :]


def _gather_logf_pages(cache_logf_flat, page_table, group):
    depth, _, _, width = cache_logf_flat.shape
    nb, n_pages = page_table.shape
    ng = n_pages // group

    def in_map(r):
        return lambda b, g, pt: (0, pt[b, g * group + r], 0, 0)

    return pl.pallas_call(
        _gather_rows_kernel,
        grid_spec=pltpu.PrefetchScalarGridSpec(
            num_scalar_prefetch=1,
            grid=(nb, ng),
            in_specs=[pl.BlockSpec((depth, None, 1, width), in_map(r)) for r in range(group)],
            out_specs=pl.BlockSpec((depth, None, group, width), lambda b, g, pt: (0, b, g, 0)),
        ),
        out_shape=jax.ShapeDtypeStruct((depth, nb, n_pages, width), F32),
        compiler_params=_params("arbitrary", "arbitrary"),
        name="gather_logf",
    )(page_table, *([cache_logf_flat] * group))


def _past_cumsum_kernel(x_ref, t3_ref, s3_ref, l3_ref, e3_ref, c_ref, tot_ref):
    x3 = jnp.concatenate(_split3(x_ref[...]), axis=1)
    y = _dot(x3, t3_ref[...])
    tot = _dot(x3, s3_ref[...])
    offs = _dot(l3_ref[...], jnp.concatenate(_split3(tot), axis=0))
    c_ref[...] = y + _dot(jnp.concatenate(_split3(offs), axis=1), e3_ref[...])
    n = tot.shape[0]
    tot_ref[...] = offs[n - 1:n, :] + tot[n - 1:n, :]


def _past_cumsum(lf_pages, heads):
    depth, nb, n_pages, width = lf_pages.shape
    pos = np.arange(width) // heads
    hd = np.arange(width) % heads
    t = ((hd[:, None] == hd[None, :]) & (pos[:, None] <= pos[None, :])).astype(np.float32)
    ssum = (hd[:, None] == np.arange(LANES)[None, :]).astype(np.float32)
    lstrict = np.tril(np.ones((n_pages, n_pages), np.float32), -1)
    e = (np.arange(LANES)[:, None] == hd[None, :]).astype(np.float32)
    t3 = jnp.asarray(np.concatenate([t, t, t], axis=0), BF16)
    s3 = jnp.asarray(np.concatenate([ssum, ssum, ssum], axis=0), BF16)
    l3 = jnp.asarray(np.concatenate([lstrict, lstrict, lstrict], axis=1), BF16)
    e3 = jnp.asarray(np.concatenate([e, e, e], axis=0), BF16)
    x = lf_pages.reshape(depth * nb, n_pages, width)
    full = lambda a: pl.BlockSpec(a.shape, lambda g: (0,) * a.ndim)
    return pl.pallas_call(
        _past_cumsum_kernel,
        grid=(depth * nb,),
        in_specs=[pl.BlockSpec((None, n_pages, width), lambda g: (g, 0, 0)),
                  full(t3), full(s3), full(l3), full(e3)],
        out_specs=[pl.BlockSpec((None, n_pages, width), lambda g: (g, 0, 0)),
                   pl.BlockSpec((None, 1, LANES), lambda g: (g, 0, 0))],
        out_shape=[jax.ShapeDtypeStruct((depth * nb, n_pages, width), F32),
                   jax.ShapeDtypeStruct((depth * nb, 1, LANES), F32)],
        compiler_params=_params("arbitrary"),
        name="past_cumsum",
    )(x, t3, s3, l3, e3)


def _fox_sample_kernel(pt_ref, base_ref, *refs, group, heads, hd, nt, aliased):
    if aliased:
        refs = refs[2:]
    q_ref, kn_ref, vn_ref, ff_ref, b_ref, ctot_ref, cp_ref = refs[:7]
    refs = refs[7:]
    k_refs = refs[:group]
    v_refs = refs[group:2 * group]
    o_ref, lf_ref, ks_ref, vs_ref = refs[2 * group:2 * group + 4]
    qall_ref, cqm_ref, m_ref, l_ref, acc_ref = refs[2 * group + 4:]
    g = pl.program_id(1)
    ng = pl.num_programs(1)
    nrow = heads * nt
    prow = k_refs[0].shape[0]
    assert heads & (heads - 1) == 0 and nt & (nt - 1) == 0
    hbits = heads.bit_length() - 1
    tbits = nt.bit_length() - 1

    def head_of_col(cols):
        return jnp.bitwise_and(cols, heads - 1)

    @pl.when(g == 0)
    def _():
        lf = _log_sigmoid(ff_ref[...] + b_ref[...])
        lf_ref[...] = lf
        trow = lax.broadcasted_iota(jnp.int32, lf.shape, 0)
        tlane = lax.broadcasted_iota(jnp.int32, lf.shape, 1)
        cs = lf
        sh = 1
        while sh < nt:
            cs = cs + jnp.where(trow >= sh, pltpu.roll(cs, sh, 0), 0.0)
            sh *= 2
        cn = ctot_ref[...] + cs
        cq = jnp.concatenate([cn[:, h:h + 1] for h in range(heads)], axis=0)
        prows = lax.broadcasted_iota(jnp.int32, (nrow, prow), 0)
        pcols = lax.broadcasted_iota(jnp.int32, (nrow, prow), 1)
        cqm_ref[...] = jnp.where(head_of_col(pcols) == jnp.right_shift(prows, tbits), cq, -jnp.inf)
        qf = q_ref[...]
        qall = jnp.concatenate([qf[:, h * hd:(h + 1) * hd] for h in range(heads)], axis=0)
        qall_ref[...] = qall.astype(BF16)
        cnm = jnp.where(tlane < heads, cn, 0.0)
        ck = jnp.zeros((1, LANES), F32)
        for t in range(nt):
            rolled = pltpu.roll(cnm, t * heads, 1)
            ck = ck + jnp.sum(jnp.where(trow == t, rolled, 0.0), axis=0, keepdims=True)
        for h in range(heads):
            ks_ref[pl.ds(h, nt, stride=heads), :] = kn_ref[:, h * hd:(h + 1) * hd]
            vs_ref[pl.ds(h, nt, stride=heads), :] = vn_ref[:, h * hd:(h + 1) * hd]
        kn = _pad_rows(ks_ref[...], LANES).astype(BF16)
        vn = _pad_rows(vs_ref[...], LANES).astype(BF16)
        s = _dot_nt(qall_ref[...], kn) + cq - ck
        rows = lax.broadcasted_iota(jnp.int32, s.shape, 0)
        cols = lax.broadcasted_iota(jnp.int32, s.shape, 1)
        ok = (head_of_col(cols) == jnp.right_shift(rows, tbits))
        ok = ok & (jnp.right_shift(cols, hbits) <= jnp.bitwise_and(rows, nt - 1)) & (cols < nrow)
        s = jnp.where(ok, s, -jnp.inf)
        m0 = jnp.max(s, axis=-1, keepdims=True)
        p = jnp.exp(s - m0)
        m_ref[...] = m0
        l_ref[...] = jnp.sum(p, axis=-1, keepdims=True)
        acc_ref[...] = _dot(p.astype(BF16), vn)

    qall = qall_ref[...]
    cqm = cqm_ref[...]
    ss = []
    for r in range(group):
        kb = k_refs[r][...].astype(BF16)
        ss.append(_dot_nt(qall, kb) + (cqm - cp_ref[r:r + 1, :]))
    mx = ss[0]
    for s in ss[1:]:
        mx = jnp.maximum(mx, s)
    m_prev = m_ref[...]
    m_new = jnp.maximum(m_prev, jnp.max(mx, axis=-1, keepdims=True))
    alpha = jnp.exp(m_prev - m_new)
    acc = alpha * acc_ref[...]
    psum = jnp.zeros((nrow, prow), F32)
    for r in range(group):
        p = jnp.exp(ss[r] - m_new)
        psum = psum + p
        acc = acc + _dot(p.astype(BF16), v_refs[r][...].astype(BF16))
    l_ref[...] = alpha * l_ref[...] + jnp.sum(psum, axis=-1, keepdims=True)
    acc_ref[...] = acc
    m_ref[...] = m_new

    @pl.when(g == ng - 1)
    def _():
        o = acc_ref[...] / l_ref[...]
        for h in range(heads):
            o_ref[:, h * hd:(h + 1) * hd] = o[h * nt:(h + 1) * nt, :]


def _fox_sample(base, zf, ff, b_fox_pad, ctot, c_past, cache_k, cache_v, page_table, layer, depth,
                row0, nt, heads, hd, group, qcol, prev):
    nb, n_pages = page_table.shape
    prow = cache_k.shape[2]
    width = heads * hd
    ng = n_pages // group
    rb0 = row0 // nt
    nrow = heads * nt
    aliased = prev is not None

    def page_map(r):
        return lambda b, g, pt: (layer, pt[b, g * group + r], 0, 0)

    kern = functools.partial(_fox_sample_kernel, group=group, heads=heads, hd=hd, nt=nt, aliased=aliased)
    new_rows = pl.BlockSpec((None, nrow, hd), lambda b, g, pt: (layer, b, 0))
    new_shape = jax.ShapeDtypeStruct((depth, nb * nrow, hd), F32)
    in_specs = [_any_spec()] + ([_any_spec(), _any_spec()] if aliased else []) + [
        pl.BlockSpec((nt, width), lambda b, g, pt: (rb0 + b, qcol)),
        pl.BlockSpec((nt, width), lambda b, g, pt: (rb0 + b, qcol + 1)),
        pl.BlockSpec((nt, width), lambda b, g, pt: (rb0 + b, qcol + 2)),
        pl.BlockSpec((nt, LANES), lambda b, g, pt: (rb0 + b, 0)),
        pl.BlockSpec((None, 1, LANES), lambda b, g, pt: (layer, 0, 0)),
        pl.BlockSpec((None, 1, LANES), lambda b, g, pt: (layer * nb + b, 0, 0)),
        pl.BlockSpec((None, group, c_past.shape[2]), lambda b, g, pt: (layer * nb + b, g, 0)),
    ]
    in_specs += [pl.BlockSpec((None, None, prow, hd), page_map(r)) for r in range(group)]
    in_specs += [pl.BlockSpec((None, None, prow, hd), page_map(r)) for r in range(group)]
    return pl.pallas_call(
        kern,
        grid_spec=pltpu.PrefetchScalarGridSpec(
            num_scalar_prefetch=1,
            grid=(nb, ng),
            in_specs=in_specs,
            out_specs=[pl.BlockSpec((nt, width), lambda b, g, pt: (rb0 + b, 0)),
                       pl.BlockSpec((None, nt, LANES), lambda b, g, pt: (b, 0, 0)),
                       new_rows, new_rows],
            scratch_shapes=[
                pltpu.VMEM((nrow, hd), BF16),
                pltpu.VMEM((nrow, prow), F32),
                pltpu.VMEM((nrow, 1), F32),
                pltpu.VMEM((nrow, 1), F32),
                pltpu.VMEM((nrow, hd), F32),
            ],
        ),
        out_shape=[jax.ShapeDtypeStruct(base.shape, F32),
                   jax.ShapeDtypeStruct((nb, nt, LANES), F32),
                   new_shape, new_shape],
        input_output_aliases={1: 0, 2: 2, 3: 3} if aliased else {1: 0},
        compiler_params=_params("arbitrary", "arbitrary"),
        name="fox_sample",
    )(page_table, base, *(prev if aliased else ()), zf, zf, zf, ff, b_fox_pad, ctot, c_past,
      *([cache_k] * group), *([cache_v] * group))


def _hgrn_operators(c, real):
    t = np.arange(c)[:, None]
    j = np.arange(c)[None, :]
    mats = [(j <= t), (j > t)]
    masks = []
    m = 1
    while m < real:
        blk = t // (2 * m)
        mid = blk * (2 * m) + m - 1
        second = t > mid
        mats.append(np.where(second, (j > mid) & (j <= t), (j > t) & (j <= mid)))
        s = np.arange(c)[None, :]
        masks.append((blk == (s // (2 * m))) & second & (s <= (s // (2 * m)) * (2 * m) + m - 1))
        m *= 2
    a = np.concatenate(mats, axis=0).astype(np.float32)
    a3 = np.concatenate([a, a, a], axis=1)
    return jnp.asarray(a3, BF16), jnp.asarray(np.stack(masks).astype(np.float32)), len(masks)


def _hgrn_kernel(*refs, layer, heads, dk, n_lev, real, aliased):
    if aliased:
        refs = refs[1:]
    (q_ref, f_ref, i_ref, g_ref, lbraw_ref, gn_ref, a3_ref, msk_ref, s0_ref,
     o_ref, sfin_ref, st_ref) = refs
    c = HGRN_CHUNK
    ci = pl.program_id(1)
    nc = pl.num_programs(1)

    @pl.when(ci == 0)
    def _():
        for h in range(heads):
            st_ref[h] = s0_ref[h].T

    raw = lbraw_ref[...]
    e = jnp.exp(raw - jnp.max(raw, axis=0, keepdims=True))
    sm = e / jnp.sum(e, axis=0, keepdims=True)
    cs = sm[0:1]
    for r in range(1, layer + 1):
        cs = cs + sm[r:r + 1]
    lb = cs - sm[0:1]
    log_lb = jnp.log(lb)
    log_1m = jnp.log1p(-lb)

    zf = _pad_rows(f_ref[...], c)
    q = _pad_rows(q_ref[...], c)
    v = _pad_rows(i_ref[...], c)
    b = log_1m + _log_sigmoid(zf)
    mx = jnp.maximum(log_lb, b)
    g = mx + jnp.log1p(jnp.exp(-jnp.abs(log_lb - b)))
    kr = (1.0 - lb) * _sigmoid(-zf)
    if real < c:
        valid = lax.broadcasted_iota(jnp.int32, g.shape, 0) < real
        g = jnp.where(valid, g, 0.0)
        kr = jnp.where(valid, kr, 0.0)

    x = jnp.exp(_dot(a3_ref[...], jnp.concatenate(_split3(g), axis=0)))
    xq = x[0:c]
    xk = x[c:2 * c]
    xlast = xq[c - 1:c]
    qe = (q * xq).astype(BF16)
    ke = (kr * xk).astype(BF16)
    vb = v.astype(BF16)
    for h in range(heads):
        hs = slice(h * dk, (h + 1) * dk)
        st = st_ref[h]
        o = _dot_nt(qe[:, hs], st.astype(BF16))
        att = jnp.zeros((c, c), F32)
        for l in range(n_lev):
            xl = x[(2 + l) * c:(3 + l) * c, hs]
            p = _dot_nt((q[:, hs] * xl).astype(BF16), (kr[:, hs] * xl).astype(BF16))
            att = att + p * msk_ref[l]
        o = o + _dot(att.astype(BF16), vb[:, hs])
        o = o + jnp.sum(q[:, hs] * kr[:, hs], axis=-1, keepdims=True) * v[:, hs]
        st_ref[h] = st * xlast[:, hs] + _dot_tn(vb[:, hs], ke[:, hs])
        on = _rms_rows(o, gn_ref[:, hs])
        gg = g_ref[:, hs]
        o_ref[:, hs] = on[:real] * (gg * _sigmoid(gg))

    @pl.when(ci == nc - 1)
    def _():
        for h in range(heads):
            sfin_ref[h] = st_ref[h].T


def _hgrn(zf, hgrn_lb, hgrn_norm, s0, layer, row0, n_seq, seq, heads, dk, col0, base=None):
    c = HGRN_CHUNK
    real = min(seq, c)
    assert seq % real == 0
    nc = seq // real
    width = heads * dk
    rb0 = row0 // real
    a3, masks, n_lev = _hgrn_operators(c, real)
    depth = hgrn_lb.shape[0]
    blk = lambda col: pl.BlockSpec((real, width), lambda n, i: (rb0 + n * nc + i, col))
    aliased = base is not None
    kern = functools.partial(_hgrn_kernel, layer=layer, heads=heads, dk=dk, n_lev=n_lev, real=real,
                             aliased=aliased)
    in_specs = [
        blk(col0), blk(col0 + 1), blk(col0 + 2), blk(col0 + 3),
        pl.BlockSpec((depth, width), lambda n, i: (0, 0)),
        pl.BlockSpec((None, 1, width), lambda n, i: (layer, 0, 0)),
        pl.BlockSpec(a3.shape, lambda n, i: (0, 0)),
        pl.BlockSpec(masks.shape, lambda n, i: (0, 0, 0)),
        pl.BlockSpec((None, heads, dk, dk), lambda n, i: (n, 0, 0, 0)),
    ]
    args = [zf, zf, zf, zf, hgrn_lb, hgrn_norm, a3, masks, s0]
    if aliased:
        in_specs = [_any_spec()] + in_specs
        args = [base] + args
    return pl.pallas_call(
        kern,
        grid=(n_seq, nc),
        in_specs=in_specs,
        out_specs=[
            blk(0),
            pl.BlockSpec((None, heads, dk, dk), lambda n, i: (n, 0, 0, 0)),
        ],
        out_shape=[
            jax.ShapeDtypeStruct((zf.shape[0], width), F32),
            jax.ShapeDtypeStruct((n_seq, heads, dk, dk), F32),
        ],
        scratch_shapes=[pltpu.VMEM((heads, dk, dk), F32)],
        input_output_aliases={0: 0} if aliased else {},
        compiler_params=_params("arbitrary", "arbitrary"),
        name="hgrn",
    )(*args)


def _merge_kernel(ma_ref, ob_ref, oc_ref, wa_ref, wb_ref, wc_ref, ga_ref, gb_ref, gc_ref, mix_ref):
    ya = _dot(ma_ref[...].astype(BF16), wa_ref[...])
    yb = _dot(ob_ref[...].astype(BF16), wb_ref[...])
    yc = _dot(oc_ref[...].astype(BF16), wc_ref[...])
    gate = lambda r: _sigmoid(r[...].astype(F32))
    mix = gate(ga_ref) * ya + gate(gb_ref) * yb + gate(gc_ref) * yc
    mix_ref[...] = mix.astype(mix_ref.dtype)


def _merge(m_a, o_b, o_c, w_a, w_b, w_c, zg, layer, tm, tn):
    m, k = m_a.shape
    d = w_a.shape[2]
    nj = d // tn
    act = pl.BlockSpec((tm, k), lambda i, j: (i, 0))
    wsp = pl.BlockSpec((None, k, tn), lambda i, j: (layer, 0, j))
    gate = lambda r: pl.BlockSpec((tm, tn), lambda i, j: (i, r * nj + j))
    return pl.pallas_call(
        _merge_kernel,
        grid=(m // tm, nj),
        in_specs=[act, act, act, wsp, wsp, wsp, gate(0), gate(1), gate(2)],
        out_specs=pl.BlockSpec((tm, tn), lambda i, j: (i, j)),
        out_shape=jax.ShapeDtypeStruct((m, d), BF16),
        compiler_params=_params("arbitrary", "arbitrary"),
        name="merge",
    )(m_a, o_b, o_c, w_a, w_b, w_c, zg, zg, zg)


def _residual_matmul_kernel(a_ref, w_ref, x_ref, o_ref):
    o_ref[...] = x_ref[...] + _dot(a_ref[...], w_ref[...])


def _residual_matmul(a, w, x, layer, tm, tn):
    m, k = a.shape
    d = w.shape[2]
    return pl.pallas_call(
        _residual_matmul_kernel,
        grid=(m // tm, d // tn),
        in_specs=[
            pl.BlockSpec((tm, k), lambda i, j: (i, 0)),
            pl.BlockSpec((None, k, tn), lambda i, j: (layer, 0, j)),
            pl.BlockSpec((tm, tn), lambda i, j: (i, j)),
        ],
        out_specs=pl.BlockSpec((tm, tn), lambda i, j: (i, j)),
        out_shape=jax.ShapeDtypeStruct((m, d), F32),
        compiler_params=_params("arbitrary", "arbitrary"),
        name="residual_matmul",
    )(a, w, x)


def _swiglu_up_kernel(x_ref, g_ref, wg_ref, wu_ref, a_ref, h_ref):
    @pl.when(pl.program_id(1) == 0)
    def _():
        h_ref[...] = _rms_rows(x_ref[...], g_ref[...]).astype(BF16)

    h = h_ref[...]
    gate = _dot(h, wg_ref[...])
    up = _dot(h, wu_ref[...])
    a_ref[...] = (gate * _sigmoid(gate) * up).astype(a_ref.dtype)


def _swiglu_up(x, norm, w_gate_up, layer, tm, tn):
    m, d = x.shape
    f = w_gate_up.shape[2] // 2
    nj = f // tn
    return pl.pallas_call(
        _swiglu_up_kernel,
        grid=(m // tm, nj),
        in_specs=[
            pl.BlockSpec((tm, d), lambda i, j: (i, 0), pipeline_mode=pl.Buffered(1)),
            pl.BlockSpec((None, 1, d), lambda i, j: (layer, 0, 0)),
            pl.BlockSpec((None, d, tn), lambda i, j: (layer, 0, j)),
            pl.BlockSpec((None, d, tn), lambda i, j: (layer, 0, nj + j)),
        ],
        out_specs=pl.BlockSpec((tm, tn), lambda i, j: (i, j)),
        out_shape=jax.ShapeDtypeStruct((m, f), BF16),
        scratch_shapes=[pltpu.VMEM((tm, d), BF16)],
        compiler_params=_params("arbitrary", "arbitrary"),
        name="swiglu_up",
    )(x, norm, w_gate_up, w_gate_up)


def _ple_kernel(x_ref, g_ref, wg_ref, p_ref, wp_ref, o_ref, h_ref, *, tn):
    j = pl.program_id(1)

    @pl.when(j == 0)
    def _():
        h_ref[...] = _rms_rows(x_ref[...], g_ref[...]).astype(BF16)

    gate = _sigmoid(_dot(h_ref[...], wg_ref[...]))
    emb = _dot(p_ref[...].astype(BF16), wp_ref[...])
    xs = x_ref[:, pl.ds(pl.multiple_of(j * tn, tn), tn)]
    o_ref[...] = xs + gate * emb


def _ple(x, norm, w_gate, p, w_proj, layer, tm, tn):
    m, d = x.shape
    pd = p.shape[2]
    return pl.pallas_call(
        functools.partial(_ple_kernel, tn=tn),
        grid=(m // tm, d // tn),
        in_specs=[
            pl.BlockSpec((tm, d), lambda i, j: (i, 0)),
            pl.BlockSpec((None, 1, d), lambda i, j: (layer, 0, 0)),
            pl.BlockSpec((None, d, tn), lambda i, j: (layer, 0, j)),
            pl.BlockSpec((None, tm, pd), lambda i, j: (layer, i, 0)),
            pl.BlockSpec((None, pd, tn), lambda i, j: (layer, 0, j)),
        ],
        out_specs=pl.BlockSpec((tm, tn), lambda i, j: (i, j)),
        out_shape=jax.ShapeDtypeStruct((m, d), F32),
        scratch_shapes=[pltpu.VMEM((tm, d), BF16)],
        compiler_params=_params("arbitrary", "arbitrary"),
        name="ple",
    )(x, norm, w_gate, p, w_proj)


def _final_norm_kernel(x_ref, g_ref, o_ref):
    o_ref[...] = _rms_rows(x_ref[...], g_ref[...])


def _final_norm(x, g, tm):
    m, d = x.shape
    return pl.pallas_call(
        _final_norm_kernel,
        grid=(m // tm,),
        in_specs=[pl.BlockSpec((tm, d), lambda i: (i, 0)), pl.BlockSpec((1, d), lambda i: (0, 0))],
        out_specs=pl.BlockSpec((tm, d), lambda i: (i, 0)),
        out_shape=jax.ShapeDtypeStruct((m, d), F32),
        compiler_params=_params("arbitrary"),
        name="final_norm",
    )(x, g)


def kernel(x_prompt, x_sample, cache_k, cache_v, cache_logf, state_conv, state_hgrn, page_table,
           p_prompt, p_sample, norm1, w_in, b_fox, conv_w, hgrn_lb, hgrn_norm, w_a, w_b, w_c, w_o,
           norm2, w_gate_up, w_down, ple_norm, ple_gate, ple_proj, final_norm):
    bp, seq, d = x_prompt.shape
    db, dseq, _ = x_sample.shape
    depth = w_in.shape[0]
    heads, hd = cache_k.shape[3], cache_k.shape[4]
    page = cache_k.shape[2]
    n_pool = cache_k.shape[1]
    cw = w_a.shape[1]
    fw = w_b.shape[1]
    rw = w_c.shape[1]
    rheads = state_hgrn.shape[2]
    dk = state_hgrn.shape[3]
    assert fw == heads * hd and rw == rheads * dk and state_hgrn.shape[4] == dk and cw == fw == rw
    mp = bp * seq
    ms = db * dseq
    m = mp + ms

    o_ff = 3 * cw + 3 * fw
    widths = (3 * cw, fw, 2 * fw, 4 * rw, 3 * d)
    w_main = jnp.concatenate([w_in[:, :, :o_ff], w_in[:, :, o_ff + heads:]], axis=2).astype(BF16)
    w_ff = jnp.pad(w_in[:, :, o_ff:o_ff + heads], ((0, 0), (0, 0), (0, LANES - heads))).astype(BF16)
    assert w_main.shape[2] == sum(widths)
    tn_in = fw
    qcol = 3 * cw // tn_in
    rcol = qcol + 3

    bf = lambda a: a.astype(BF16)
    w_a_b, w_b_b, w_c_b, w_o_b = bf(w_a), bf(w_b), bf(w_c), bf(w_o)
    w_gu_b, w_dn_b, w_pg_b, w_pp_b = bf(w_gate_up), bf(w_down), bf(ple_gate), bf(ple_proj)
    row3 = lambda a: a.reshape(a.shape[0], 1, a.shape[1])
    norm1_r, norm2_r, ple_norm_r, hnorm_r = row3(norm1), row3(norm2), row3(ple_norm), row3(hgrn_norm)
    b_fox_pad = row3(jnp.pad(b_fox, ((0, 0), (0, LANES - heads))))

    x = _stack_rows(x_prompt.reshape(1, mp, d), x_sample.reshape(1, ms, d)).reshape(m, d)
    p_all = _stack_rows(p_prompt.reshape(depth, mp, -1), p_sample.reshape(depth, ms, -1))
    ck = cache_k.reshape(depth, n_pool, page * heads, hd)
    cv = cache_v.reshape(depth, n_pool, page * heads, hd)

    n_pages = page_table.shape[1]
    group = _tile(n_pages, 8, 1)
    lf_flat = cache_logf.reshape(depth, n_pool, 1, page * heads)
    lf_pages = _gather_logf_pages(lf_flat, page_table, group)
    c_past, c_tot = _past_cumsum(lf_pages, heads)

    tm = _tile(m, 704, 16)
    tq = _tile(seq, 512, LANES)
    conv_tile = _tile(seq, 512, 8)
    zeros_conv = jnp.zeros((bp, 2, cw), F32)
    zeros_state = jnp.zeros((bp, rheads, dk, dk), F32)

    outs = {k: [] for k in ("lfp", "cp", "sp", "lfs", "cs", "ss")}
    kvp = None
    kvs = None
    for i in range(depth):
        zf, kvb, zg, ff = _in_proj(x, norm1_r, w_main, w_ff, i, tm, tn_in, widths, hd)

        m_a, conv_p = _conv(zf, conv_w, zeros_conv, i, 0, bp, seq, conv_tile, cw)
        m_a, conv_s = _conv(zf, conv_w, state_conv[i], i, mp, db, dseq, dseq, cw, base=m_a)

        lf_p, qaug, kaug, kp, vp = _logf_prompt(ff, b_fox_pad, zf, kvb, i, depth, bp, seq, heads, hd,
                                                qcol, kvp)
        kvp = (kp, vp)
        o_b = _fox_prompt(qaug, kaug, kvb, m, bp, seq, heads, hd, tq)
        o_b, lf_s, ks, vs = _fox_sample(o_b, zf, ff, b_fox_pad, c_tot, c_past, ck, cv, page_table, i,
                                        depth, mp, dseq, heads, hd, group, qcol, kvs)
        kvs = (ks, vs)

        o_c, st_p = _hgrn(zf, hgrn_lb, hnorm_r, zeros_state, i, 0, bp, seq, rheads, dk, rcol)
        o_c, st_s = _hgrn(zf, hgrn_lb, hnorm_r, state_hgrn[i], i, mp, db, dseq, rheads, dk, rcol,
                          base=o_c)

        mix = _merge(m_a, o_b, o_c, w_a_b, w_b_b, w_c_b, zg, i, tm, 512)
        x = _residual_matmul(mix, w_o_b, x, i, tm, 1024)

        act = _swiglu_up(x, norm2_r, w_gu_b, i, tm, _tile(w_gate_up.shape[2] // 2, 1408, LANES))
        x = _residual_matmul(act, w_dn_b, x, i, tm, 512)
        x = _ple(x, ple_norm_r, w_pg_b, p_all, w_pp_b, i, tm, 1024)

        outs["lfp"].append(lf_p[:, :, :heads])
        outs["lfs"].append(lf_s[:, :, :heads])
        outs["cp"].append(conv_p)
        outs["cs"].append(conv_s)
        outs["sp"].append(st_p)
        outs["ss"].append(st_s)

    y = _final_norm(x, final_norm.reshape(1, d), tm)
    st = lambda k: jnp.stack(outs[k])
    return (y[:mp].reshape(bp, seq, d), y[mp:].reshape(db, dseq, d),
            kp.reshape(depth, bp, seq, heads, hd), vp.reshape(depth, bp, seq, heads, hd),
            st("lfp"), st("cp"), st("sp"),
            ks.reshape(depth, db, dseq, heads, hd), vs.reshape(depth, db, dseq, heads, hd),
            st("lfs"), st("cs"), st("ss"))
```

```python
import functools

import numpy as np
import jax
import jax.numpy as jnp
from jax import lax
from jax.experimental import pallas as pl
from jax.experimental.pallas import tpu as pltpu

F32 = jnp.float32
BF16 = jnp.bfloat16
RMS_EPS = 1e-6
LANES = 128
SUBLANES = 8
HGRN_CHUNK = 128
VMEM_LIMIT = 56 * 1024 * 1024


def _tile(n, target, align):
    best = None
    for t in range(align, min(n, target) + 1, align):
        if n % t == 0:
            best = t
    assert best is not None, (n, target, align)
    return best


def _params(*sem):
    return pltpu.CompilerParams(dimension_semantics=sem, vmem_limit_bytes=VMEM_LIMIT)


def _sigmoid(x):
    return 1.0 / (1.0 + jnp.exp(-x))


def _log_sigmoid(x):
    return jnp.minimum(x, 0.0) - jnp.log(1.0 + jnp.exp(-jnp.abs(x)))


def _rms_rows(x, g):
    ms = jnp.mean(x * x, axis=-1, keepdims=True)
    return x * lax.rsqrt(ms + RMS_EPS) * g


def _split3(x):
    hi = x.astype(BF16)
    r1 = x - hi.astype(F32)
    mid = r1.astype(BF16)
    lo = (r1 - mid.astype(F32)).astype(BF16)
    return hi, mid, lo


def _dot(a, b):
    return jnp.dot(a, b, preferred_element_type=F32)


def _dot_nt(a, b):
    return lax.dot_general(a, b, (((1,), (1,)), ((), ())), preferred_element_type=F32)


def _dot_tn(a, b):
    return lax.dot_general(a, b, (((0,), (0,)), ((), ())), preferred_element_type=F32)


def _pad_rows(a, n):
    if a.shape[0] == n:
        return a
    return jnp.concatenate([a, jnp.zeros((n - a.shape[0], a.shape[1]), a.dtype)], axis=0)


def _any_spec():
    return pl.BlockSpec(memory_space=pl.ANY)


def _copy_kernel(*refs, aliased):
    a_ref, o_ref = refs[1:] if aliased else refs
    o_ref[...] = a_ref[...]


def _stack_rows(a, b):
    nl, ra, c = a.shape
    rb = b.shape[1]
    assert ra % rb == 0
    ta = _tile(ra, 1024, 8)
    shape = jax.ShapeDtypeStruct((nl, ra + rb, c), a.dtype)
    out = pl.pallas_call(
        functools.partial(_copy_kernel, aliased=False),
        grid=(nl, ra // ta),
        in_specs=[pl.BlockSpec((None, ta, c), lambda l, i: (l, i, 0))],
        out_specs=pl.BlockSpec((None, ta, c), lambda l, i: (l, i, 0)),
        out_shape=shape,
        compiler_params=_params("arbitrary", "arbitrary"),
        name="stack_rows_a",
    )(a)
    return pl.pallas_call(
        functools.partial(_copy_kernel, aliased=True),
        grid=(nl,),
        in_specs=[_any_spec(), pl.BlockSpec((None, rb, c), lambda l: (l, 0, 0))],
        out_specs=pl.BlockSpec((None, rb, c), lambda l: (l, ra // rb, 0)),
        out_shape=shape,
        input_output_aliases={0: 0},
        compiler_params=_params("arbitrary"),
        name="stack_rows_b",
    )(out, b)


def _in_proj_kernel(x_ref, g_ref, w_ref, wff_ref, zf_ref, kvb_ref, zg_ref, ff_ref, h_ref,
                    *, bounds, scale):
    j = pl.program_id(1)
    jq, jk, jr, jg = bounds

    @pl.when(j == 0)
    def _():
        h = _rms_rows(x_ref[...], g_ref[...]).astype(BF16)
        h_ref[...] = h
        ff_ref[...] = _dot(h, wff_ref[...])

    def mm():
        return _dot(h_ref[...], w_ref[...])

    @pl.when((j < jq) | ((j >= jr) & (j < jg)))
    def _():
        zf_ref[...] = mm()

    @pl.when((j >= jq) & (j < jk))
    def _():
        zf_ref[...] = mm() * scale

    @pl.when((j >= jk) & (j < jr))
    def _():
        z = mm()
        zf_ref[...] = z
        kvb_ref[...] = z.astype(BF16)

    @pl.when(j >= jg)
    def _():
        zg_ref[...] = mm().astype(BF16)


def _in_proj(x, norm, w_main, w_ff, layer, tm, tn, widths, hd):
    m, d = x.shape
    n = w_main.shape[2]
    assert all(w % tn == 0 for w in widths) and sum(widths) == n
    cuts = np.cumsum([w // tn for w in widths])
    jq, jk, jr, jg, nj = (int(c) for c in cuts)
    kern = functools.partial(_in_proj_kernel, bounds=(jq, jk, jr, jg), scale=hd ** -0.5)
    return pl.pallas_call(
        kern,
        grid=(m // tm, nj),
        in_specs=[
            pl.BlockSpec((tm, d), lambda i, j: (i, 0)),
            pl.BlockSpec((None, 1, d), lambda i, j: (layer, 0, 0)),
            pl.BlockSpec((None, d, tn), lambda i, j: (layer, 0, j)),
            pl.BlockSpec((None, d, LANES), lambda i, j: (layer, 0, 0)),
        ],
        out_specs=[
            pl.BlockSpec((tm, tn), lambda i, j: (i, jnp.minimum(j, jg - 1))),
            pl.BlockSpec((tm, tn), lambda i, j: (i, jnp.clip(j - jk, 0, jr - jk - 1))),
            pl.BlockSpec((tm, tn), lambda i, j: (i, jnp.clip(j - jg, 0, nj - jg - 1))),
            pl.BlockSpec((tm, LANES), lambda i, j: (i, 0)),
        ],
        out_shape=[
            jax.ShapeDtypeStruct((m, jg * tn), F32),
            jax.ShapeDtypeStruct((m, widths[2]), BF16),
            jax.ShapeDtypeStruct((m, widths[4]), BF16),
            jax.ShapeDtypeStruct((m, LANES), F32),
        ],
        scratch_shapes=[pltpu.VMEM((tm, d), BF16)],
        compiler_params=_params("arbitrary", "arbitrary"),
        name="in_proj",
    )(x, norm, w_main, w_ff)


def _bias_operators(heads, width):
    hd = width // heads
    selq = np.zeros((3 * LANES, width), np.float32)
    selk = np.zeros((3 * LANES, width), np.float32)
    cq = np.zeros((1, width), np.float32)
    ck = np.zeros((1, width), np.float32)
    for h in range(heads):
        for r in range(3):
            selq[r * LANES + h, h * hd + r] = 1.0
            selk[r * LANES + h, h * hd + 3 + r] = -1.0
            cq[0, h * hd + 3 + r] = 1.0
            ck[0, h * hd + r] = 1.0
    return jnp.asarray(selq, BF16), jnp.asarray(selk, BF16), jnp.asarray(cq), jnp.asarray(ck)


def _logf_prompt_kernel(*refs, heads, hd, aliased):
    if aliased:
        refs = refs[2:]
    (ff_ref, b_ref, q_ref, k_ref, k32_ref, v32_ref, tri_ref, selq_ref, selk_ref, cq_ref, ck_ref,
     lf_ref, qa_ref, ka_ref, kp_ref, vp_ref, carry_ref) = refs
    blk = ff_ref.shape[0]
    for h in range(heads):
        kp_ref[pl.ds(h, blk, stride=heads), :] = k32_ref[:, h * hd:(h + 1) * hd]
        vp_ref[pl.ds(h, blk, stride=heads), :] = v32_ref[:, h * hd:(h + 1) * hd]

    @pl.when(pl.program_id(1) == 0)
    def _():
        carry_ref[...] = jnp.zeros(carry_ref.shape, F32)

    lf = _log_sigmoid(ff_ref[...] + b_ref[...])
    lf_ref[...] = lf
    cb = _dot(tri_ref[...], jnp.concatenate(_split3(lf), axis=0)) + carry_ref[...]
    carry_ref[...] = cb[blk - 1:blk, :]
    c3 = jnp.concatenate(_split3(cb), axis=1)
    qb = (_dot(c3, selq_ref[...]) + cq_ref[...]).astype(BF16)
    kb = (_dot(c3, selk_ref[...]) + ck_ref[...]).astype(BF16)
    q = q_ref[...].astype(BF16)
    k = k_ref[...]
    for h in range(heads):
        hs = slice(h * hd, (h + 1) * hd)
        qa_ref[:, 2 * h * hd:(2 * h + 1) * hd] = q[:, hs]
        qa_ref[:, (2 * h + 1) * hd:(2 * h + 2) * hd] = qb[:, hs]
        ka_ref[:, 2 * h * hd:(2 * h + 1) * hd] = k[:, hs]
        ka_ref[:, (2 * h + 1) * hd:(2 * h + 2) * hd] = kb[:, hs]


def _logf_prompt(ff, b_fox_pad, zf, kvb, layer, depth, batch, seq, heads, hd, qcol, prev):
    width = heads * hd
    blk = _tile(seq, 512, LANES)
    nb = seq // blk
    tri = np.tril(np.ones((blk, blk), np.float32))
    tri3 = jnp.asarray(np.concatenate([tri, tri, tri], axis=1), BF16)
    ops = _bias_operators(heads, width)
    full = lambda a: pl.BlockSpec(a.shape, lambda b, i: (0,) * a.ndim)
    rows = lambda c, w: pl.BlockSpec((blk, w), lambda b, i: (b * nb + i, c))
    cache = pl.BlockSpec((None, blk * heads, hd), lambda b, i: (layer, b * nb + i, 0))
    aliased = prev is not None
    in_specs = [
        rows(0, LANES),
        pl.BlockSpec((None, 1, LANES), lambda b, i: (layer, 0, 0)),
        rows(qcol, width),
        rows(0, width),
        rows(qcol + 1, width),
        rows(qcol + 2, width),
        full(tri3), full(ops[0]), full(ops[1]), full(ops[2]), full(ops[3]),
    ]
    args = [ff, b_fox_pad, zf, kvb, zf, zf, tri3, *ops]
    if aliased:
        in_specs = [_any_spec(), _any_spec()] + in_specs
        args = list(prev) + args
    cache_shape = jax.ShapeDtypeStruct((depth, batch * seq * heads, hd), F32)
    return pl.pallas_call(
        functools.partial(_logf_prompt_kernel, heads=heads, hd=hd, aliased=aliased),
        grid=(batch, nb),
        in_specs=in_specs,
        out_specs=[
            pl.BlockSpec((None, blk, LANES), lambda b, i: (b, i, 0)),
            rows(0, 2 * width),
            rows(0, 2 * width),
            cache,
            cache,
        ],
        out_shape=[
            jax.ShapeDtypeStruct((batch, seq, LANES), F32),
            jax.ShapeDtypeStruct((batch * seq, 2 * width), BF16),
            jax.ShapeDtypeStruct((batch * seq, 2 * width), BF16),
            cache_shape,
            cache_shape,
        ],
        scratch_shapes=[pltpu.VMEM((1, LANES), F32)],
        input_output_aliases={0: 3, 1: 4} if aliased else {},
        compiler_params=_params("arbitrary", "arbitrary"),
        name="logf_prompt",
    )(*args)


def _conv_kernel(*refs, aliased):
    if aliased:
        refs = refs[1:]
    ab_ref, ac_ref, av_ref, w_ref, prev_ref, m_ref, newc_ref, carry_ref = refs
    i = pl.program_id(1)
    t = ac_ref.shape[0]

    @pl.when(i == 0)
    def _():
        carry_ref[...] = prev_ref[...]

    u = ac_ref[...] * av_ref[...]
    row = lax.broadcasted_iota(jnp.int32, u.shape, 0)
    p1 = carry_ref[1:2, :]
    p2 = carry_ref[0:1, :]
    u1 = jnp.where(row == 0, p1, pltpu.roll(u, 1, 0))
    u2 = pltpu.roll(u, 2, 0)
    u2 = jnp.where(row == 0, p2, jnp.where(row == 1, p1, u2))
    w = w_ref[...]
    conv = w[0:1, :] * u2 + w[1:2, :] * u1 + w[2:3, :] * u
    m_ref[...] = ab_ref[...] * conv
    last2 = u[t - 2:t, :]
    carry_ref[...] = last2
    newc_ref[...] = last2


def _conv(zf, conv_w, prev, layer, row0, n_seq, seq, tile, width, base=None):
    nt = seq // tile
    rb0 = row0 // tile
    rows = lambda c: pl.BlockSpec((tile, width), lambda n, i: (rb0 + n * nt + i, c))
    aliased = base is not None
    in_specs = [rows(0), rows(1), rows(2),
                pl.BlockSpec((None, 3, width), lambda n, i: (layer, 0, 0)),
                pl.BlockSpec((None, 2, width), lambda n, i: (n, 0, 0))]
    args = [zf, zf, zf, conv_w, prev]
    if aliased:
        in_specs = [_any_spec()] + in_specs
        args = [base] + args
    return pl.pallas_call(
        functools.partial(_conv_kernel, aliased=aliased),
        grid=(n_seq, nt),
        in_specs=in_specs,
        out_specs=[
            rows(0),
            pl.BlockSpec((None, 2, width), lambda n, i: (n, 0, 0)),
        ],
        out_shape=[
            jax.ShapeDtypeStruct((zf.shape[0], width), F32),
            jax.ShapeDtypeStruct((n_seq, 2, width), F32),
        ],
        scratch_shapes=[pltpu.VMEM((2, width), F32)],
        input_output_aliases={0: 0} if aliased else {},
        compiler_params=_params("arbitrary", "arbitrary"),
        name="gated_conv",
    )(*args)


def _fox_prompt_kernel(qa_ref, ka_ref, v_ref, o_ref, m_ref, l_ref, acc_ref, *, heads, hd):
    qi = pl.program_id(1)
    ki = pl.program_id(2)
    tq = qa_ref.shape[0]
    tk = ka_ref.shape[0]

    @pl.when(ki == 0)
    def _():
        m_ref[...] = jnp.full(m_ref.shape, -jnp.inf, F32)
        l_ref[...] = jnp.zeros(l_ref.shape, F32)
        acc_ref[...] = jnp.zeros(acc_ref.shape, F32)

    def step(masked):
        if masked:
            krow = lax.broadcasted_iota(jnp.int32, (tk, tq), 0)
            qcol = lax.broadcasted_iota(jnp.int32, (tk, tq), 1)
            keep = krow <= qcol
        for h in range(heads):
            aug = slice(2 * h * hd, (2 * h + 2) * hd)
            st = _dot_nt(ka_ref[:, aug], qa_ref[:, aug])
            if masked:
                st = jnp.where(keep, st, -jnp.inf)
            m_prev = m_ref[h]
            m_new = jnp.maximum(m_prev, jnp.max(st, axis=0, keepdims=True))
            alpha = jnp.exp(m_prev - m_new)
            p = jnp.exp(st - m_new)
            l_ref[h] = alpha * l_ref[h] + jnp.sum(p, axis=0, keepdims=True)
            acc_ref[h] = alpha * acc_ref[h] + _dot_tn(v_ref[:, h * hd:(h + 1) * hd], p.astype(BF16))
            m_ref[h] = m_new

    @pl.when(ki < qi)
    def _():
        step(False)

    @pl.when(ki == qi)
    def _():
        step(True)
        for h in range(heads):
            o_ref[:, h * hd:(h + 1) * hd] = (acc_ref[h] / l_ref[h]).T


def _fox_prompt(qaug, kaug, kvb, m_rows, batch, seq, heads, hd, tq):
    nq = seq // tq
    width = heads * hd
    kern = functools.partial(_fox_prompt_kernel, heads=heads, hd=hd)
    qmap = lambda b, qi, ki: (b * nq + qi, 0)
    kmap = lambda b, qi, ki: (b * nq + jnp.minimum(ki, qi), 0)
    vmap = lambda b, qi, ki: (b * nq + jnp.minimum(ki, qi), 1)
    return pl.pallas_call(
        kern,
        grid=(batch, nq, nq),
        in_specs=[
            pl.BlockSpec((tq, 2 * width), qmap),
            pl.BlockSpec((tq, 2 * width), kmap),
            pl.BlockSpec((tq, width), vmap),
        ],
        out_specs=pl.BlockSpec((tq, width), qmap),
        out_shape=jax.ShapeDtypeStruct((m_rows, width), F32),
        scratch_shapes=[
            pltpu.VMEM((heads, 1, tq), F32),
            pltpu.VMEM((heads, 1, tq), F32),
            pltpu.VMEM((heads, hd, tq), F32),
        ],
        compiler_params=_params("arbitrary", "arbitrary", "arbitrary"),
        name="fox_prompt",
    )(qaug, kaug, kvb)


def _gather_rows_kernel(pt_ref, *refs):
    n_in = len(refs) - 1
    out_ref = refs[-1]
    for r in range(n_in):
        out_ref[:, r, :] = refs[r][:, 0, :]


def _gather_logf_pages(cache_logf_flat, page_table, group):
    depth, _, _, width = cache_logf_flat.shape
    nb, n_pages = page_table.shape
    ng = n_pages // group

    def in_map(r):
        return lambda b, g, pt: (0, pt[b, g * group + r], 0, 0)

    return pl.pallas_call(
        _gather_rows_kernel,
        grid_spec=pltpu.PrefetchScalarGridSpec(
            num_scalar_prefetch=1,
            grid=(nb, ng),
            in_specs=[pl.BlockSpec((depth, None, 1, width), in_map(r)) for r in range(group)],
            out_specs=pl.BlockSpec((depth, None, group, width), lambda b, g, pt: (0, b, g, 0)),
        ),
        out_shape=jax.ShapeDtypeStruct((depth, nb, n_pages, width), F32),
        compiler_params=_params("arbitrary", "arbitrary"),
        name="gather_logf",
    )(page_table, *([cache_logf_flat] * group))


def _past_cumsum_kernel(x_ref, t3_ref, s3_ref, l3_ref, e3_ref, c_ref, tot_ref):
    x3 = jnp.concatenate(_split3(x_ref[...]), axis=1)
    y = _dot(x3, t3_ref[...])
    tot = _dot(x3, s3_ref[...])
    offs = _dot(l3_ref[...], jnp.concatenate(_split3(tot), axis=0))
    c_ref[...] = y + _dot(jnp.concatenate(_split3(offs), axis=1), e3_ref[...])
    n = tot.shape[0]
    tot_ref[...] = offs[n - 1:n, :] + tot[n - 1:n, :]


def _past_cumsum(lf_pages, heads):
    depth, nb, n_pages, width = lf_pages.shape
    pos = np.arange(width) // heads
    hd = np.arange(width) % heads
    t = ((hd[:, None] == hd[None, :]) & (pos[:, None] <= pos[None, :])).astype(np.float32)
    ssum = (hd[:, None] == np.arange(LANES)[None, :]).astype(np.float32)
    lstrict = np.tril(np.ones((n_pages, n_pages), np.float32), -1)
    e = (np.arange(LANES)[:, None] == hd[None, :]).astype(np.float32)
    t3 = jnp.asarray(np.concatenate([t, t, t], axis=0), BF16)
    s3 = jnp.asarray(np.concatenate([ssum, ssum, ssum], axis=0), BF16)
    l3 = jnp.asarray(np.concatenate([lstrict, lstrict, lstrict], axis=1), BF16)
    e3 = jnp.asarray(np.concatenate([e, e, e], axis=0), BF16)
    x = lf_pages.reshape(depth * nb, n_pages, width)
    full = lambda a: pl.BlockSpec(a.shape, lambda g: (0,) * a.ndim)
    return pl.pallas_call(
        _past_cumsum_kernel,
        grid=(depth * nb,),
        in_specs=[pl.BlockSpec((None, n_pages, width), lambda g: (g, 0, 0)),
                  full(t3), full(s3), full(l3), full(e3)],
        out_specs=[pl.BlockSpec((None, n_pages, width), lambda g: (g, 0, 0)),
                   pl.BlockSpec((None, 1, LANES), lambda g: (g, 0, 0))],
        out_shape=[jax.ShapeDtypeStruct((depth * nb, n_pages, width), F32),
                   jax.ShapeDtypeStruct((depth * nb, 1, LANES), F32)],
        compiler_params=_params("arbitrary"),
        name="past_cumsum",
    )(x, t3, s3, l3, e3)


def _fox_sample_kernel(pt_ref, base_ref, *refs, group, heads, hd, nt, aliased):
    if aliased:
        refs = refs[2:]
    q_ref, kn_ref, vn_ref, ff_ref, b_ref, ctot_ref, cp_ref = refs[:7]
    refs = refs[7:]
    k_refs = refs[:group]
    v_refs = refs[group:2 * group]
    o_ref, lf_ref, ks_ref, vs_ref = refs[2 * group:2 * group + 4]
    qall_ref, cqm_ref, m_ref, l_ref, acc_ref = refs[2 * group + 4:]
    g = pl.program_id(1)
    ng = pl.num_programs(1)
    nrow = heads * nt
    prow = k_refs[0].shape[0]
    assert heads & (heads - 1) == 0 and nt & (nt - 1) == 0
    hbits = heads.bit_length() - 1
    tbits = nt.bit_length() - 1

    def head_of_col(cols):
        return jnp.bitwise_and(cols, heads - 1)

    @pl.when(g == 0)
    def _():
        lf = _log_sigmoid(ff_ref[...] + b_ref[...])
        lf_ref[...] = lf
        trow = lax.broadcasted_iota(jnp.int32, lf.shape, 0)
        tlane = lax.broadcasted_iota(jnp.int32, lf.shape, 1)
        cs = lf
        sh = 1
        while sh < nt:
            cs = cs + jnp.where(trow >= sh, pltpu.roll(cs, sh, 0), 0.0)
            sh *= 2
        cn = ctot_ref[...] + cs
        cq = jnp.concatenate([cn[:, h:h + 1] for h in range(heads)], axis=0)
        prows = lax.broadcasted_iota(jnp.int32, (nrow, prow), 0)
        pcols = lax.broadcasted_iota(jnp.int32, (nrow, prow), 1)
        cqm_ref[...] = jnp.where(head_of_col(pcols) == jnp.right_shift(prows, tbits), cq, -jnp.inf)
        qf = q_ref[...]
        qall = jnp.concatenate([qf[:, h * hd:(h + 1) * hd] for h in range(heads)], axis=0)
        qall_ref[...] = qall.astype(BF16)
        cnm = jnp.where(tlane < heads, cn, 0.0)
        ck = jnp.zeros((1, LANES), F32)
        for t in range(nt):
            rolled = pltpu.roll(cnm, t * heads, 1)
            ck = ck + jnp.sum(jnp.where(trow == t, rolled, 0.0), axis=0, keepdims=True)
        for h in range(heads):
            ks_ref[pl.ds(h, nt, stride=heads), :] = kn_ref[:, h * hd:(h + 1) * hd]
            vs_ref[pl.ds(h, nt, stride=heads), :] = vn_ref[:, h * hd:(h + 1) * hd]
        kn = _pad_rows(ks_ref[...], LANES).astype(BF16)
        vn = _pad_rows(vs_ref[...], LANES).astype(BF16)
        s = _dot_nt(qall_ref[...], kn) + cq - ck
        rows = lax.broadcasted_iota(jnp.int32, s.shape, 0)
        cols = lax.broadcasted_iota(jnp.int32, s.shape, 1)
        ok = (head_of_col(cols) == jnp.right_shift(rows, tbits))
        ok = ok & (jnp.right_shift(cols, hbits) <= jnp.bitwise_and(rows, nt - 1)) & (cols < nrow)
        s = jnp.where(ok, s, -jnp.inf)
        m0 = jnp.max(s, axis=-1, keepdims=True)
        p = jnp.exp(s - m0)
        m_ref[...] = m0
        l_ref[...] = jnp.sum(p, axis=-1, keepdims=True)
        acc_ref[...] = _dot(p.astype(BF16), vn)

    qall = qall_ref[...]
    cqm = cqm_ref[...]
    ss = []
    for r in range(group):
        kb = k_refs[r][...].astype(BF16)
        ss.append(_dot_nt(qall, kb) + (cqm - cp_ref[r:r + 1, :]))
    mx = ss[0]
    for s in ss[1:]:
        mx = jnp.maximum(mx, s)
    m_prev = m_ref[...]
    m_new = jnp.maximum(m_prev, jnp.max(mx, axis=-1, keepdims=True))
    alpha = jnp.exp(m_prev - m_new)
    acc = alpha * acc_ref[...]
    psum = jnp.zeros((nrow, prow), F32)
    for r in range(group):
        p = jnp.exp(ss[r] - m_new)
        psum = psum + p
        acc = acc + _dot(p.astype(BF16), v_refs[r][...].astype(BF16))
    l_ref[...] = alpha * l_ref[...] + jnp.sum(psum, axis=-1, keepdims=True)
    acc_ref[...] = acc
    m_ref[...] = m_new

    @pl.when(g == ng - 1)
    def _():
        o = acc_ref[...] / l_ref[...]
        for h in range(heads):
            o_ref[:, h * hd:(h + 1) * hd] = o[h * nt:(h + 1) * nt, :]


def _fox_sample(base, zf, ff, b_fox_pad, ctot, c_past, cache_k, cache_v, page_table, layer, depth,
                row0, nt, heads, hd, group, qcol, prev):
    nb, n_pages = page_table.shape
    prow = cache_k.shape[2]
    width = heads * hd
    ng = n_pages // group
    rb0 = row0 // nt
    nrow = heads * nt
    aliased = prev is not None

    def page_map(r):
        return lambda b, g, pt: (layer, pt[b, g * group + r], 0, 0)

    kern = functools.partial(_fox_sample_kernel, group=group, heads=heads, hd=hd, nt=nt, aliased=aliased)
    new_rows = pl.BlockSpec((None, nrow, hd), lambda b, g, pt: (layer, b, 0))
    new_shape = jax.ShapeDtypeStruct((depth, nb * nrow, hd), F32)
    in_specs = [_any_spec()] + ([_any_spec(), _any_spec()] if aliased else []) + [
        pl.BlockSpec((nt, width), lambda b, g, pt: (rb0 + b, qcol)),
        pl.BlockSpec((nt, width), lambda b, g, pt: (rb0 + b, qcol + 1)),
        pl.BlockSpec((nt, width), lambda b, g, pt: (rb0 + b, qcol + 2)),
        pl.BlockSpec((nt, LANES), lambda b, g, pt: (rb0 + b, 0)),
        pl.BlockSpec((None, 1, LANES), lambda b, g, pt: (layer, 0, 0)),
        pl.BlockSpec((None, 1, LANES), lambda b, g, pt: (layer * nb + b, 0, 0)),
        pl.BlockSpec((None, group, c_past.shape[2]), lambda b, g, pt: (layer * nb + b, g, 0)),
    ]
    in_specs += [pl.BlockSpec((None, None, prow, hd), page_map(r)) for r in range(group)]
    in_specs += [pl.BlockSpec((None, None, prow, hd), page_map(r)) for r in range(group)]
    return pl.pallas_call(
        kern,
        grid_spec=pltpu.PrefetchScalarGridSpec(
            num_scalar_prefetch=1,
            grid=(nb, ng),
            in_specs=in_specs,
            out_specs=[pl.BlockSpec((nt, width), lambda b, g, pt: (rb0 + b, 0)),
                       pl.BlockSpec((None, nt, LANES), lambda b, g, pt: (b, 0, 0)),
                       new_rows, new_rows],
            scratch_shapes=[
                pltpu.VMEM((nrow, hd), BF16),
                pltpu.VMEM((nrow, prow), F32),
                pltpu.VMEM((nrow, 1), F32),
                pltpu.VMEM((nrow, 1), F32),
                pltpu.VMEM((nrow, hd), F32),
            ],
        ),
        out_shape=[jax.ShapeDtypeStruct(base.shape, F32),
                   jax.ShapeDtypeStruct((nb, nt, LANES), F32),
                   new_shape, new_shape],
        input_output_aliases={1: 0, 2: 2, 3: 3} if aliased else {1: 0},
        compiler_params=_params("arbitrary", "arbitrary"),
        name="fox_sample",
    )(page_table, base, *(prev if aliased else ()), zf, zf, zf, ff, b_fox_pad, ctot, c_past,
      *([cache_k] * group), *([cache_v] * group))


def _hgrn_operators(c, real):
    t = np.arange(c)[:, None]
    j = np.arange(c)[None, :]
    mats = [(j <= t), (j > t)]
    masks = []
    m = 1
    while m < real:
        blk = t // (2 * m)
        mid = blk * (2 * m) + m - 1
        second = t > mid
        if m < SUBLANES:
            mats.append(np.where(second, (j > mid) & (j <= t), (j > t) & (j <= mid)))
        s = np.arange(c)[None, :]
        masks.append((blk == (s // (2 * m))) & second & (s <= (s // (2 * m)) * (2 * m) + m - 1))
        m *= 2
    a = np.concatenate(mats, axis=0).astype(np.float32)
    a3 = np.concatenate([a, a, a], axis=1)
    return jnp.asarray(a3, BF16), jnp.asarray(np.stack(masks).astype(np.float32)), len(masks)


def _hgrn_kernel(*refs, layer, heads, dk, n_lev, real, aliased):
    if aliased:
        refs = refs[1:]
    (q_ref, f_ref, i_ref, g_ref, lbraw_ref, gn_ref, a3_ref, msk_ref, s0_ref,
     o_ref, sfin_ref, st_ref) = refs
    c = HGRN_CHUNK
    ci = pl.program_id(1)
    nc = pl.num_programs(1)

    @pl.when(ci == 0)
    def _():
        for h in range(heads):
            st_ref[h] = s0_ref[h].T

    raw = lbraw_ref[...]
    e = jnp.exp(raw - jnp.max(raw, axis=0, keepdims=True))
    sm = e / jnp.sum(e, axis=0, keepdims=True)
    cs = sm[0:1]
    for r in range(1, layer + 1):
        cs = cs + sm[r:r + 1]
    lb = cs - sm[0:1]
    log_lb = jnp.log(lb)
    log_1m = jnp.log1p(-lb)

    zf = _pad_rows(f_ref[...], c)
    q = _pad_rows(q_ref[...], c)
    v = _pad_rows(i_ref[...], c)
    b = log_1m + _log_sigmoid(zf)
    mx = jnp.maximum(log_lb, b)
    g = mx + jnp.log(1.0 + jnp.exp(-jnp.abs(log_lb - b)))
    kr = (1.0 - lb) * _sigmoid(-zf)
    if real < c:
        valid = lax.broadcasted_iota(jnp.int32, g.shape, 0) < real
        g = jnp.where(valid, g, 0.0)
        kr = jnp.where(valid, kr, 0.0)

    e = _dot(a3_ref[...], jnp.concatenate(_split3(g), axis=0))
    eq = e[0:c]
    xq = jnp.exp(eq)
    xk = jnp.exp(e[c:2 * c])
    xlast = xq[c - 1:c]
    n_mx = e.shape[0] // c - 2

    def level_factor(l):
        if l < n_mx:
            return jnp.exp(e[(2 + l) * c:(3 + l) * c])
        m = 1 << l
        mids = [jnp.broadcast_to(eq[b0 + m - 1:b0 + m, :], (2 * m, eq.shape[1]))
                for b0 in range(0, c, 2 * m)]
        r = mids[0] if len(mids) == 1 else jnp.concatenate(mids, axis=0)
        return jnp.exp(-jnp.abs(eq - r))

    xls = [level_factor(l) for l in range(n_lev)]
    qe = (q * xq).astype(BF16)
    ke = (kr * xk).astype(BF16)
    vb = v.astype(BF16)
    for h in range(heads):
        hs = slice(h * dk, (h + 1) * dk)
        st = st_ref[h]
        o = _dot_nt(qe[:, hs], st.astype(BF16))
        att = jnp.zeros((c, c), F32)
        for l in range(n_lev):
            xl = xls[l][:, hs]
            p = _dot_nt((q[:, hs] * xl).astype(BF16), (kr[:, hs] * xl).astype(BF16))
            att = att + p * msk_ref[l]
        o = o + _dot(att.astype(BF16), vb[:, hs])
        o = o + jnp.sum(q[:, hs] * kr[:, hs], axis=-1, keepdims=True) * v[:, hs]
        st_ref[h] = st * xlast[:, hs] + _dot_tn(vb[:, hs], ke[:, hs])
        on = _rms_rows(o, gn_ref[:, hs])
        gg = g_ref[:, hs]
        o_ref[:, hs] = on[:real] * (gg * _sigmoid(gg))

    @pl.when(ci == nc - 1)
    def _():
        for h in range(heads):
            sfin_ref[h] = st_ref[h].T


def _hgrn(zf, hgrn_lb, hgrn_norm, s0, layer, row0, n_seq, seq, heads, dk, col0, base=None):
    c = HGRN_CHUNK
    real = min(seq, c)
    assert seq % real == 0
    nc = seq // real
    width = heads * dk
    rb0 = row0 // real
    a3, masks, n_lev = _hgrn_operators(c, real)
    depth = hgrn_lb.shape[0]
    blk = lambda col: pl.BlockSpec((real, width), lambda n, i: (rb0 + n * nc + i, col))
    aliased = base is not None
    kern = functools.partial(_hgrn_kernel, layer=layer, heads=heads, dk=dk, n_lev=n_lev, real=real,
                             aliased=aliased)
    in_specs = [
        blk(col0), blk(col0 + 1), blk(col0 + 2), blk(col0 + 3),
        pl.BlockSpec((depth, width), lambda n, i: (0, 0)),
        pl.BlockSpec((None, 1, width), lambda n, i: (layer, 0, 0)),
        pl.BlockSpec(a3.shape, lambda n, i: (0, 0)),
        pl.BlockSpec(masks.shape, lambda n, i: (0, 0, 0)),
        pl.BlockSpec((None, heads, dk, dk), lambda n, i: (n, 0, 0, 0)),
    ]
    args = [zf, zf, zf, zf, hgrn_lb, hgrn_norm, a3, masks, s0]
    if aliased:
        in_specs = [_any_spec()] + in_specs
        args = [base] + args
    return pl.pallas_call(
        kern,
        grid=(n_seq, nc),
        in_specs=in_specs,
        out_specs=[
            blk(0),
            pl.BlockSpec((None, heads, dk, dk), lambda n, i: (n, 0, 0, 0)),
        ],
        out_shape=[
            jax.ShapeDtypeStruct((zf.shape[0], width), F32),
            jax.ShapeDtypeStruct((n_seq, heads, dk, dk), F32),
        ],
        scratch_shapes=[pltpu.VMEM((heads, dk, dk), F32)],
        input_output_aliases={0: 0} if aliased else {},
        compiler_params=_params("arbitrary", "arbitrary"),
        name="hgrn",
    )(*args)


def _merge_kernel(ma_ref, ob_ref, oc_ref, wa_ref, wb_ref, wc_ref, ga_ref, gb_ref, gc_ref, mix_ref,
                  a_ref):
    @pl.when(pl.program_id(1) == 0)
    def _():
        a_ref[0] = ma_ref[...].astype(BF16)
        a_ref[1] = ob_ref[...].astype(BF16)
        a_ref[2] = oc_ref[...].astype(BF16)

    ya = _dot(a_ref[0], wa_ref[...])
    yb = _dot(a_ref[1], wb_ref[...])
    yc = _dot(a_ref[2], wc_ref[...])
    gate = lambda r: _sigmoid(r[...].astype(F32))
    mix = gate(ga_ref) * ya + gate(gb_ref) * yb + gate(gc_ref) * yc
    mix_ref[...] = mix.astype(mix_ref.dtype)


def _merge(m_a, o_b, o_c, w_a, w_b, w_c, zg, layer, tm, tn):
    m, k = m_a.shape
    d = w_a.shape[2]
    nj = d // tn
    act = pl.BlockSpec((tm, k), lambda i, j: (i, 0))
    wsp = pl.BlockSpec((None, k, tn), lambda i, j: (layer, 0, j))
    gate = lambda r: pl.BlockSpec((tm, tn), lambda i, j: (i, r * nj + j))
    return pl.pallas_call(
        _merge_kernel,
        grid=(m // tm, nj),
        in_specs=[act, act, act, wsp, wsp, wsp, gate(0), gate(1), gate(2)],
        out_specs=pl.BlockSpec((tm, tn), lambda i, j: (i, j)),
        out_shape=jax.ShapeDtypeStruct((m, d), BF16),
        scratch_shapes=[pltpu.VMEM((3, tm, k), BF16)],
        compiler_params=_params("arbitrary", "arbitrary"),
        name="merge",
    )(m_a, o_b, o_c, w_a, w_b, w_c, zg, zg, zg)


def _residual_matmul_kernel(a_ref, w_ref, x_ref, o_ref):
    o_ref[...] = x_ref[...] + _dot(a_ref[...], w_ref[...])


def _weight_spec(shape, index_map, resident):
    if resident:
        return pl.BlockSpec(shape, index_map, pipeline_mode=pl.Buffered(1))
    return pl.BlockSpec(shape, index_map)


def _residual_matmul(a, w, x, layer, tm, tn):
    m, k = a.shape
    d = w.shape[2]
    return pl.pallas_call(
        _residual_matmul_kernel,
        grid=(m // tm, d // tn),
        in_specs=[
            pl.BlockSpec((tm, k), lambda i, j: (i, 0)),
            _weight_spec((None, k, tn), lambda i, j: (layer, 0, j), tn == d),
            pl.BlockSpec((tm, tn), lambda i, j: (i, j)),
        ],
        out_specs=pl.BlockSpec((tm, tn), lambda i, j: (i, j)),
        out_shape=jax.ShapeDtypeStruct((m, d), F32),
        compiler_params=_params("arbitrary", "arbitrary"),
        name="residual_matmul",
    )(a, w, x)


def _swiglu_up_kernel(x_ref, g_ref, wg_ref, wu_ref, a_ref, h_ref):
    @pl.when(pl.program_id(1) == 0)
    def _():
        h_ref[...] = _rms_rows(x_ref[...], g_ref[...]).astype(BF16)

    h = h_ref[...]
    gate = _dot(h, wg_ref[...])
    up = _dot(h, wu_ref[...])
    a_ref[...] = (gate * _sigmoid(gate) * up).astype(a_ref.dtype)


def _swiglu_up(x, norm, w_gate_up, layer, tm, tn):
    m, d = x.shape
    f = w_gate_up.shape[2] // 2
    nj = f // tn
    return pl.pallas_call(
        _swiglu_up_kernel,
        grid=(m // tm, nj),
        in_specs=[
            pl.BlockSpec((tm, d), lambda i, j: (i, 0)),
            pl.BlockSpec((None, 1, d), lambda i, j: (layer, 0, 0)),
            pl.BlockSpec((None, d, tn), lambda i, j: (layer, 0, j)),
            pl.BlockSpec((None, d, tn), lambda i, j: (layer, 0, nj + j)),
        ],
        out_specs=pl.BlockSpec((tm, tn), lambda i, j: (i, j)),
        out_shape=jax.ShapeDtypeStruct((m, f), BF16),
        scratch_shapes=[pltpu.VMEM((tm, d), BF16)],
        compiler_params=_params("arbitrary", "arbitrary"),
        name="swiglu_up",
    )(x, norm, w_gate_up, w_gate_up)


def _ple_kernel(x_ref, g_ref, wg_ref, p_ref, wp_ref, o_ref, h_ref, *, tn):
    j = pl.program_id(1)

    @pl.when(j == 0)
    def _():
        h_ref[...] = _rms_rows(x_ref[...], g_ref[...]).astype(BF16)

    gate = _sigmoid(_dot(h_ref[...], wg_ref[...]))
    emb = _dot(p_ref[...].astype(BF16), wp_ref[...])
    xs = x_ref[:, pl.ds(pl.multiple_of(j * tn, tn), tn)]
    o_ref[...] = xs + gate * emb


def _ple(x, norm, w_gate, p, w_proj, layer, tm, tn):
    m, d = x.shape
    pd = p.shape[2]
    return pl.pallas_call(
        functools.partial(_ple_kernel, tn=tn),
        grid=(m // tm, d // tn),
        in_specs=[
            pl.BlockSpec((tm, d), lambda i, j: (i, 0)),
            pl.BlockSpec((None, 1, d), lambda i, j: (layer, 0, 0)),
            _weight_spec((None, d, tn), lambda i, j: (layer, 0, j), tn == d),
            pl.BlockSpec((None, tm, pd), lambda i, j: (layer, i, 0)),
            _weight_spec((None, pd, tn), lambda i, j: (layer, 0, j), tn == d),
        ],
        out_specs=pl.BlockSpec((tm, tn), lambda i, j: (i, j)),
        out_shape=jax.ShapeDtypeStruct((m, d), F32),
        scratch_shapes=[pltpu.VMEM((tm, d), BF16)],
        compiler_params=_params("arbitrary", "arbitrary"),
        name="ple",
    )(x, norm, w_gate, p, w_proj)


def _final_norm_kernel(x_ref, g_ref, o_ref):
    o_ref[...] = _rms_rows(x_ref[...], g_ref[...])


def _final_norm(x, g, tm):
    m, d = x.shape
    return pl.pallas_call(
        _final_norm_kernel,
        grid=(m // tm,),
        in_specs=[pl.BlockSpec((tm, d), lambda i: (i, 0)), pl.BlockSpec((1, d), lambda i: (0, 0))],
        out_specs=pl.BlockSpec((tm, d), lambda i: (i, 0)),
        out_shape=jax.ShapeDtypeStruct((m, d), F32),
        compiler_params=_params("arbitrary"),
        name="final_norm",
    )(x, g)


def kernel(x_prompt, x_sample, cache_k, cache_v, cache_logf, state_conv, state_hgrn, page_table,
           p_prompt, p_sample, norm1, w_in, b_fox, conv_w, hgrn_lb, hgrn_norm, w_a, w_b, w_c, w_o,
           norm2, w_gate_up, w_down, ple_norm, ple_gate, ple_proj, final_norm):
    bp, seq, d = x_prompt.shape
    db, dseq, _ = x_sample.shape
    depth = w_in.shape[0]
    heads, hd = cache_k.shape[3], cache_k.shape[4]
    page = cache_k.shape[2]
    n_pool = cache_k.shape[1]
    cw = w_a.shape[1]
    fw = w_b.shape[1]
    rw = w_c.shape[1]
    rheads = state_hgrn.shape[2]
    dk = state_hgrn.shape[3]
    assert fw == heads * hd and rw == rheads * dk and state_hgrn.shape[4] == dk and cw == fw == rw
    mp = bp * seq
    ms = db * dseq
    m = mp + ms

    o_ff = 3 * cw + 3 * fw
    widths = (3 * cw, fw, 2 * fw, 4 * rw, 3 * d)
    n_main = w_in.shape[2] - heads
    col = lax.broadcasted_iota(jnp.int32, (1, 1, n_main), 2)
    w_main = jnp.where(col < o_ff, w_in[:, :, :n_main], w_in[:, :, heads:]).astype(BF16)
    w_ff = jnp.pad(w_in[:, :, o_ff:o_ff + heads], ((0, 0), (0, 0), (0, LANES - heads))).astype(BF16)
    assert w_main.shape[2] == sum(widths)
    tn_in = fw
    qcol = 3 * cw // tn_in
    rcol = qcol + 3

    bf = lambda a: a.astype(BF16)
    w_a_b, w_b_b, w_c_b, w_o_b = bf(w_a), bf(w_b), bf(w_c), bf(w_o)
    w_gu_b, w_dn_b, w_pg_b, w_pp_b = bf(w_gate_up), bf(w_down), bf(ple_gate), bf(ple_proj)
    row3 = lambda a: a.reshape(a.shape[0], 1, a.shape[1])
    norm1_r, norm2_r, ple_norm_r, hnorm_r = row3(norm1), row3(norm2), row3(ple_norm), row3(hgrn_norm)
    b_fox_pad = row3(jnp.pad(b_fox, ((0, 0), (0, LANES - heads))))

    x = _stack_rows(x_prompt.reshape(1, mp, d), x_sample.reshape(1, ms, d)).reshape(m, d)
    p_all = _stack_rows(p_prompt.reshape(depth, mp, -1), p_sample.reshape(depth, ms, -1))
    ck = cache_k.reshape(depth, n_pool, page * heads, hd)
    cv = cache_v.reshape(depth, n_pool, page * heads, hd)

    n_pages = page_table.shape[1]
    group = _tile(n_pages, 8, 1)
    lf_flat = cache_logf.reshape(depth, n_pool, 1, page * heads)
    lf_pages = _gather_logf_pages(lf_flat, page_table, group)
    c_past, c_tot = _past_cumsum(lf_pages, heads)

    tm = _tile(m, 704, 16)
    tq = _tile(seq, 512, LANES)
    conv_tile = _tile(seq, 512, 8)
    zeros_conv = jnp.zeros((bp, 2, cw), F32)
    zeros_state = jnp.zeros((bp, rheads, dk, dk), F32)

    outs = {k: [] for k in ("lfp", "cp", "sp", "lfs", "cs", "ss")}
    kvp = None
    kvs = None
    for i in range(depth):
        zf, kvb, zg, ff = _in_proj(x, norm1_r, w_main, w_ff, i, tm, tn_in, widths, hd)

        m_a, conv_p = _conv(zf, conv_w, zeros_conv, i, 0, bp, seq, conv_tile, cw)
        m_a, conv_s = _conv(zf, conv_w, state_conv[i], i, mp, db, dseq, dseq, cw, base=m_a)

        lf_p, qaug, kaug, kp, vp = _logf_prompt(ff, b_fox_pad, zf, kvb, i, depth, bp, seq, heads, hd,
                                                qcol, kvp)
        kvp = (kp, vp)
        o_b = _fox_prompt(qaug, kaug, kvb, m, bp, seq, heads, hd, tq)
        o_b, lf_s, ks, vs = _fox_sample(o_b, zf, ff, b_fox_pad, c_tot, c_past, ck, cv, page_table, i,
                                        depth, mp, dseq, heads, hd, group, qcol, kvs)
        kvs = (ks, vs)

        o_c, st_p = _hgrn(zf, hgrn_lb, hnorm_r, zeros_state, i, 0, bp, seq, rheads, dk, rcol)
        o_c, st_s = _hgrn(zf, hgrn_lb, hnorm_r, state_hgrn[i], i, mp, db, dseq, rheads, dk, rcol,
                          base=o_c)

        mix = _merge(m_a, o_b, o_c, w_a_b, w_b_b, w_c_b, zg, i, tm, 1024)
        x = _residual_matmul(mix, w_o_b, x, i, tm, d)

        act = _swiglu_up(x, norm2_r, w_gu_b, i, tm, _tile(w_gate_up.shape[2] // 2, 512, 2 * LANES))
        x = _residual_matmul(act, w_dn_b, x, i, tm, 512)
        x = _ple(x, ple_norm_r, w_pg_b, p_all, w_pp_b, i, tm, d)

        outs["lfp"].append(lf_p[:, :, :heads])
        outs["lfs"].append(lf_s[:, :, :heads])
        outs["cp"].append(conv_p)
        outs["cs"].append(conv_s)
        outs["sp"].append(st_p)
        outs["ss"].append(st_s)

    y = _final_norm(x, final_norm.reshape(1, d), tm)
    st = lambda k: jnp.stack(outs[k])
    return (y[:mp].reshape(bp, seq, d), y[mp:].reshape(db, dseq, d),
            kp.reshape(depth, bp, seq, heads, hd), vp.reshape(depth, bp, seq, heads, hd),
            st("lfp"), st("cp"), st("sp"),
            ks.reshape(depth, db, dseq, heads, hd), vs.reshape(depth, db, dseq, heads, hd),
            st("lfs"), st("cs"), st("ss"))
```

```python
import functools

import numpy as np
import jax
import jax.numpy as jnp
from jax import lax
from jax.experimental import pallas as pl
from jax.experimental.pallas import tpu as pltpu

F32 = jnp.float32
BF16 = jnp.bfloat16
RMS_EPS = 1e-6
LANES = 128
SUBLANES = 8
HGRN_CHUNK = 128
VMEM_LIMIT = 56 * 1024 * 1024


def _tile(n, target, align):
    best = None
    for t in range(align, min(n, target) + 1, align):
        if n % t == 0:
            best = t
    assert best is not None, (n, target, align)
    return best


def _params(*sem):
    return pltpu.CompilerParams(dimension_semantics=sem, vmem_limit_bytes=VMEM_LIMIT)


def _sigmoid(x):
    return 1.0 / (1.0 + jnp.exp(-x))


def _log_sigmoid(x):
    return jnp.minimum(x, 0.0) - jnp.log(1.0 + jnp.exp(-jnp.abs(x)))


def _rms_rows(x, g):
    ms = jnp.mean(x * x, axis=-1, keepdims=True)
    return x * lax.rsqrt(ms + RMS_EPS) * g


def _split3(x):
    hi = x.astype(BF16)
    r1 = x - hi.astype(F32)
    mid = r1.astype(BF16)
    lo = (r1 - mid.astype(F32)).astype(BF16)
    return hi, mid, lo


def _dot(a, b):
    return jnp.dot(a, b, preferred_element_type=F32)


def _dot_nt(a, b):
    return lax.dot_general(a, b, (((1,), (1,)), ((), ())), preferred_element_type=F32)


def _dot_tn(a, b):
    return lax.dot_general(a, b, (((0,), (0,)), ((), ())), preferred_element_type=F32)


def _pad_rows(a, n):
    if a.shape[0] == n:
        return a
    return jnp.concatenate([a, jnp.zeros((n - a.shape[0], a.shape[1]), a.dtype)], axis=0)


def _any_spec():
    return pl.BlockSpec(memory_space=pl.ANY)


def _copy_kernel(*refs, aliased):
    a_ref, o_ref = refs[1:] if aliased else refs
    o_ref[...] = a_ref[...]


def _stack_rows(a, b):
    nl, ra, c = a.shape
    rb = b.shape[1]
    assert ra % rb == 0
    ta = _tile(ra, 1024, 8)
    shape = jax.ShapeDtypeStruct((nl, ra + rb, c), a.dtype)
    out = pl.pallas_call(
        functools.partial(_copy_kernel, aliased=False),
        grid=(nl, ra // ta),
        in_specs=[pl.BlockSpec((None, ta, c), lambda l, i: (l, i, 0))],
        out_specs=pl.BlockSpec((None, ta, c), lambda l, i: (l, i, 0)),
        out_shape=shape,
        compiler_params=_params("arbitrary", "arbitrary"),
        name="stack_rows_a",
    )(a)
    return pl.pallas_call(
        functools.partial(_copy_kernel, aliased=True),
        grid=(nl,),
        in_specs=[_any_spec(), pl.BlockSpec((None, rb, c), lambda l: (l, 0, 0))],
        out_specs=pl.BlockSpec((None, rb, c), lambda l: (l, ra // rb, 0)),
        out_shape=shape,
        input_output_aliases={0: 0},
        compiler_params=_params("arbitrary"),
        name="stack_rows_b",
    )(out, b)


def _in_proj_kernel(x_ref, g_ref, wlo_ref, whi_ref, wff_ref, zf_ref, kvb_ref, zg_ref, ff_ref, h_ref,
                    *, bounds, scale):
    j = pl.program_id(1)
    jq, jk, jr, jg = bounds

    @pl.when(j == 0)
    def _():
        h = _rms_rows(x_ref[...], g_ref[...]).astype(BF16)
        h_ref[...] = h
        ff_ref[...] = _dot(h, wff_ref[...])

    @pl.when(j < jq)
    def _():
        zf_ref[...] = _dot(h_ref[...], wlo_ref[...])

    @pl.when((j >= jq) & (j < jk))
    def _():
        zf_ref[...] = _dot(h_ref[...], wlo_ref[...]) * scale

    @pl.when((j >= jk) & (j < jr))
    def _():
        z = _dot(h_ref[...], wlo_ref[...])
        zf_ref[...] = z
        kvb_ref[...] = z.astype(BF16)

    @pl.when((j >= jr) & (j < jg))
    def _():
        zf_ref[...] = _dot(h_ref[...], whi_ref[...])

    @pl.when(j >= jg)
    def _():
        zg_ref[...] = _dot(h_ref[...], whi_ref[...]).astype(BF16)


def _in_proj(x, norm, w_lo, w_hi, w_ff, layer, tm, tn, widths, hd):
    m, d = x.shape
    assert all(w % tn == 0 for w in widths)
    assert sum(widths[:3]) == w_lo.shape[2] and sum(widths[3:]) == w_hi.shape[2]
    cuts = np.cumsum([w // tn for w in widths])
    jq, jk, jr, jg, nj = (int(c) for c in cuts)
    kern = functools.partial(_in_proj_kernel, bounds=(jq, jk, jr, jg), scale=hd ** -0.5)
    return pl.pallas_call(
        kern,
        grid=(m // tm, nj),
        in_specs=[
            pl.BlockSpec((tm, d), lambda i, j: (i, 0)),
            pl.BlockSpec((None, 1, d), lambda i, j: (layer, 0, 0)),
            pl.BlockSpec((None, d, tn), lambda i, j: (layer, 0, jnp.minimum(j, jr - 1))),
            pl.BlockSpec((None, d, tn), lambda i, j: (layer, 0, jnp.maximum(j - jr, 0))),
            pl.BlockSpec((None, d, LANES), lambda i, j: (layer, 0, 0)),
        ],
        out_specs=[
            pl.BlockSpec((tm, tn), lambda i, j: (i, jnp.minimum(j, jg - 1))),
            pl.BlockSpec((tm, tn), lambda i, j: (i, jnp.clip(j - jk, 0, jr - jk - 1))),
            pl.BlockSpec((tm, tn), lambda i, j: (i, jnp.clip(j - jg, 0, nj - jg - 1))),
            pl.BlockSpec((tm, LANES), lambda i, j: (i, 0)),
        ],
        out_shape=[
            jax.ShapeDtypeStruct((m, jg * tn), F32),
            jax.ShapeDtypeStruct((m, widths[2]), BF16),
            jax.ShapeDtypeStruct((m, widths[4]), BF16),
            jax.ShapeDtypeStruct((m, LANES), F32),
        ],
        scratch_shapes=[pltpu.VMEM((tm, d), BF16)],
        compiler_params=_params("arbitrary", "arbitrary"),
        name="in_proj",
    )(x, norm, w_lo, w_hi, w_ff)


def _bias_operators(heads, width):
    hd = width // heads
    selq = np.zeros((3 * LANES, width), np.float32)
    selk = np.zeros((3 * LANES, width), np.float32)
    cq = np.zeros((1, width), np.float32)
    ck = np.zeros((1, width), np.float32)
    for h in range(heads):
        for r in range(3):
            selq[r * LANES + h, h * hd + r] = 1.0
            selk[r * LANES + h, h * hd + 3 + r] = -1.0
            cq[0, h * hd + 3 + r] = 1.0
            ck[0, h * hd + r] = 1.0
    return jnp.asarray(selq, BF16), jnp.asarray(selk, BF16), jnp.asarray(cq), jnp.asarray(ck)


def _logf_prompt_kernel(*refs, heads, hd, aliased):
    if aliased:
        refs = refs[2:]
    (ff_ref, b_ref, q_ref, k_ref, k32_ref, v32_ref, tri_ref, selq_ref, selk_ref, cq_ref, ck_ref,
     lf_ref, qa_ref, ka_ref, kp_ref, vp_ref, carry_ref) = refs
    blk = ff_ref.shape[0]
    for h in range(heads):
        kp_ref[pl.ds(h, blk, stride=heads), :] = k32_ref[:, h * hd:(h + 1) * hd]
        vp_ref[pl.ds(h, blk, stride=heads), :] = v32_ref[:, h * hd:(h + 1) * hd]

    @pl.when(pl.program_id(1) == 0)
    def _():
        carry_ref[...] = jnp.zeros(carry_ref.shape, F32)

    lf = _log_sigmoid(ff_ref[...] + b_ref[...])
    lf_ref[...] = lf
    cb = _dot(tri_ref[...], jnp.concatenate(_split3(lf), axis=0)) + carry_ref[...]
    carry_ref[...] = cb[blk - 1:blk, :]
    c3 = jnp.concatenate(_split3(cb), axis=1)
    qb = (_dot(c3, selq_ref[...]) + cq_ref[...]).astype(BF16)
    kb = (_dot(c3, selk_ref[...]) + ck_ref[...]).astype(BF16)
    q = q_ref[...].astype(BF16)
    k = k_ref[...]
    for h in range(heads):
        hs = slice(h * hd, (h + 1) * hd)
        qa_ref[:, 2 * h * hd:(2 * h + 1) * hd] = q[:, hs]
        qa_ref[:, (2 * h + 1) * hd:(2 * h + 2) * hd] = qb[:, hs]
        ka_ref[:, 2 * h * hd:(2 * h + 1) * hd] = k[:, hs]
        ka_ref[:, (2 * h + 1) * hd:(2 * h + 2) * hd] = kb[:, hs]


def _logf_prompt(ff, b_fox_pad, zf, kvb, layer, depth, batch, seq, heads, hd, qcol, prev):
    width = heads * hd
    blk = _tile(seq, 512, LANES)
    nb = seq // blk
    tri = np.tril(np.ones((blk, blk), np.float32))
    tri3 = jnp.asarray(np.concatenate([tri, tri, tri], axis=1), BF16)
    ops = _bias_operators(heads, width)
    full = lambda a: pl.BlockSpec(a.shape, lambda b, i: (0,) * a.ndim)
    rows = lambda c, w: pl.BlockSpec((blk, w), lambda b, i: (b * nb + i, c))
    cache = pl.BlockSpec((None, blk * heads, hd), lambda b, i: (layer, b * nb + i, 0))
    aliased = prev is not None
    in_specs = [
        rows(0, LANES),
        pl.BlockSpec((None, 1, LANES), lambda b, i: (layer, 0, 0)),
        rows(qcol, width),
        rows(0, width),
        rows(qcol + 1, width),
        rows(qcol + 2, width),
        full(tri3), full(ops[0]), full(ops[1]), full(ops[2]), full(ops[3]),
    ]
    args = [ff, b_fox_pad, zf, kvb, zf, zf, tri3, *ops]
    if aliased:
        in_specs = [_any_spec(), _any_spec()] + in_specs
        args = list(prev) + args
    cache_shape = jax.ShapeDtypeStruct((depth, batch * seq * heads, hd), F32)
    return pl.pallas_call(
        functools.partial(_logf_prompt_kernel, heads=heads, hd=hd, aliased=aliased),
        grid=(batch, nb),
        in_specs=in_specs,
        out_specs=[
            pl.BlockSpec((None, blk, LANES), lambda b, i: (b, i, 0)),
            rows(0, 2 * width),
            rows(0, 2 * width),
            cache,
            cache,
        ],
        out_shape=[
            jax.ShapeDtypeStruct((batch, seq, LANES), F32),
            jax.ShapeDtypeStruct((batch * seq, 2 * width), BF16),
            jax.ShapeDtypeStruct((batch * seq, 2 * width), BF16),
            cache_shape,
            cache_shape,
        ],
        scratch_shapes=[pltpu.VMEM((1, LANES), F32)],
        input_output_aliases={0: 3, 1: 4} if aliased else {},
        compiler_params=_params("arbitrary", "arbitrary"),
        name="logf_prompt",
    )(*args)


def _conv_kernel(ab_ref, ac_ref, av_ref, w_ref, prev_ref, m_ref, newc_ref, carry_ref):
    i = pl.program_id(1)
    t = ac_ref.shape[0]

    @pl.when(i == 0)
    def _():
        carry_ref[...] = prev_ref[...]

    u = ac_ref[...] * av_ref[...]
    row = lax.broadcasted_iota(jnp.int32, u.shape, 0)
    p1 = carry_ref[1:2, :]
    p2 = carry_ref[0:1, :]
    u1 = jnp.where(row == 0, p1, pltpu.roll(u, 1, 0))
    u2 = pltpu.roll(u, 2, 0)
    u2 = jnp.where(row == 0, p2, jnp.where(row == 1, p1, u2))
    w = w_ref[...]
    conv = w[0:1, :] * u2 + w[1:2, :] * u1 + w[2:3, :] * u
    m_ref[...] = (ab_ref[...] * conv).astype(m_ref.dtype)
    last2 = u[t - 2:t, :]
    carry_ref[...] = last2
    newc_ref[...] = last2


def _conv(zf, conv_w, prev, layer, row0, n_seq, seq, tile, width, out_rows, out_dtype):
    nt = seq // tile
    rb0 = row0 // tile
    rows = lambda c: pl.BlockSpec((tile, width), lambda n, i: (rb0 + n * nt + i, c))
    return pl.pallas_call(
        _conv_kernel,
        grid=(n_seq, nt),
        in_specs=[rows(0), rows(1), rows(2),
                  pl.BlockSpec((None, 3, width), lambda n, i: (layer, 0, 0)),
                  pl.BlockSpec((None, 2, width), lambda n, i: (n, 0, 0))],
        out_specs=[
            pl.BlockSpec((tile, width), lambda n, i: (n * nt + i, 0)),
            pl.BlockSpec((None, 2, width), lambda n, i: (n, 0, 0)),
        ],
        out_shape=[
            jax.ShapeDtypeStruct((out_rows, width), out_dtype),
            jax.ShapeDtypeStruct((n_seq, 2, width), F32),
        ],
        scratch_shapes=[pltpu.VMEM((2, width), F32)],
        compiler_params=_params("arbitrary", "arbitrary"),
        name="gated_conv",
    )(zf, zf, zf, conv_w, prev)


def _fox_prompt_kernel(qa_ref, ka_ref, v_ref, o_ref, m_ref, l_ref, acc_ref, *, heads, hd):
    qi = pl.program_id(1)
    ki = pl.program_id(2)
    tq = qa_ref.shape[0]
    tk = ka_ref.shape[0]

    @pl.when(ki == 0)
    def _():
        m_ref[...] = jnp.full(m_ref.shape, -jnp.inf, F32)
        l_ref[...] = jnp.zeros(l_ref.shape, F32)
        acc_ref[...] = jnp.zeros(acc_ref.shape, F32)

    def step(masked):
        if masked:
            krow = lax.broadcasted_iota(jnp.int32, (tk, tq), 0)
            qcol = lax.broadcasted_iota(jnp.int32, (tk, tq), 1)
            keep = krow <= qcol
        for h in range(heads):
            aug = slice(2 * h * hd, (2 * h + 2) * hd)
            st = _dot_nt(ka_ref[:, aug], qa_ref[:, aug])
            if masked:
                st = jnp.where(keep, st, -jnp.inf)
            m_prev = m_ref[h]
            m_new = jnp.maximum(m_prev, jnp.max(st, axis=0, keepdims=True))
            alpha = jnp.exp(m_prev - m_new)
            p = jnp.exp(st - m_new)
            l_ref[h] = alpha * l_ref[h] + jnp.sum(p, axis=0, keepdims=True)
            acc_ref[h] = alpha * acc_ref[h] + _dot_tn(v_ref[:, h * hd:(h + 1) * hd], p.astype(BF16))
            m_ref[h] = m_new

    @pl.when(ki < qi)
    def _():
        step(False)

    @pl.when(ki == qi)
    def _():
        step(True)
        for h in range(heads):
            o_ref[:, h * hd:(h + 1) * hd] = (acc_ref[h] / l_ref[h]).T.astype(o_ref.dtype)


def _fox_prompt(qaug, kaug, kvb, m_rows, batch, seq, heads, hd, tq):
    nq = seq // tq
    width = heads * hd
    kern = functools.partial(_fox_prompt_kernel, heads=heads, hd=hd)
    qmap = lambda b, qi, ki: (b * nq + qi, 0)
    kmap = lambda b, qi, ki: (b * nq + jnp.minimum(ki, qi), 0)
    vmap = lambda b, qi, ki: (b * nq + jnp.minimum(ki, qi), 1)
    return pl.pallas_call(
        kern,
        grid=(batch, nq, nq),
        in_specs=[
            pl.BlockSpec((tq, 2 * width), qmap),
            pl.BlockSpec((tq, 2 * width), kmap),
            pl.BlockSpec((tq, width), vmap),
        ],
        out_specs=pl.BlockSpec((tq, width), qmap),
        out_shape=jax.ShapeDtypeStruct((m_rows, width), BF16),
        scratch_shapes=[
            pltpu.VMEM((heads, 1, tq), F32),
            pltpu.VMEM((heads, 1, tq), F32),
            pltpu.VMEM((heads, hd, tq), F32),
        ],
        compiler_params=_params("arbitrary", "arbitrary", "arbitrary"),
        name="fox_prompt",
    )(qaug, kaug, kvb)


def _gather_rows_kernel(pt_ref, *refs):
    n_in = len(refs) - 1
    out_ref = refs[-1]
    for r in range(n_in):
        out_ref[:, r, :] = refs[r][:, 0, :]


def _gather_logf_pages(cache_logf_flat, page_table, group):
    depth, _, _, width = cache_logf_flat.shape
    nb, n_pages = page_table.shape
    ng = n_pages // group

    def in_map(r):
        return lambda b, g, pt: (0, pt[b, g * group + r], 0, 0)

    return pl.pallas_call(
        _gather_rows_kernel,
        grid_spec=pltpu.PrefetchScalarGridSpec(
            num_scalar_prefetch=1,
            grid=(nb, ng),
            in_specs=[pl.BlockSpec((depth, None, 1, width), in_map(r)) for r in range(group)],
            out_specs=pl.BlockSpec((depth, None, group, width), lambda b, g, pt: (0, b, g, 0)),
        ),
        out_shape=jax.ShapeDtypeStruct((depth, nb, n_pages, width), F32),
        compiler_params=_params("arbitrary", "arbitrary"),
        name="gather_logf",
    )(page_table, *([cache_logf_flat] * group))


def _past_cumsum_kernel(x_ref, t3_ref, s3_ref, l3_ref, e3_ref, c_ref, tot_ref):
    x3 = jnp.concatenate(_split3(x_ref[...]), axis=1)
    y = _dot(x3, t3_ref[...])
    tot = _dot(x3, s3_ref[...])
    offs = _dot(l3_ref[...], jnp.concatenate(_split3(tot), axis=0))
    c_ref[...] = y + _dot(jnp.concatenate(_split3(offs), axis=1), e3_ref[...])
    n = tot.shape[0]
    tot_ref[...] = offs[n - 1:n, :] + tot[n - 1:n, :]


def _past_cumsum(lf_pages, heads):
    depth, nb, n_pages, width = lf_pages.shape
    pos = np.arange(width) // heads
    hd = np.arange(width) % heads
    t = ((hd[:, None] == hd[None, :]) & (pos[:, None] <= pos[None, :])).astype(np.float32)
    ssum = (hd[:, None] == np.arange(LANES)[None, :]).astype(np.float32)
    lstrict = np.tril(np.ones((n_pages, n_pages), np.float32), -1)
    e = (np.arange(LANES)[:, None] == hd[None, :]).astype(np.float32)
    t3 = jnp.asarray(np.concatenate([t, t, t], axis=0), BF16)
    s3 = jnp.asarray(np.concatenate([ssum, ssum, ssum], axis=0), BF16)
    l3 = jnp.asarray(np.concatenate([lstrict, lstrict, lstrict], axis=1), BF16)
    e3 = jnp.asarray(np.concatenate([e, e, e], axis=0), BF16)
    x = lf_pages.reshape(depth * nb, n_pages, width)
    full = lambda a: pl.BlockSpec(a.shape, lambda g: (0,) * a.ndim)
    return pl.pallas_call(
        _past_cumsum_kernel,
        grid=(depth * nb,),
        in_specs=[pl.BlockSpec((None, n_pages, width), lambda g: (g, 0, 0)),
                  full(t3), full(s3), full(l3), full(e3)],
        out_specs=[pl.BlockSpec((None, n_pages, width), lambda g: (g, 0, 0)),
                   pl.BlockSpec((None, 1, LANES), lambda g: (g, 0, 0))],
        out_shape=[jax.ShapeDtypeStruct((depth * nb, n_pages, width), F32),
                   jax.ShapeDtypeStruct((depth * nb, 1, LANES), F32)],
        compiler_params=_params("arbitrary"),
        name="past_cumsum",
    )(x, t3, s3, l3, e3)


def _fox_sample_kernel(pt_ref, *refs, group, heads, hd, nt, aliased):
    if aliased:
        refs = refs[2:]
    q_ref, kn_ref, vn_ref, ff_ref, b_ref, ctot_ref, cp_ref = refs[:7]
    refs = refs[7:]
    k_refs = refs[:group]
    v_refs = refs[group:2 * group]
    o_ref, lf_ref, ks_ref, vs_ref = refs[2 * group:2 * group + 4]
    qall_ref, cqm_ref, m_ref, l_ref, acc_ref = refs[2 * group + 4:]
    g = pl.program_id(1)
    ng = pl.num_programs(1)
    nrow = heads * nt
    prow = k_refs[0].shape[0]
    assert heads & (heads - 1) == 0 and nt & (nt - 1) == 0
    hbits = heads.bit_length() - 1
    tbits = nt.bit_length() - 1

    def head_of_col(cols):
        return jnp.bitwise_and(cols, heads - 1)

    @pl.when(g == 0)
    def _():
        lf = _log_sigmoid(ff_ref[...] + b_ref[...])
        lf_ref[...] = lf
        trow = lax.broadcasted_iota(jnp.int32, lf.shape, 0)
        tlane = lax.broadcasted_iota(jnp.int32, lf.shape, 1)
        cs = lf
        sh = 1
        while sh < nt:
            cs = cs + jnp.where(trow >= sh, pltpu.roll(cs, sh, 0), 0.0)
            sh *= 2
        cn = ctot_ref[...] + cs
        cq = jnp.concatenate([cn[:, h:h + 1] for h in range(heads)], axis=0)
        prows = lax.broadcasted_iota(jnp.int32, (nrow, prow), 0)
        pcols = lax.broadcasted_iota(jnp.int32, (nrow, prow), 1)
        cqm_ref[...] = jnp.where(head_of_col(pcols) == jnp.right_shift(prows, tbits), cq, -jnp.inf)
        qf = q_ref[...]
        qall = jnp.concatenate([qf[:, h * hd:(h + 1) * hd] for h in range(heads)], axis=0)
        qall_ref[...] = qall.astype(BF16)
        cnm = jnp.where(tlane < heads, cn, 0.0)
        ck = jnp.zeros((1, LANES), F32)
        for t in range(nt):
            rolled = pltpu.roll(cnm, t * heads, 1)
            ck = ck + jnp.sum(jnp.where(trow == t, rolled, 0.0), axis=0, keepdims=True)
        for h in range(heads):
            ks_ref[pl.ds(h, nt, stride=heads), :] = kn_ref[:, h * hd:(h + 1) * hd]
            vs_ref[pl.ds(h, nt, stride=heads), :] = vn_ref[:, h * hd:(h + 1) * hd]
        kn = _pad_rows(ks_ref[...], LANES).astype(BF16)
        vn = _pad_rows(vs_ref[...], LANES).astype(BF16)
        s = _dot_nt(qall_ref[...], kn) + cq - ck
        rows = lax.broadcasted_iota(jnp.int32, s.shape, 0)
        cols = lax.broadcasted_iota(jnp.int32, s.shape, 1)
        ok = (head_of_col(cols) == jnp.right_shift(rows, tbits))
        ok = ok & (jnp.right_shift(cols, hbits) <= jnp.bitwise_and(rows, nt - 1)) & (cols < nrow)
        s = jnp.where(ok, s, -jnp.inf)
        m0 = jnp.max(s, axis=-1, keepdims=True)
        p = jnp.exp(s - m0)
        m_ref[...] = m0
        l_ref[...] = jnp.sum(p, axis=-1, keepdims=True)
        acc_ref[...] = _dot(p.astype(BF16), vn)

    qall = qall_ref[...]
    cqm = cqm_ref[...]
    ss = []
    for r in range(group):
        kb = k_refs[r][...].astype(BF16)
        ss.append(_dot_nt(qall, kb) + (cqm - cp_ref[r:r + 1, :]))
    mx = ss[0]
    for s in ss[1:]:
        mx = jnp.maximum(mx, s)
    m_prev = m_ref[...]
    m_new = jnp.maximum(m_prev, jnp.max(mx, axis=-1, keepdims=True))
    alpha = jnp.exp(m_prev - m_new)
    acc = alpha * acc_ref[...]
    psum = jnp.zeros((nrow, prow), F32)
    for r in range(group):
        p = jnp.exp(ss[r] - m_new)
        psum = psum + p
        acc = acc + _dot(p.astype(BF16), v_refs[r][...].astype(BF16))
    l_ref[...] = alpha * l_ref[...] + jnp.sum(psum, axis=-1, keepdims=True)
    acc_ref[...] = acc
    m_ref[...] = m_new

    @pl.when(g == ng - 1)
    def _():
        o = acc_ref[...] / l_ref[...]
        for h in range(heads):
            o_ref[:, h * hd:(h + 1) * hd] = o[h * nt:(h + 1) * nt, :]


def _fox_sample(zf, ff, b_fox_pad, ctot, c_past, cache_k, cache_v, page_table, layer, depth,
                row0, nt, heads, hd, group, qcol, prev):
    nb, n_pages = page_table.shape
    prow = cache_k.shape[2]
    width = heads * hd
    ng = n_pages // group
    rb0 = row0 // nt
    nrow = heads * nt
    aliased = prev is not None

    def page_map(r):
        return lambda b, g, pt: (layer, pt[b, g * group + r], 0, 0)

    kern = functools.partial(_fox_sample_kernel, group=group, heads=heads, hd=hd, nt=nt, aliased=aliased)
    new_rows = pl.BlockSpec((None, nrow, hd), lambda b, g, pt: (layer, b, 0))
    new_shape = jax.ShapeDtypeStruct((depth, nb * nrow, hd), F32)
    in_specs = ([_any_spec(), _any_spec()] if aliased else []) + [
        pl.BlockSpec((nt, width), lambda b, g, pt: (rb0 + b, qcol)),
        pl.BlockSpec((nt, width), lambda b, g, pt: (rb0 + b, qcol + 1)),
        pl.BlockSpec((nt, width), lambda b, g, pt: (rb0 + b, qcol + 2)),
        pl.BlockSpec((nt, LANES), lambda b, g, pt: (rb0 + b, 0)),
        pl.BlockSpec((None, 1, LANES), lambda b, g, pt: (layer, 0, 0)),
        pl.BlockSpec((None, 1, LANES), lambda b, g, pt: (layer * nb + b, 0, 0)),
        pl.BlockSpec((None, group, c_past.shape[2]), lambda b, g, pt: (layer * nb + b, g, 0)),
    ]
    in_specs += [pl.BlockSpec((None, None, prow, hd), page_map(r)) for r in range(group)]
    in_specs += [pl.BlockSpec((None, None, prow, hd), page_map(r)) for r in range(group)]
    return pl.pallas_call(
        kern,
        grid_spec=pltpu.PrefetchScalarGridSpec(
            num_scalar_prefetch=1,
            grid=(nb, ng),
            in_specs=in_specs,
            out_specs=[pl.BlockSpec((nt, width), lambda b, g, pt: (b, 0)),
                       pl.BlockSpec((None, nt, LANES), lambda b, g, pt: (b, 0, 0)),
                       new_rows, new_rows],
            scratch_shapes=[
                pltpu.VMEM((nrow, hd), BF16),
                pltpu.VMEM((nrow, prow), F32),
                pltpu.VMEM((nrow, 1), F32),
                pltpu.VMEM((nrow, 1), F32),
                pltpu.VMEM((nrow, hd), F32),
            ],
        ),
        out_shape=[jax.ShapeDtypeStruct((nb * nt, width), F32),
                   jax.ShapeDtypeStruct((nb, nt, LANES), F32),
                   new_shape, new_shape],
        input_output_aliases={1: 2, 2: 3} if aliased else {},
        compiler_params=_params("arbitrary", "arbitrary"),
        name="fox_sample",
    )(page_table, *(prev if aliased else ()), zf, zf, zf, ff, b_fox_pad, ctot, c_past,
      *([cache_k] * group), *([cache_v] * group))


def _hgrn_operators(c, real):
    t = np.arange(c)[:, None]
    j = np.arange(c)[None, :]
    mats = [(j <= t), (j > t)]
    masks = []
    m = 1
    while m < real:
        blk = t // (2 * m)
        mid = blk * (2 * m) + m - 1
        second = t > mid
        if m < SUBLANES:
            mats.append(np.where(second, (j > mid) & (j <= t), (j > t) & (j <= mid)))
        s = np.arange(c)[None, :]
        masks.append((blk == (s // (2 * m))) & second & (s <= (s // (2 * m)) * (2 * m) + m - 1))
        m *= 2
    a = np.concatenate(mats, axis=0).astype(np.float32)
    a3 = np.concatenate([a, a, a], axis=1)
    return jnp.asarray(a3, BF16), jnp.asarray(np.stack(masks).astype(np.float32)), len(masks)


def _hgrn_kernel(q_ref, f_ref, i_ref, g_ref, lbraw_ref, gn_ref, a3_ref, msk_ref, s0_ref,
                 o_ref, sfin_ref, st_ref, *, layer, heads, dk, n_lev, real):
    c = HGRN_CHUNK
    ci = pl.program_id(1)
    nc = pl.num_programs(1)

    @pl.when(ci == 0)
    def _():
        for h in range(heads):
            st_ref[h] = s0_ref[h].T

    raw = lbraw_ref[...]
    e = jnp.exp(raw - jnp.max(raw, axis=0, keepdims=True))
    sm = e / jnp.sum(e, axis=0, keepdims=True)
    cs = sm[0:1]
    for r in range(1, layer + 1):
        cs = cs + sm[r:r + 1]
    lb = cs - sm[0:1]
    log_lb = jnp.log(lb)
    log_1m = jnp.log1p(-lb)

    zf = _pad_rows(f_ref[...], c)
    q = _pad_rows(q_ref[...], c)
    v = _pad_rows(i_ref[...], c)
    b = log_1m + _log_sigmoid(zf)
    mx = jnp.maximum(log_lb, b)
    g = mx + jnp.log(1.0 + jnp.exp(-jnp.abs(log_lb - b)))
    kr = (1.0 - lb) * _sigmoid(-zf)
    if real < c:
        valid = lax.broadcasted_iota(jnp.int32, g.shape, 0) < real
        g = jnp.where(valid, g, 0.0)
        kr = jnp.where(valid, kr, 0.0)

    e = _dot(a3_ref[...], jnp.concatenate(_split3(g), axis=0))
    eq = e[0:c]
    xq = jnp.exp(eq)
    xk = jnp.exp(e[c:2 * c])
    xlast = xq[c - 1:c]
    n_mx = e.shape[0] // c - 2

    def level_factor(l):
        if l < n_mx:
            return jnp.exp(e[(2 + l) * c:(3 + l) * c])
        m = 1 << l
        mids = [jnp.broadcast_to(eq[b0 + m - 1:b0 + m, :], (2 * m, eq.shape[1]))
                for b0 in range(0, c, 2 * m)]
        r = mids[0] if len(mids) == 1 else jnp.concatenate(mids, axis=0)
        return jnp.exp(-jnp.abs(eq - r))

    xls = [level_factor(l) for l in range(n_lev)]
    qe = (q * xq).astype(BF16)
    ke = (kr * xk).astype(BF16)
    vb = v.astype(BF16)
    for h in range(heads):
        hs = slice(h * dk, (h + 1) * dk)
        st = st_ref[h]
        o = _dot_nt(qe[:, hs], st.astype(BF16))
        att = jnp.zeros((c, c), F32)
        for l in range(n_lev):
            xl = xls[l][:, hs]
            p = _dot_nt((q[:, hs] * xl).astype(BF16), (kr[:, hs] * xl).astype(BF16))
            att = att + p * msk_ref[l]
        o = o + _dot(att.astype(BF16), vb[:, hs])
        o = o + jnp.sum(q[:, hs] * kr[:, hs], axis=-1, keepdims=True) * v[:, hs]
        st_ref[h] = st * xlast[:, hs] + _dot_tn(vb[:, hs], ke[:, hs])
        on = _rms_rows(o, gn_ref[:, hs])
        gg = g_ref[:, hs]
        o_ref[:, hs] = (on[:real] * (gg * _sigmoid(gg))).astype(o_ref.dtype)

    @pl.when(ci == nc - 1)
    def _():
        for h in range(heads):
            sfin_ref[h] = st_ref[h].T


def _hgrn(zf, hgrn_lb, hgrn_norm, s0, layer, row0, n_seq, seq, heads, dk, col0, out_rows, out_dtype):
    c = HGRN_CHUNK
    real = min(seq, c)
    assert seq % real == 0
    nc = seq // real
    width = heads * dk
    rb0 = row0 // real
    a3, masks, n_lev = _hgrn_operators(c, real)
    depth = hgrn_lb.shape[0]
    blk = lambda col: pl.BlockSpec((real, width), lambda n, i: (rb0 + n * nc + i, col))
    kern = functools.partial(_hgrn_kernel, layer=layer, heads=heads, dk=dk, n_lev=n_lev, real=real)
    in_specs = [
        blk(col0), blk(col0 + 1), blk(col0 + 2), blk(col0 + 3),
        pl.BlockSpec((depth, width), lambda n, i: (0, 0)),
        pl.BlockSpec((None, 1, width), lambda n, i: (layer, 0, 0)),
        pl.BlockSpec(a3.shape, lambda n, i: (0, 0)),
        pl.BlockSpec(masks.shape, lambda n, i: (0, 0, 0)),
        pl.BlockSpec((None, heads, dk, dk), lambda n, i: (n, 0, 0, 0)),
    ]
    return pl.pallas_call(
        kern,
        grid=(n_seq, nc),
        in_specs=in_specs,
        out_specs=[
            pl.BlockSpec((real, width), lambda n, i: (n * nc + i, 0)),
            pl.BlockSpec((None, heads, dk, dk), lambda n, i: (n, 0, 0, 0)),
        ],
        out_shape=[
            jax.ShapeDtypeStruct((out_rows, width), out_dtype),
            jax.ShapeDtypeStruct((n_seq, heads, dk, dk), F32),
        ],
        scratch_shapes=[pltpu.VMEM((heads, dk, dk), F32)],
        compiler_params=_params("arbitrary", "arbitrary"),
        name="hgrn",
    )(zf, zf, zf, zf, hgrn_lb, hgrn_norm, a3, masks, s0)


def _patch_rows_kernel(*refs):
    n = len(refs) // 3
    for small_ref, out_ref in zip(refs[n:2 * n], refs[2 * n:]):
        out_ref[...] = small_ref[...].astype(out_ref.dtype)


def _patch_rows(bigs, smalls, row0):
    rows = smalls[0].shape[0]
    assert row0 % rows == 0
    n = len(bigs)
    return pl.pallas_call(
        _patch_rows_kernel,
        grid=(1,),
        in_specs=[_any_spec()] * n + [pl.BlockSpec(s.shape, lambda i: (0, 0)) for s in smalls],
        out_specs=[pl.BlockSpec((rows, b.shape[1]), lambda i: (row0 // rows, 0)) for b in bigs],
        out_shape=[jax.ShapeDtypeStruct(b.shape, b.dtype) for b in bigs],
        input_output_aliases={i: i for i in range(n)},
        compiler_params=_params("arbitrary"),
        name="patch_rows",
    )(*bigs, *smalls)


def _merge_kernel(ma_ref, ob_ref, oc_ref, wa_ref, wb_ref, wc_ref, ga_ref, gb_ref, gc_ref, mix_ref):
    ya = _dot(ma_ref[...], wa_ref[...])
    yb = _dot(ob_ref[...], wb_ref[...])
    yc = _dot(oc_ref[...], wc_ref[...])
    gate = lambda r: _sigmoid(r[...].astype(F32))
    mix = gate(ga_ref) * ya + gate(gb_ref) * yb + gate(gc_ref) * yc
    mix_ref[...] = mix.astype(mix_ref.dtype)


def _merge(m_a, o_b, o_c, w_a, w_b, w_c, zg, layer, tm, tn):
    m, k = m_a.shape
    d = w_a.shape[2]
    nj = d // tn
    act = pl.BlockSpec((tm, k), lambda i, j: (i, 0))
    wsp = pl.BlockSpec((None, k, tn), lambda i, j: (layer, 0, j))
    gate = lambda r: pl.BlockSpec((tm, tn), lambda i, j: (i, r * nj + j))
    return pl.pallas_call(
        _merge_kernel,
        grid=(m // tm, nj),
        in_specs=[act, act, act, wsp, wsp, wsp, gate(0), gate(1), gate(2)],
        out_specs=pl.BlockSpec((tm, tn), lambda i, j: (i, j)),
        out_shape=jax.ShapeDtypeStruct((m, d), BF16),
        compiler_params=_params("arbitrary", "arbitrary"),
        name="merge",
    )(m_a, o_b, o_c, w_a, w_b, w_c, zg, zg, zg)


def _residual_matmul_kernel(a_ref, w_ref, x_ref, o_ref):
    o_ref[...] = x_ref[...] + _dot(a_ref[...], w_ref[...])


def _weight_spec(shape, index_map, resident):
    if resident:
        return pl.BlockSpec(shape, index_map, pipeline_mode=pl.Buffered(1))
    return pl.BlockSpec(shape, index_map)


def _residual_matmul(a, w, x, layer, tm, tn):
    m, k = a.shape
    d = w.shape[2]
    return pl.pallas_call(
        _residual_matmul_kernel,
        grid=(m // tm, d // tn),
        in_specs=[
            pl.BlockSpec((tm, k), lambda i, j: (i, 0)),
            _weight_spec((None, k, tn), lambda i, j: (layer, 0, j), tn == d),
            pl.BlockSpec((tm, tn), lambda i, j: (i, j)),
        ],
        out_specs=pl.BlockSpec((tm, tn), lambda i, j: (i, j)),
        out_shape=jax.ShapeDtypeStruct((m, d), F32),
        compiler_params=_params("arbitrary", "arbitrary"),
        name="residual_matmul",
    )(a, w, x)


def _swiglu_up_kernel(x_ref, g_ref, wg_ref, wu_ref, a_ref, h_ref):
    @pl.when(pl.program_id(1) == 0)
    def _():
        h_ref[...] = _rms_rows(x_ref[...], g_ref[...]).astype(BF16)

    h = h_ref[...]
    gate = _dot(h, wg_ref[...])
    up = _dot(h, wu_ref[...])
    a_ref[...] = (gate * _sigmoid(gate) * up).astype(a_ref.dtype)


def _swiglu_up(x, norm, w_gate_up, layer, tm, tn):
    m, d = x.shape
    f = w_gate_up.shape[2] // 2
    nj = f // tn
    return pl.pallas_call(
        _swiglu_up_kernel,
        grid=(m // tm, nj),
        in_specs=[
            pl.BlockSpec((tm, d), lambda i, j: (i, 0)),
            pl.BlockSpec((None, 1, d), lambda i, j: (layer, 0, 0)),
            pl.BlockSpec((None, d, tn), lambda i, j: (layer, 0, j)),
            pl.BlockSpec((None, d, tn), lambda i, j: (layer, 0, nj + j)),
        ],
        out_specs=pl.BlockSpec((tm, tn), lambda i, j: (i, j)),
        out_shape=jax.ShapeDtypeStruct((m, f), BF16),
        scratch_shapes=[pltpu.VMEM((tm, d), BF16)],
        compiler_params=_params("arbitrary", "arbitrary"),
        name="swiglu_up",
    )(x, norm, w_gate_up, w_gate_up)


def _ple_kernel(x_ref, g_ref, wg_ref, p_ref, wp_ref, o_ref, h_ref, *, tn):
    j = pl.program_id(1)

    @pl.when(j == 0)
    def _():
        h_ref[...] = _rms_rows(x_ref[...], g_ref[...]).astype(BF16)

    gate = _sigmoid(_dot(h_ref[...], wg_ref[...]))
    emb = _dot(p_ref[...].astype(BF16), wp_ref[...])
    xs = x_ref[:, pl.ds(pl.multiple_of(j * tn, tn), tn)]
    o_ref[...] = xs + gate * emb


def _ple(x, norm, w_gate, p, w_proj, layer, tm, tn):
    m, d = x.shape
    pd = p.shape[2]
    return pl.pallas_call(
        functools.partial(_ple_kernel, tn=tn),
        grid=(m // tm, d // tn),
        in_specs=[
            pl.BlockSpec((tm, d), lambda i, j: (i, 0)),
            pl.BlockSpec((None, 1, d), lambda i, j: (layer, 0, 0)),
            _weight_spec((None, d, tn), lambda i, j: (layer, 0, j), tn == d),
            pl.BlockSpec((None, tm, pd), lambda i, j: (layer, i, 0)),
            _weight_spec((None, pd, tn), lambda i, j: (layer, 0, j), tn == d),
        ],
        out_specs=pl.BlockSpec((tm, tn), lambda i, j: (i, j)),
        out_shape=jax.ShapeDtypeStruct((m, d), F32),
        scratch_shapes=[pltpu.VMEM((tm, d), BF16)],
        compiler_params=_params("arbitrary", "arbitrary"),
        name="ple",
    )(x, norm, w_gate, p, w_proj)


def _final_norm_kernel(x_ref, g_ref, o_ref):
    o_ref[...] = _rms_rows(x_ref[...], g_ref[...])


def _final_norm(x, g, tm):
    m, d = x.shape
    return pl.pallas_call(
        _final_norm_kernel,
        grid=(m // tm,),
        in_specs=[pl.BlockSpec((tm, d), lambda i: (i, 0)), pl.BlockSpec((1, d), lambda i: (0, 0))],
        out_specs=pl.BlockSpec((tm, d), lambda i: (i, 0)),
        out_shape=jax.ShapeDtypeStruct((m, d), F32),
        compiler_params=_params("arbitrary"),
        name="final_norm",
    )(x, g)


def kernel(x_prompt, x_sample, cache_k, cache_v, cache_logf, state_conv, state_hgrn, page_table,
           p_prompt, p_sample, norm1, w_in, b_fox, conv_w, hgrn_lb, hgrn_norm, w_a, w_b, w_c, w_o,
           norm2, w_gate_up, w_down, ple_norm, ple_gate, ple_proj, final_norm):
    bp, seq, d = x_prompt.shape
    db, dseq, _ = x_sample.shape
    depth = w_in.shape[0]
    heads, hd = cache_k.shape[3], cache_k.shape[4]
    page = cache_k.shape[2]
    n_pool = cache_k.shape[1]
    cw = w_a.shape[1]
    fw = w_b.shape[1]
    rw = w_c.shape[1]
    rheads = state_hgrn.shape[2]
    dk = state_hgrn.shape[3]
    assert fw == heads * hd and rw == rheads * dk and state_hgrn.shape[4] == dk and cw == fw == rw
    mp = bp * seq
    ms = db * dseq
    m = mp + ms

    o_ff = 3 * cw + 3 * fw
    widths = (3 * cw, fw, 2 * fw, 4 * rw, 3 * d)
    w_lo = w_in[:, :, :o_ff].astype(BF16)
    w_hi = w_in[:, :, o_ff + heads:].astype(BF16)
    w_ff = jnp.pad(w_in[:, :, o_ff:o_ff + heads], ((0, 0), (0, 0), (0, LANES - heads))).astype(BF16)
    assert w_lo.shape[2] + w_hi.shape[2] == sum(widths)
    tn_in = fw
    qcol = 3 * cw // tn_in
    rcol = qcol + 3

    bf = lambda a: a.astype(BF16)
    w_a_b, w_b_b, w_c_b, w_o_b = bf(w_a), bf(w_b), bf(w_c), bf(w_o)
    w_gu_b, w_dn_b, w_pg_b, w_pp_b = bf(w_gate_up), bf(w_down), bf(ple_gate), bf(ple_proj)
    row3 = lambda a: a.reshape(a.shape[0], 1, a.shape[1])
    norm1_r, norm2_r, ple_norm_r, hnorm_r = row3(norm1), row3(norm2), row3(ple_norm), row3(hgrn_norm)
    b_fox_pad = row3(jnp.pad(b_fox, ((0, 0), (0, LANES - heads))))

    x = _stack_rows(x_prompt.reshape(1, mp, d), x_sample.reshape(1, ms, d)).reshape(m, d)
    p_all = _stack_rows(p_prompt.reshape(depth, mp, -1), p_sample.reshape(depth, ms, -1))
    ck = cache_k.reshape(depth, n_pool, page * heads, hd)
    cv = cache_v.reshape(depth, n_pool, page * heads, hd)

    n_pages = page_table.shape[1]
    group = _tile(n_pages, 16, 1)
    lf_flat = cache_logf.reshape(depth, n_pool, 1, page * heads)
    lf_pages = _gather_logf_pages(lf_flat, page_table, group)
    c_past, c_tot = _past_cumsum(lf_pages, heads)

    tm = _tile(m, 704, 16)
    tq = _tile(seq, 512, LANES)
    conv_tile = _tile(seq, 512, 8)
    zeros_conv = jnp.zeros((bp, 2, cw), F32)
    zeros_state = jnp.zeros((bp, rheads, dk, dk), F32)

    outs = {k: [] for k in ("lfp", "cp", "sp", "lfs", "cs", "ss")}
    kvp = None
    kvs = None
    for i in range(depth):
        zf, kvb, zg, ff = _in_proj(x, norm1_r, w_lo, w_hi, w_ff, i, tm, tn_in, widths, hd)

        m_a, conv_p = _conv(zf, conv_w, zeros_conv, i, 0, bp, seq, conv_tile, cw, m, BF16)
        ma_s, conv_s = _conv(zf, conv_w, state_conv[i], i, mp, db, dseq, dseq, cw, ms, F32)

        lf_p, qaug, kaug, kp, vp = _logf_prompt(ff, b_fox_pad, zf, kvb, i, depth, bp, seq, heads, hd,
                                                qcol, kvp)
        kvp = (kp, vp)
        o_b = _fox_prompt(qaug, kaug, kvb, m, bp, seq, heads, hd, tq)
        ob_s, lf_s, ks, vs = _fox_sample(zf, ff, b_fox_pad, c_tot, c_past, ck, cv, page_table, i,
                                         depth, mp, dseq, heads, hd, group, qcol, kvs)
        kvs = (ks, vs)

        o_c, st_p = _hgrn(zf, hgrn_lb, hnorm_r, zeros_state, i, 0, bp, seq, rheads, dk, rcol, m, BF16)
        oc_s, st_s = _hgrn(zf, hgrn_lb, hnorm_r, state_hgrn[i], i, mp, db, dseq, rheads, dk, rcol,
                           ms, F32)

        m_a, o_b, o_c = _patch_rows((m_a, o_b, o_c), (ma_s, ob_s, oc_s), mp)
        mix = _merge(m_a, o_b, o_c, w_a_b, w_b_b, w_c_b, zg, i, tm, 1024)
        x = _residual_matmul(mix, w_o_b, x, i, tm, d)

        act = _swiglu_up(x, norm2_r, w_gu_b, i, tm, _tile(w_gate_up.shape[2] // 2, 512, 2 * LANES))
        x = _residual_matmul(act, w_dn_b, x, i, tm, 512)
        x = _ple(x, ple_norm_r, w_pg_b, p_all, w_pp_b, i, tm, d)

        outs["lfp"].append(lf_p[:, :, :heads])
        outs["lfs"].append(lf_s[:, :, :heads])
        outs["cp"].append(conv_p)
        outs["cs"].append(conv_s)
        outs["sp"].append(st_p)
        outs["ss"].append(st_s)

    y = _final_norm(x, final_norm.reshape(1, d), tm)
    st = lambda k: jnp.stack(outs[k])
    return (y[:mp].reshape(bp, seq, d), y[mp:].reshape(db, dseq, d),
            kp.reshape(depth, bp, seq, heads, hd), vp.reshape(depth, bp, seq, heads, hd),
            st("lfp"), st("cp"), st("sp"),
            ks.reshape(depth, db, dseq, heads, hd), vs.reshape(depth, db, dseq, heads, hd),
            st("lfs"), st("cs"), st("ss"))
```

```python
import functools

import numpy as np
import jax
import jax.numpy as jnp
from jax import lax
from jax.experimental import pallas as pl
from jax.experimental.pallas import tpu as pltpu

F32 = jnp.float32
BF16 = jnp.bfloat16
RMS_EPS = 1e-6
LANES = 128
SUBLANES = 8
HGRN_CHUNK = 128
VMEM_LIMIT = 56 * 1024 * 1024


def _tile(n, target, align):
    best = None
    for t in range(align, min(n, target) + 1, align):
        if n % t == 0:
            best = t
    assert best is not None, (n, target, align)
    return best


def _params(*sem):
    return pltpu.CompilerParams(dimension_semantics=sem, vmem_limit_bytes=VMEM_LIMIT)


def _sigmoid(x):
    return 1.0 / (1.0 + jnp.exp(-x))


def _log_sigmoid(x):
    return jnp.minimum(x, 0.0) - jnp.log(1.0 + jnp.exp(-jnp.abs(x)))


def _rms_rows(x, g):
    ms = jnp.mean(x * x, axis=-1, keepdims=True)
    return x * lax.rsqrt(ms + RMS_EPS) * g


def _split3(x):
    hi = x.astype(BF16)
    r1 = x - hi.astype(F32)
    mid = r1.astype(BF16)
    lo = (r1 - mid.astype(F32)).astype(BF16)
    return hi, mid, lo


def _dot(a, b):
    return jnp.dot(a, b, preferred_element_type=F32)


def _dot_nt(a, b):
    return lax.dot_general(a, b, (((1,), (1,)), ((), ())), preferred_element_type=F32)


def _dot_tn(a, b):
    return lax.dot_general(a, b, (((0,), (0,)), ((), ())), preferred_element_type=F32)


def _pad_rows(a, n):
    if a.shape[0] == n:
        return a
    return jnp.concatenate([a, jnp.zeros((n - a.shape[0], a.shape[1]), a.dtype)], axis=0)


def _any_spec():
    return pl.BlockSpec(memory_space=pl.ANY)


def _copy_kernel(*refs, aliased):
    a_ref, o_ref = refs[1:] if aliased else refs
    o_ref[...] = a_ref[...]


def _stack_rows(a, b):
    nl, ra, c = a.shape
    rb = b.shape[1]
    assert ra % rb == 0
    ta = _tile(ra, 1024, 8)
    shape = jax.ShapeDtypeStruct((nl, ra + rb, c), a.dtype)
    out = pl.pallas_call(
        functools.partial(_copy_kernel, aliased=False),
        grid=(nl, ra // ta),
        in_specs=[pl.BlockSpec((None, ta, c), lambda l, i: (l, i, 0))],
        out_specs=pl.BlockSpec((None, ta, c), lambda l, i: (l, i, 0)),
        out_shape=shape,
        compiler_params=_params("arbitrary", "arbitrary"),
        name="stack_rows_a",
    )(a)
    return pl.pallas_call(
        functools.partial(_copy_kernel, aliased=True),
        grid=(nl,),
        in_specs=[_any_spec(), pl.BlockSpec((None, rb, c), lambda l: (l, 0, 0))],
        out_specs=pl.BlockSpec((None, rb, c), lambda l: (l, ra // rb, 0)),
        out_shape=shape,
        input_output_aliases={0: 0},
        compiler_params=_params("arbitrary"),
        name="stack_rows_b",
    )(out, b)


def _repack_kernel(a_ref, s_ref, o_ref, *, j_shift, shift):
    j = pl.program_id(2)
    tn = a_ref.shape[1]

    @pl.when(j < j_shift)
    def _():
        o_ref[...] = a_ref[...].astype(o_ref.dtype)

    @pl.when(j >= j_shift)
    def _():
        a = pltpu.roll(a_ref[...], tn - shift, 1)
        nxt = pltpu.roll(s_ref[...], LANES - shift, 1)
        lane = lax.broadcasted_iota(jnp.int32, nxt.shape, 1)
        o_ref[:, :tn - LANES] = a[:, :tn - LANES].astype(o_ref.dtype)
        o_ref[:, tn - LANES:] = jnp.where(lane < LANES - shift, a[:, tn - LANES:], nxt).astype(o_ref.dtype)


def _repack_w_in(w_in, o_ff, shift, tn):
    depth, d, n = w_in.shape
    n_out = n - shift
    assert o_ff % tn == 0 and n_out % tn == 0 and shift < LANES
    j_shift = o_ff // tn
    nj = n_out // tn
    heads = jnp.stack([w_in[:, :, (j + 1) * tn:(j + 1) * tn + shift] for j in range(j_shift, nj)], axis=2)
    heads = jnp.pad(heads, ((0, 0), (0, 0), (0, 0), (0, LANES - shift))).reshape(depth, d, -1)
    rows = _tile(d, 512, 8)
    return pl.pallas_call(
        functools.partial(_repack_kernel, j_shift=j_shift, shift=shift),
        grid=(depth, d // rows, nj),
        in_specs=[
            pl.BlockSpec((None, rows, tn), lambda l, r, j: (l, r, j)),
            pl.BlockSpec((None, rows, LANES), lambda l, r, j: (l, r, jnp.maximum(j - j_shift, 0))),
        ],
        out_specs=pl.BlockSpec((None, rows, tn), lambda l, r, j: (l, r, j)),
        out_shape=jax.ShapeDtypeStruct((depth, d, n_out), BF16),
        compiler_params=_params("arbitrary", "arbitrary", "arbitrary"),
        name="repack_w_in",
    )(w_in, heads)


def _in_proj_kernel(x_ref, g_ref, w_ref, wff_ref, zf_ref, kvb_ref, zg_ref, ff_ref, h_ref,
                    *, bounds, scale):
    j = pl.program_id(1)
    jq, jk, jr, jg = bounds

    @pl.when(j == 0)
    def _():
        h = _rms_rows(x_ref[...], g_ref[...]).astype(BF16)
        h_ref[...] = h
        ff_ref[...] = _dot(h, wff_ref[...])

    @pl.when(j < jq)
    def _():
        zf_ref[...] = _dot(h_ref[...], w_ref[...])

    @pl.when((j >= jq) & (j < jk))
    def _():
        zf_ref[...] = _dot(h_ref[...], w_ref[...]) * scale

    @pl.when((j >= jk) & (j < jr))
    def _():
        z = _dot(h_ref[...], w_ref[...])
        zf_ref[...] = z
        kvb_ref[...] = z.astype(BF16)

    @pl.when((j >= jr) & (j < jg))
    def _():
        zf_ref[...] = _dot(h_ref[...], w_ref[...])

    @pl.when(j >= jg)
    def _():
        zg_ref[...] = _dot(h_ref[...], w_ref[...]).astype(BF16)


def _in_proj(x, norm, w_main, w_ff, layer, tm, tn, widths, hd):
    m, d = x.shape
    assert all(w % tn == 0 for w in widths) and sum(widths) == w_main.shape[2]
    cuts = np.cumsum([w // tn for w in widths])
    jq, jk, jr, jg, nj = (int(c) for c in cuts)
    kern = functools.partial(_in_proj_kernel, bounds=(jq, jk, jr, jg), scale=hd ** -0.5)
    return pl.pallas_call(
        kern,
        grid=(m // tm, nj),
        in_specs=[
            pl.BlockSpec((tm, d), lambda i, j: (i, 0)),
            pl.BlockSpec((None, 1, d), lambda i, j: (layer, 0, 0)),
            pl.BlockSpec((None, d, tn), lambda i, j: (layer, 0, j)),
            pl.BlockSpec((None, d, LANES), lambda i, j: (layer, 0, 0)),
        ],
        out_specs=[
            pl.BlockSpec((tm, tn), lambda i, j: (i, jnp.minimum(j, jg - 1))),
            pl.BlockSpec((tm, tn), lambda i, j: (i, jnp.clip(j - jk, 0, jr - jk - 1))),
            pl.BlockSpec((tm, tn), lambda i, j: (i, jnp.clip(j - jg, 0, nj - jg - 1))),
            pl.BlockSpec((tm, LANES), lambda i, j: (i, 0)),
        ],
        out_shape=[
            jax.ShapeDtypeStruct((m, jg * tn), F32),
            jax.ShapeDtypeStruct((m, widths[2]), BF16),
            jax.ShapeDtypeStruct((m, widths[4]), BF16),
            jax.ShapeDtypeStruct((m, LANES), F32),
        ],
        scratch_shapes=[pltpu.VMEM((tm, d), BF16)],
        compiler_params=_params("arbitrary", "arbitrary"),
        name="in_proj",
    )(x, norm, w_main, w_ff)


def _bias_operators(heads, width):
    hd = width // heads
    selq = np.zeros((3 * LANES, width), np.float32)
    selk = np.zeros((3 * LANES, width), np.float32)
    cq = np.zeros((1, width), np.float32)
    ck = np.zeros((1, width), np.float32)
    for h in range(heads):
        for r in range(3):
            selq[r * LANES + h, h * hd + r] = 1.0
            selk[r * LANES + h, h * hd + 3 + r] = -1.0
            cq[0, h * hd + 3 + r] = 1.0
            ck[0, h * hd + r] = 1.0
    return jnp.asarray(selq, BF16), jnp.asarray(selk, BF16), jnp.asarray(cq), jnp.asarray(ck)


def _logf_prompt_kernel(*refs, heads, hd, aliased):
    if aliased:
        refs = refs[2:]
    (ff_ref, b_ref, q_ref, k_ref, k32_ref, v32_ref, tri_ref, selq_ref, selk_ref, cq_ref, ck_ref,
     lf_ref, qa_ref, ka_ref, kp_ref, vp_ref, carry_ref) = refs
    blk = ff_ref.shape[0]
    for h in range(heads):
        kp_ref[pl.ds(h, blk, stride=heads), :] = k32_ref[:, h * hd:(h + 1) * hd]
        vp_ref[pl.ds(h, blk, stride=heads), :] = v32_ref[:, h * hd:(h + 1) * hd]

    @pl.when(pl.program_id(1) == 0)
    def _():
        carry_ref[...] = jnp.zeros(carry_ref.shape, F32)

    lf = _log_sigmoid(ff_ref[...] + b_ref[...])
    lf_ref[...] = lf
    cb = _dot(tri_ref[...], jnp.concatenate(_split3(lf), axis=0)) + carry_ref[...]
    carry_ref[...] = cb[blk - 1:blk, :]
    c3 = jnp.concatenate(_split3(cb), axis=1)
    qb = (_dot(c3, selq_ref[...]) + cq_ref[...]).astype(BF16)
    kb = (_dot(c3, selk_ref[...]) + ck_ref[...]).astype(BF16)
    q = q_ref[...].astype(BF16)
    k = k_ref[...]
    for h in range(heads):
        hs = slice(h * hd, (h + 1) * hd)
        qa_ref[:, 2 * h * hd:(2 * h + 1) * hd] = q[:, hs]
        qa_ref[:, (2 * h + 1) * hd:(2 * h + 2) * hd] = qb[:, hs]
        ka_ref[:, 2 * h * hd:(2 * h + 1) * hd] = k[:, hs]
        ka_ref[:, (2 * h + 1) * hd:(2 * h + 2) * hd] = kb[:, hs]


def _logf_prompt(ff, b_fox_pad, zf, kvb, layer, depth, batch, seq, heads, hd, qcol, prev):
    width = heads * hd
    blk = _tile(seq, 512, LANES)
    nb = seq // blk
    tri = np.tril(np.ones((blk, blk), np.float32))
    tri3 = jnp.asarray(np.concatenate([tri, tri, tri], axis=1), BF16)
    ops = _bias_operators(heads, width)
    full = lambda a: pl.BlockSpec(a.shape, lambda b, i: (0,) * a.ndim)
    rows = lambda c, w: pl.BlockSpec((blk, w), lambda b, i: (b * nb + i, c))
    cache = pl.BlockSpec((None, blk * heads, hd), lambda b, i: (layer, b * nb + i, 0))
    aliased = prev is not None
    in_specs = [
        rows(0, LANES),
        pl.BlockSpec((None, 1, LANES), lambda b, i: (layer, 0, 0)),
        rows(qcol, width),
        rows(0, width),
        rows(qcol + 1, width),
        rows(qcol + 2, width),
        full(tri3), full(ops[0]), full(ops[1]), full(ops[2]), full(ops[3]),
    ]
    args = [ff, b_fox_pad, zf, kvb, zf, zf, tri3, *ops]
    if aliased:
        in_specs = [_any_spec(), _any_spec()] + in_specs
        args = list(prev) + args
    cache_shape = jax.ShapeDtypeStruct((depth, batch * seq * heads, hd), F32)
    return pl.pallas_call(
        functools.partial(_logf_prompt_kernel, heads=heads, hd=hd, aliased=aliased),
        grid=(batch, nb),
        in_specs=in_specs,
        out_specs=[
            pl.BlockSpec((None, blk, LANES), lambda b, i: (b, i, 0)),
            rows(0, 2 * width),
            rows(0, 2 * width),
            cache,
            cache,
        ],
        out_shape=[
            jax.ShapeDtypeStruct((batch, seq, LANES), F32),
            jax.ShapeDtypeStruct((batch * seq, 2 * width), BF16),
            jax.ShapeDtypeStruct((batch * seq, 2 * width), BF16),
            cache_shape,
            cache_shape,
        ],
        scratch_shapes=[pltpu.VMEM((1, LANES), F32)],
        input_output_aliases={0: 3, 1: 4} if aliased else {},
        compiler_params=_params("arbitrary", "arbitrary"),
        name="logf_prompt",
    )(*args)


def _conv_kernel(ab_ref, ac_ref, av_ref, w_ref, prev_ref, m_ref, newc_ref, carry_ref):
    i = pl.program_id(1)
    t = ac_ref.shape[0]

    @pl.when(i == 0)
    def _():
        carry_ref[...] = prev_ref[...]

    u = ac_ref[...] * av_ref[...]
    row = lax.broadcasted_iota(jnp.int32, u.shape, 0)
    p1 = carry_ref[1:2, :]
    p2 = carry_ref[0:1, :]
    u1 = jnp.where(row == 0, p1, pltpu.roll(u, 1, 0))
    u2 = pltpu.roll(u, 2, 0)
    u2 = jnp.where(row == 0, p2, jnp.where(row == 1, p1, u2))
    w = w_ref[...]
    conv = w[0:1, :] * u2 + w[1:2, :] * u1 + w[2:3, :] * u
    m_ref[...] = (ab_ref[...] * conv).astype(m_ref.dtype)
    last2 = u[t - 2:t, :]
    carry_ref[...] = last2
    newc_ref[...] = last2


def _conv(zf, conv_w, prev, layer, row0, n_seq, seq, tile, width, out_rows, out_dtype):
    nt = seq // tile
    rb0 = row0 // tile
    rows = lambda c: pl.BlockSpec((tile, width), lambda n, i: (rb0 + n * nt + i, c))
    return pl.pallas_call(
        _conv_kernel,
        grid=(n_seq, nt),
        in_specs=[rows(0), rows(1), rows(2),
                  pl.BlockSpec((None, 3, width), lambda n, i: (layer, 0, 0)),
                  pl.BlockSpec((None, 2, width), lambda n, i: (n, 0, 0))],
        out_specs=[
            pl.BlockSpec((tile, width), lambda n, i: (n * nt + i, 0)),
            pl.BlockSpec((None, 2, width), lambda n, i: (n, 0, 0)),
        ],
        out_shape=[
            jax.ShapeDtypeStruct((out_rows, width), out_dtype),
            jax.ShapeDtypeStruct((n_seq, 2, width), F32),
        ],
        scratch_shapes=[pltpu.VMEM((2, width), F32)],
        compiler_params=_params("arbitrary", "arbitrary"),
        name="gated_conv",
    )(zf, zf, zf, conv_w, prev)


def _fox_prompt_kernel(qa_ref, ka_ref, v_ref, o_ref, m_ref, l_ref, acc_ref, *, heads, hd):
    qi = pl.program_id(1)
    ki = pl.program_id(2)
    tq = qa_ref.shape[0]
    tk = ka_ref.shape[0]

    @pl.when(ki == 0)
    def _():
        m_ref[...] = jnp.full(m_ref.shape, -jnp.inf, F32)
        l_ref[...] = jnp.zeros(l_ref.shape, F32)
        acc_ref[...] = jnp.zeros(acc_ref.shape, F32)

    def step(masked):
        if masked:
            krow = lax.broadcasted_iota(jnp.int32, (tk, tq), 0)
            qcol = lax.broadcasted_iota(jnp.int32, (tk, tq), 1)
            keep = krow <= qcol
        for h in range(heads):
            aug = slice(2 * h * hd, (2 * h + 2) * hd)
            st = _dot_nt(ka_ref[:, aug], qa_ref[:, aug])
            if masked:
                st = jnp.where(keep, st, -jnp.inf)
            m_prev = m_ref[h]
            m_new = jnp.maximum(m_prev, jnp.max(st, axis=0, keepdims=True))
            alpha = jnp.exp(m_prev - m_new)
            p = jnp.exp(st - m_new)
            l_ref[h] = alpha * l_ref[h] + jnp.sum(p, axis=0, keepdims=True)
            acc_ref[h] = alpha * acc_ref[h] + _dot_tn(v_ref[:, h * hd:(h + 1) * hd], p.astype(BF16))
            m_ref[h] = m_new

    @pl.when(ki < qi)
    def _():
        step(False)

    @pl.when(ki == qi)
    def _():
        step(True)
        for h in range(heads):
            o_ref[:, h * hd:(h + 1) * hd] = (acc_ref[h] / l_ref[h]).T.astype(o_ref.dtype)


def _fox_prompt(qaug, kaug, kvb, m_rows, batch, seq, heads, hd, tq):
    nq = seq // tq
    width = heads * hd
    kern = functools.partial(_fox_prompt_kernel, heads=heads, hd=hd)
    qmap = lambda b, qi, ki: (b * nq + qi, 0)
    kmap = lambda b, qi, ki: (b * nq + jnp.minimum(ki, qi), 0)
    vmap = lambda b, qi, ki: (b * nq + jnp.minimum(ki, qi), 1)
    return pl.pallas_call(
        kern,
        grid=(batch, nq, nq),
        in_specs=[
            pl.BlockSpec((tq, 2 * width), qmap),
            pl.BlockSpec((tq, 2 * width), kmap),
            pl.BlockSpec((tq, width), vmap),
        ],
        out_specs=pl.BlockSpec((tq, width), qmap),
        out_shape=jax.ShapeDtypeStruct((m_rows, width), BF16),
        scratch_shapes=[
            pltpu.VMEM((heads, 1, tq), F32),
            pltpu.VMEM((heads, 1, tq), F32),
            pltpu.VMEM((heads, hd, tq), F32),
        ],
        compiler_params=_params("arbitrary", "arbitrary", "arbitrary"),
        name="fox_prompt",
    )(qaug, kaug, kvb)


def _gather_rows_kernel(pt_ref, *refs):
    n_in = len(refs) - 1
    out_ref = refs[-1]
    for r in range(n_in):
        out_ref[:, r, :] = refs[r][:, 0, :]


def _gather_logf_pages(cache_logf_flat, page_table, group):
    depth, _, _, width = cache_logf_flat.shape
    nb, n_pages = page_table.shape
    ng = n_pages // group

    def in_map(r):
        return lambda b, g, pt: (0, pt[b, g * group + r], 0, 0)

    return pl.pallas_call(
        _gather_rows_kernel,
        grid_spec=pltpu.PrefetchScalarGridSpec(
            num_scalar_prefetch=1,
            grid=(nb, ng),
            in_specs=[pl.BlockSpec((depth, None, 1, width), in_map(r)) for r in range(group)],
            out_specs=pl.BlockSpec((depth, None, group, width), lambda b, g, pt: (0, b, g, 0)),
        ),
        out_shape=jax.ShapeDtypeStruct((depth, nb, n_pages, width), F32),
        compiler_params=_params("arbitrary", "arbitrary"),
        name="gather_logf",
    )(page_table, *([cache_logf_flat] * group))


def _past_cumsum_kernel(x_ref, t3_ref, s3_ref, l3_ref, e3_ref, c_ref, tot_ref):
    x3 = jnp.concatenate(_split3(x_ref[...]), axis=1)
    y = _dot(x3, t3_ref[...])
    tot = _dot(x3, s3_ref[...])
    offs = _dot(l3_ref[...], jnp.concatenate(_split3(tot), axis=0))
    c_ref[...] = y + _dot(jnp.concatenate(_split3(offs), axis=1), e3_ref[...])
    n = tot.shape[0]
    tot_ref[...] = offs[n - 1:n, :] + tot[n - 1:n, :]


def _past_cumsum(lf_pages, heads):
    depth, nb, n_pages, width = lf_pages.shape
    pos = np.arange(width) // heads
    hd = np.arange(width) % heads
    t = ((hd[:, None] == hd[None, :]) & (pos[:, None] <= pos[None, :])).astype(np.float32)
    ssum = (hd[:, None] == np.arange(LANES)[None, :]).astype(np.float32)
    lstrict = np.tril(np.ones((n_pages, n_pages), np.float32), -1)
    e = (np.arange(LANES)[:, None] == hd[None, :]).astype(np.float32)
    t3 = jnp.asarray(np.concatenate([t, t, t], axis=0), BF16)
    s3 = jnp.asarray(np.concatenate([ssum, ssum, ssum], axis=0), BF16)
    l3 = jnp.asarray(np.concatenate([lstrict, lstrict, lstrict], axis=1), BF16)
    e3 = jnp.asarray(np.concatenate([e, e, e], axis=0), BF16)
    x = lf_pages.reshape(depth * nb, n_pages, width)
    full = lambda a: pl.BlockSpec(a.shape, lambda g: (0,) * a.ndim)
    return pl.pallas_call(
        _past_cumsum_kernel,
        grid=(depth * nb,),
        in_specs=[pl.BlockSpec((None, n_pages, width), lambda g: (g, 0, 0)),
                  full(t3), full(s3), full(l3), full(e3)],
        out_specs=[pl.BlockSpec((None, n_pages, width), lambda g: (g, 0, 0)),
                   pl.BlockSpec((None, 1, LANES), lambda g: (g, 0, 0))],
        out_shape=[jax.ShapeDtypeStruct((depth * nb, n_pages, width), F32),
                   jax.ShapeDtypeStruct((depth * nb, 1, LANES), F32)],
        compiler_params=_params("arbitrary"),
        name="past_cumsum",
    )(x, t3, s3, l3, e3)


def _fox_sample_kernel(pt_ref, *refs, group, heads, hd, nt, aliased):
    if aliased:
        refs = refs[2:]
    q_ref, kn_ref, vn_ref, ff_ref, b_ref, ctot_ref, cp_ref = refs[:7]
    refs = refs[7:]
    k_refs = refs[:group]
    v_refs = refs[group:2 * group]
    o_ref, lf_ref, ks_ref, vs_ref = refs[2 * group:2 * group + 4]
    qall_ref, cqm_ref, m_ref, l_ref, acc_ref = refs[2 * group + 4:]
    g = pl.program_id(1)
    ng = pl.num_programs(1)
    nrow = heads * nt
    prow = k_refs[0].shape[0]
    assert heads & (heads - 1) == 0 and nt & (nt - 1) == 0
    hbits = heads.bit_length() - 1
    tbits = nt.bit_length() - 1

    def head_of_col(cols):
        return jnp.bitwise_and(cols, heads - 1)

    @pl.when(g == 0)
    def _():
        lf = _log_sigmoid(ff_ref[...] + b_ref[...])
        lf_ref[...] = lf
        trow = lax.broadcasted_iota(jnp.int32, lf.shape, 0)
        tlane = lax.broadcasted_iota(jnp.int32, lf.shape, 1)
        cs = lf
        sh = 1
        while sh < nt:
            cs = cs + jnp.where(trow >= sh, pltpu.roll(cs, sh, 0), 0.0)
            sh *= 2
        cn = ctot_ref[...] + cs
        cq = jnp.concatenate([cn[:, h:h + 1] for h in range(heads)], axis=0)
        prows = lax.broadcasted_iota(jnp.int32, (nrow, prow), 0)
        pcols = lax.broadcasted_iota(jnp.int32, (nrow, prow), 1)
        cqm_ref[...] = jnp.where(head_of_col(pcols) == jnp.right_shift(prows, tbits), cq, -jnp.inf)
        qf = q_ref[...]
        qall = jnp.concatenate([qf[:, h * hd:(h + 1) * hd] for h in range(heads)], axis=0)
        qall_ref[...] = qall.astype(BF16)
        cnm = jnp.where(tlane < heads, cn, 0.0)
        ck = jnp.zeros((1, LANES), F32)
        for t in range(nt):
            rolled = pltpu.roll(cnm, t * heads, 1)
            ck = ck + jnp.sum(jnp.where(trow == t, rolled, 0.0), axis=0, keepdims=True)
        for h in range(heads):
            ks_ref[pl.ds(h, nt, stride=heads), :] = kn_ref[:, h * hd:(h + 1) * hd]
            vs_ref[pl.ds(h, nt, stride=heads), :] = vn_ref[:, h * hd:(h + 1) * hd]
        kn = _pad_rows(ks_ref[...], LANES).astype(BF16)
        vn = _pad_rows(vs_ref[...], LANES).astype(BF16)
        s = _dot_nt(qall_ref[...], kn) + cq - ck
        rows = lax.broadcasted_iota(jnp.int32, s.shape, 0)
        cols = lax.broadcasted_iota(jnp.int32, s.shape, 1)
        ok = (head_of_col(cols) == jnp.right_shift(rows, tbits))
        ok = ok & (jnp.right_shift(cols, hbits) <= jnp.bitwise_and(rows, nt - 1)) & (cols < nrow)
        s = jnp.where(ok, s, -jnp.inf)
        m0 = jnp.max(s, axis=-1, keepdims=True)
        p = jnp.exp(s - m0)
        m_ref[...] = m0
        l_ref[...] = jnp.sum(p, axis=-1, keepdims=True)
        acc_ref[...] = _dot(p.astype(BF16), vn)

    qall = qall_ref[...]
    cqm = cqm_ref[...]
    ss = []
    for r in range(group):
        kb = k_refs[r][...].astype(BF16)
        ss.append(_dot_nt(qall, kb) + (cqm - cp_ref[r:r + 1, :]))
    mx = ss[0]
    for s in ss[1:]:
        mx = jnp.maximum(mx, s)
    m_prev = m_ref[...]
    m_new = jnp.maximum(m_prev, jnp.max(mx, axis=-1, keepdims=True))
    alpha = jnp.exp(m_prev - m_new)
    acc = alpha * acc_ref[...]
    psum = jnp.zeros((nrow, prow), F32)
    for r in range(group):
        p = jnp.exp(ss[r] - m_new)
        psum = psum + p
        acc = acc + _dot(p.astype(BF16), v_refs[r][...].astype(BF16))
    l_ref[...] = alpha * l_ref[...] + jnp.sum(psum, axis=-1, keepdims=True)
    acc_ref[...] = acc
    m_ref[...] = m_new

    @pl.when(g == ng - 1)
    def _():
        o = acc_ref[...] / l_ref[...]
        for h in range(heads):
            o_ref[:, h * hd:(h + 1) * hd] = o[h * nt:(h + 1) * nt, :]


def _fox_sample(zf, ff, b_fox_pad, ctot, c_past, cache_k, cache_v, page_table, layer, depth,
                row0, nt, heads, hd, group, qcol, prev):
    nb, n_pages = page_table.shape
    prow = cache_k.shape[2]
    width = heads * hd
    ng = n_pages // group
    rb0 = row0 // nt
    nrow = heads * nt
    aliased = prev is not None

    def page_map(r):
        return lambda b, g, pt: (layer, pt[b, g * group + r], 0, 0)

    kern = functools.partial(_fox_sample_kernel, group=group, heads=heads, hd=hd, nt=nt, aliased=aliased)
    new_rows = pl.BlockSpec((None, nrow, hd), lambda b, g, pt: (layer, b, 0))
    new_shape = jax.ShapeDtypeStruct((depth, nb * nrow, hd), F32)
    in_specs = ([_any_spec(), _any_spec()] if aliased else []) + [
        pl.BlockSpec((nt, width), lambda b, g, pt: (rb0 + b, qcol)),
        pl.BlockSpec((nt, width), lambda b, g, pt: (rb0 + b, qcol + 1)),
        pl.BlockSpec((nt, width), lambda b, g, pt: (rb0 + b, qcol + 2)),
        pl.BlockSpec((nt, LANES), lambda b, g, pt: (rb0 + b, 0)),
        pl.BlockSpec((None, 1, LANES), lambda b, g, pt: (layer, 0, 0)),
        pl.BlockSpec((None, 1, LANES), lambda b, g, pt: (layer * nb + b, 0, 0)),
        pl.BlockSpec((None, group, c_past.shape[2]), lambda b, g, pt: (layer * nb + b, g, 0)),
    ]
    in_specs += [pl.BlockSpec((None, None, prow, hd), page_map(r)) for r in range(group)]
    in_specs += [pl.BlockSpec((None, None, prow, hd), page_map(r)) for r in range(group)]
    return pl.pallas_call(
        kern,
        grid_spec=pltpu.PrefetchScalarGridSpec(
            num_scalar_prefetch=1,
            grid=(nb, ng),
            in_specs=in_specs,
            out_specs=[pl.BlockSpec((nt, width), lambda b, g, pt: (b, 0)),
                       pl.BlockSpec((None, nt, LANES), lambda b, g, pt: (b, 0, 0)),
                       new_rows, new_rows],
            scratch_shapes=[
                pltpu.VMEM((nrow, hd), BF16),
                pltpu.VMEM((nrow, prow), F32),
                pltpu.VMEM((nrow, 1), F32),
                pltpu.VMEM((nrow, 1), F32),
                pltpu.VMEM((nrow, hd), F32),
            ],
        ),
        out_shape=[jax.ShapeDtypeStruct((nb * nt, width), F32),
                   jax.ShapeDtypeStruct((nb, nt, LANES), F32),
                   new_shape, new_shape],
        input_output_aliases={1: 2, 2: 3} if aliased else {},
        compiler_params=_params("arbitrary", "arbitrary"),
        name="fox_sample",
    )(page_table, *(prev if aliased else ()), zf, zf, zf, ff, b_fox_pad, ctot, c_past,
      *([cache_k] * group), *([cache_v] * group))


def _hgrn_operators(c, real):
    t = np.arange(c)[:, None]
    j = np.arange(c)[None, :]
    mats = [(j <= t), (j > t)]
    masks = []
    m = 1
    while m < real:
        blk = t // (2 * m)
        mid = blk * (2 * m) + m - 1
        second = t > mid
        if m < SUBLANES:
            mats.append(np.where(second, (j > mid) & (j <= t), (j > t) & (j <= mid)))
        s = np.arange(c)[None, :]
        masks.append((blk == (s // (2 * m))) & second & (s <= (s // (2 * m)) * (2 * m) + m - 1))
        m *= 2
    a = np.concatenate(mats, axis=0).astype(np.float32)
    a3 = np.concatenate([a, a, a], axis=1)
    return jnp.asarray(a3, BF16), jnp.asarray(np.stack(masks).astype(np.float32)), len(masks)


def _hgrn_kernel(q_ref, f_ref, i_ref, g_ref, lbraw_ref, gn_ref, a3_ref, msk_ref, s0_ref,
                 o_ref, sfin_ref, st_ref, *, layer, heads, dk, n_lev, real):
    c = HGRN_CHUNK
    ci = pl.program_id(1)
    nc = pl.num_programs(1)

    @pl.when(ci == 0)
    def _():
        for h in range(heads):
            st_ref[h] = s0_ref[h].T

    raw = lbraw_ref[...]
    e = jnp.exp(raw - jnp.max(raw, axis=0, keepdims=True))
    sm = e / jnp.sum(e, axis=0, keepdims=True)
    cs = sm[0:1]
    for r in range(1, layer + 1):
        cs = cs + sm[r:r + 1]
    lb = cs - sm[0:1]
    log_lb = jnp.log(lb)
    log_1m = jnp.log1p(-lb)

    zf = _pad_rows(f_ref[...], c)
    q = _pad_rows(q_ref[...], c)
    v = _pad_rows(i_ref[...], c)
    b = log_1m + _log_sigmoid(zf)
    mx = jnp.maximum(log_lb, b)
    g = mx + jnp.log(1.0 + jnp.exp(-jnp.abs(log_lb - b)))
    kr = (1.0 - lb) * _sigmoid(-zf)
    if real < c:
        valid = lax.broadcasted_iota(jnp.int32, g.shape, 0) < real
        g = jnp.where(valid, g, 0.0)
        kr = jnp.where(valid, kr, 0.0)

    e = _dot(a3_ref[...], jnp.concatenate(_split3(g), axis=0))
    eq = e[0:c]
    xq = jnp.exp(eq)
    xk = jnp.exp(e[c:2 * c])
    xlast = xq[c - 1:c]
    n_mx = e.shape[0] // c - 2

    def level_factor(l):
        if l < n_mx:
            return jnp.exp(e[(2 + l) * c:(3 + l) * c])
        m = 1 << l
        mids = [jnp.broadcast_to(eq[b0 + m - 1:b0 + m, :], (2 * m, eq.shape[1]))
                for b0 in range(0, c, 2 * m)]
        r = mids[0] if len(mids) == 1 else jnp.concatenate(mids, axis=0)
        return jnp.exp(-jnp.abs(eq - r))

    xls = [level_factor(l) for l in range(n_lev)]
    qe = (q * xq).astype(BF16)
    ke = (kr * xk).astype(BF16)
    vb = v.astype(BF16)
    for h in range(heads):
        hs = slice(h * dk, (h + 1) * dk)
        st = st_ref[h]
        o = _dot_nt(qe[:, hs], st.astype(BF16))
        att = jnp.zeros((c, c), F32)
        for l in range(n_lev):
            xl = xls[l][:, hs]
            p = _dot_nt((q[:, hs] * xl).astype(BF16), (kr[:, hs] * xl).astype(BF16))
            att = att + p * msk_ref[l]
        o = o + _dot(att.astype(BF16), vb[:, hs])
        o = o + jnp.sum(q[:, hs] * kr[:, hs], axis=-1, keepdims=True) * v[:, hs]
        st_ref[h] = st * xlast[:, hs] + _dot_tn(vb[:, hs], ke[:, hs])
        on = _rms_rows(o, gn_ref[:, hs])
        gg = g_ref[:, hs]
        o_ref[:, hs] = (on[:real] * (gg * _sigmoid(gg))).astype(o_ref.dtype)

    @pl.when(ci == nc - 1)
    def _():
        for h in range(heads):
            sfin_ref[h] = st_ref[h].T


def _hgrn(zf, hgrn_lb, hgrn_norm, s0, layer, row0, n_seq, seq, heads, dk, col0, out_rows, out_dtype):
    c = HGRN_CHUNK
    real = min(seq, c)
    assert seq % real == 0
    nc = seq // real
    width = heads * dk
    rb0 = row0 // real
    a3, masks, n_lev = _hgrn_operators(c, real)
    depth = hgrn_lb.shape[0]
    blk = lambda col: pl.BlockSpec((real, width), lambda n, i: (rb0 + n * nc + i, col))
    kern = functools.partial(_hgrn_kernel, layer=layer, heads=heads, dk=dk, n_lev=n_lev, real=real)
    in_specs = [
        blk(col0), blk(col0 + 1), blk(col0 + 2), blk(col0 + 3),
        pl.BlockSpec((depth, width), lambda n, i: (0, 0)),
        pl.BlockSpec((None, 1, width), lambda n, i: (layer, 0, 0)),
        pl.BlockSpec(a3.shape, lambda n, i: (0, 0)),
        pl.BlockSpec(masks.shape, lambda n, i: (0, 0, 0)),
        pl.BlockSpec((None, heads, dk, dk), lambda n, i: (n, 0, 0, 0)),
    ]
    return pl.pallas_call(
        kern,
        grid=(n_seq, nc),
        in_specs=in_specs,
        out_specs=[
            pl.BlockSpec((real, width), lambda n, i: (n * nc + i, 0)),
            pl.BlockSpec((None, heads, dk, dk), lambda n, i: (n, 0, 0, 0)),
        ],
        out_shape=[
            jax.ShapeDtypeStruct((out_rows, width), out_dtype),
            jax.ShapeDtypeStruct((n_seq, heads, dk, dk), F32),
        ],
        scratch_shapes=[pltpu.VMEM((heads, dk, dk), F32)],
        compiler_params=_params("arbitrary", "arbitrary"),
        name="hgrn",
    )(zf, zf, zf, zf, hgrn_lb, hgrn_norm, a3, masks, s0)


def _patch_rows_kernel(*refs):
    n = len(refs) // 3
    for small_ref, out_ref in zip(refs[n:2 * n], refs[2 * n:]):
        out_ref[...] = small_ref[...].astype(out_ref.dtype)


def _patch_rows(bigs, smalls, row0):
    rows = smalls[0].shape[0]
    assert row0 % rows == 0
    n = len(bigs)
    return pl.pallas_call(
        _patch_rows_kernel,
        grid=(1,),
        in_specs=[_any_spec()] * n + [pl.BlockSpec(s.shape, lambda i: (0, 0)) for s in smalls],
        out_specs=[pl.BlockSpec((rows, b.shape[1]), lambda i: (row0 // rows, 0)) for b in bigs],
        out_shape=[jax.ShapeDtypeStruct(b.shape, b.dtype) for b in bigs],
        input_output_aliases={i: i for i in range(n)},
        compiler_params=_params("arbitrary"),
        name="patch_rows",
    )(*bigs, *smalls)


def _merge_kernel(ma_ref, ob_ref, oc_ref, wa_ref, wb_ref, wc_ref, ga_ref, gb_ref, gc_ref, mix_ref):
    ya = _dot(ma_ref[...], wa_ref[...])
    yb = _dot(ob_ref[...], wb_ref[...])
    yc = _dot(oc_ref[...], wc_ref[...])
    gate = lambda r: _sigmoid(r[...].astype(F32))
    mix = gate(ga_ref) * ya + gate(gb_ref) * yb + gate(gc_ref) * yc
    mix_ref[...] = mix.astype(mix_ref.dtype)


def _merge(m_a, o_b, o_c, w_a, w_b, w_c, zg, layer, tm, tn):
    m, k = m_a.shape
    d = w_a.shape[2]
    nj = d // tn
    act = pl.BlockSpec((tm, k), lambda i, j: (i, 0))
    wsp = pl.BlockSpec((None, k, tn), lambda i, j: (layer, 0, j))
    gate = lambda r: pl.BlockSpec((tm, tn), lambda i, j: (i, r * nj + j))
    return pl.pallas_call(
        _merge_kernel,
        grid=(m // tm, nj),
        in_specs=[act, act, act, wsp, wsp, wsp, gate(0), gate(1), gate(2)],
        out_specs=pl.BlockSpec((tm, tn), lambda i, j: (i, j)),
        out_shape=jax.ShapeDtypeStruct((m, d), BF16),
        compiler_params=_params("arbitrary", "arbitrary"),
        name="merge",
    )(m_a, o_b, o_c, w_a, w_b, w_c, zg, zg, zg)


def _residual_matmul_kernel(a_ref, w_ref, x_ref, o_ref):
    o_ref[...] = x_ref[...] + _dot(a_ref[...], w_ref[...])


def _weight_spec(shape, index_map, resident):
    if resident:
        return pl.BlockSpec(shape, index_map, pipeline_mode=pl.Buffered(1))
    return pl.BlockSpec(shape, index_map)


def _residual_matmul(a, w, x, layer, tm, tn):
    m, k = a.shape
    d = w.shape[2]
    return pl.pallas_call(
        _residual_matmul_kernel,
        grid=(m // tm, d // tn),
        in_specs=[
            pl.BlockSpec((tm, k), lambda i, j: (i, 0)),
            _weight_spec((None, k, tn), lambda i, j: (layer, 0, j), tn == d),
            pl.BlockSpec((tm, tn), lambda i, j: (i, j)),
        ],
        out_specs=pl.BlockSpec((tm, tn), lambda i, j: (i, j)),
        out_shape=jax.ShapeDtypeStruct((m, d), F32),
        compiler_params=_params("arbitrary", "arbitrary"),
        name="residual_matmul",
    )(a, w, x)


def _swiglu_up_kernel(x_ref, g_ref, wg_ref, wu_ref, a_ref, h_ref):
    @pl.when(pl.program_id(1) == 0)
    def _():
        h_ref[...] = _rms_rows(x_ref[...], g_ref[...]).astype(BF16)

    h = h_ref[...]
    gate = _dot(h, wg_ref[...])
    up = _dot(h, wu_ref[...])
    a_ref[...] = (gate * _sigmoid(gate) * up).astype(a_ref.dtype)


def _swiglu_up(x, norm, w_gate_up, layer, tm, tn):
    m, d = x.shape
    f = w_gate_up.shape[2] // 2
    nj = f // tn
    return pl.pallas_call(
        _swiglu_up_kernel,
        grid=(m // tm, nj),
        in_specs=[
            pl.BlockSpec((tm, d), lambda i, j: (i, 0)),
            pl.BlockSpec((None, 1, d), lambda i, j: (layer, 0, 0)),
            pl.BlockSpec((None, d, tn), lambda i, j: (layer, 0, j)),
            pl.BlockSpec((None, d, tn), lambda i, j: (layer, 0, nj + j)),
        ],
        out_specs=pl.BlockSpec((tm, tn), lambda i, j: (i, j)),
        out_shape=jax.ShapeDtypeStruct((m, f), BF16),
        scratch_shapes=[pltpu.VMEM((tm, d), BF16)],
        compiler_params=_params("arbitrary", "arbitrary"),
        name="swiglu_up",
    )(x, norm, w_gate_up, w_gate_up)


def _ple_kernel(x_ref, g_ref, wg_ref, p_ref, wp_ref, o_ref, h_ref, *, tn):
    j = pl.program_id(1)

    @pl.when(j == 0)
    def _():
        h_ref[...] = _rms_rows(x_ref[...], g_ref[...]).astype(BF16)

    gate = _sigmoid(_dot(h_ref[...], wg_ref[...]))
    emb = _dot(p_ref[...].astype(BF16), wp_ref[...])
    xs = x_ref[:, pl.ds(pl.multiple_of(j * tn, tn), tn)]
    o_ref[...] = xs + gate * emb


def _ple(x, norm, w_gate, p, w_proj, layer, tm, tn):
    m, d = x.shape
    pd = p.shape[2]
    return pl.pallas_call(
        functools.partial(_ple_kernel, tn=tn),
        grid=(m // tm, d // tn),
        in_specs=[
            pl.BlockSpec((tm, d), lambda i, j: (i, 0)),
            pl.BlockSpec((None, 1, d), lambda i, j: (layer, 0, 0)),
            _weight_spec((None, d, tn), lambda i, j: (layer, 0, j), tn == d),
            pl.BlockSpec((None, tm, pd), lambda i, j: (layer, i, 0)),
            _weight_spec((None, pd, tn), lambda i, j: (layer, 0, j), tn == d),
        ],
        out_specs=pl.BlockSpec((tm, tn), lambda i, j: (i, j)),
        out_shape=jax.ShapeDtypeStruct((m, d), F32),
        scratch_shapes=[pltpu.VMEM((tm, d), BF16)],
        compiler_params=_params("arbitrary", "arbitrary"),
        name="ple",
    )(x, norm, w_gate, p, w_proj)


def _final_norm_kernel(x_ref, g_ref, o_ref):
    o_ref[...] = _rms_rows(x_ref[...], g_ref[...])


def _final_norm(x, g, tm):
    m, d = x.shape
    return pl.pallas_call(
        _final_norm_kernel,
        grid=(m // tm,),
        in_specs=[pl.BlockSpec((tm, d), lambda i: (i, 0)), pl.BlockSpec((1, d), lambda i: (0, 0))],
        out_specs=pl.BlockSpec((tm, d), lambda i: (i, 0)),
        out_shape=jax.ShapeDtypeStruct((m, d), F32),
        compiler_params=_params("arbitrary"),
        name="final_norm",
    )(x, g)


def kernel(x_prompt, x_sample, cache_k, cache_v, cache_logf, state_conv, state_hgrn, page_table,
           p_prompt, p_sample, norm1, w_in, b_fox, conv_w, hgrn_lb, hgrn_norm, w_a, w_b, w_c, w_o,
           norm2, w_gate_up, w_down, ple_norm, ple_gate, ple_proj, final_norm):
    bp, seq, d = x_prompt.shape
    db, dseq, _ = x_sample.shape
    depth = w_in.shape[0]
    heads, hd = cache_k.shape[3], cache_k.shape[4]
    page = cache_k.shape[2]
    n_pool = cache_k.shape[1]
    cw = w_a.shape[1]
    fw = w_b.shape[1]
    rw = w_c.shape[1]
    rheads = state_hgrn.shape[2]
    dk = state_hgrn.shape[3]
    assert fw == heads * hd and rw == rheads * dk and state_hgrn.shape[4] == dk and cw == fw == rw
    mp = bp * seq
    ms = db * dseq
    m = mp + ms

    o_ff = 3 * cw + 3 * fw
    widths = (3 * cw, fw, 2 * fw, 4 * rw, 3 * d)
    w_main = _repack_w_in(w_in, o_ff, heads, fw)
    w_ff = jnp.pad(w_in[:, :, o_ff:o_ff + heads], ((0, 0), (0, 0), (0, LANES - heads))).astype(BF16)
    assert w_main.shape[2] == sum(widths)
    tn_in = fw
    qcol = 3 * cw // tn_in
    rcol = qcol + 3

    bf = lambda a: a.astype(BF16)
    w_a_b, w_b_b, w_c_b, w_o_b = bf(w_a), bf(w_b), bf(w_c), bf(w_o)
    w_gu_b, w_dn_b, w_pg_b, w_pp_b = bf(w_gate_up), bf(w_down), bf(ple_gate), bf(ple_proj)
    row3 = lambda a: a.reshape(a.shape[0], 1, a.shape[1])
    norm1_r, norm2_r, ple_norm_r, hnorm_r = row3(norm1), row3(norm2), row3(ple_norm), row3(hgrn_norm)
    b_fox_pad = row3(jnp.pad(b_fox, ((0, 0), (0, LANES - heads))))

    x = _stack_rows(x_prompt.reshape(1, mp, d), x_sample.reshape(1, ms, d)).reshape(m, d)
    p_all = _stack_rows(p_prompt.reshape(depth, mp, -1), p_sample.reshape(depth, ms, -1))
    ck = cache_k.reshape(depth, n_pool, page * heads, hd)
    cv = cache_v.reshape(depth, n_pool, page * heads, hd)

    n_pages = page_table.shape[1]
    group = _tile(n_pages, 16, 1)
    lf_flat = cache_logf.reshape(depth, n_pool, 1, page * heads)
    lf_pages = _gather_logf_pages(lf_flat, page_table, group)
    c_past, c_tot = _past_cumsum(lf_pages, heads)

    tm = _tile(m, 704, 16)
    tm_wide = _tile(m, 1408, 16)
    tq =_tile(seq, 512, LANES)
    conv_tile = _tile(seq, 512, 8)
    zeros_conv = jnp.zeros((bp, 2, cw), F32)
    zeros_state = jnp.zeros((bp, rheads, dk, dk), F32)

    outs = {k: [] for k in ("lfp", "cp", "sp", "lfs", "cs", "ss")}
    kvp = None
    kvs = None
    for i in range(depth):
        zf, kvb, zg, ff = _in_proj(x, norm1_r, w_main, w_ff, i, tm, tn_in, widths, hd)

        m_a, conv_p = _conv(zf, conv_w, zeros_conv, i, 0, bp, seq, conv_tile, cw, m, BF16)
        ma_s, conv_s = _conv(zf, conv_w, state_conv[i], i, mp, db, dseq, dseq, cw, ms, F32)

        lf_p, qaug, kaug, kp, vp = _logf_prompt(ff, b_fox_pad, zf, kvb, i, depth, bp, seq, heads, hd,
                                                qcol, kvp)
        kvp = (kp, vp)
        o_b = _fox_prompt(qaug, kaug, kvb, m, bp, seq, heads, hd, tq)
        ob_s, lf_s, ks, vs = _fox_sample(zf, ff, b_fox_pad, c_tot, c_past, ck, cv, page_table, i,
                                         depth, mp, dseq, heads, hd, group, qcol, kvs)
        kvs = (ks, vs)

        o_c, st_p = _hgrn(zf, hgrn_lb, hnorm_r, zeros_state, i, 0, bp, seq, rheads, dk, rcol, m, BF16)
        oc_s, st_s = _hgrn(zf, hgrn_lb, hnorm_r, state_hgrn[i], i, mp, db, dseq, rheads, dk, rcol,
                           ms, F32)

        m_a, o_b, o_c = _patch_rows((m_a, o_b, o_c), (ma_s, ob_s, oc_s), mp)
        mix = _merge(m_a, o_b, o_c, w_a_b, w_b_b, w_c_b, zg, i, tm, 1024)
        x = _residual_matmul(mix, w_o_b, x, i, tm, d)

        act = _swiglu_up(x, norm2_r, w_gu_b, i, tm_wide, _tile(w_gate_up.shape[2] // 2, 512, 2 * LANES))
        x = _residual_matmul(act, w_dn_b, x, i, tm, 512)
        x = _ple(x, ple_norm_r, w_pg_b, p_all, w_pp_b, i, tm, d)

        outs["lfp"].append(lf_p[:, :, :heads])
        outs["lfs"].append(lf_s[:, :, :heads])
        outs["cp"].append(conv_p)
        outs["cs"].append(conv_s)
        outs["sp"].append(st_p)
        outs["ss"].append(st_s)

    y = _final_norm(x, final_norm.reshape(1, d), tm)
    st = lambda k: jnp.stack(outs[k])
    return (y[:mp].reshape(bp, seq, d), y[mp:].reshape(db, dseq, d),
            kp.reshape(depth, bp, seq, heads, hd), vp.reshape(depth, bp, seq, heads, hd),
            st("lfp"), st("cp"), st("sp"),
            ks.reshape(depth, db, dseq, heads, hd), vs.reshape(depth, db, dseq, heads, hd),
            st("lfs"), st("cs"), st("ss"))
```

```python
import functools

import numpy as np
import jax
import jax.numpy as jnp
from jax import lax
from jax.experimental import pallas as pl
from jax.experimental.pallas import tpu as pltpu

F32 = jnp.float32
BF16 = jnp.bfloat16
RMS_EPS = 1e-6
LANES = 128
SUBLANES = 8
HGRN_CHUNK = 128
VMEM_LIMIT = 56 * 1024 * 1024


def _tile(n, target, align):
    best = None
    for t in range(align, min(n, target) + 1, align):
        if n % t == 0:
            best = t
    assert best is not None, (n, target, align)
    return best


def _params(*sem):
    return pltpu.CompilerParams(dimension_semantics=sem, vmem_limit_bytes=VMEM_LIMIT)


def _sigmoid(x):
    return 1.0 / (1.0 + jnp.exp(-x))


def _log_sigmoid(x):
    return jnp.minimum(x, 0.0) - jnp.log(1.0 + jnp.exp(-jnp.abs(x)))


def _rms_rows(x, g):
    ms = jnp.mean(x * x, axis=-1, keepdims=True)
    return x * lax.rsqrt(ms + RMS_EPS) * g


def _split3(x):
    hi = x.astype(BF16)
    r1 = x - hi.astype(F32)
    mid = r1.astype(BF16)
    lo = (r1 - mid.astype(F32)).astype(BF16)
    return hi, mid, lo


def _dot(a, b):
    return jnp.dot(a, b, preferred_element_type=F32)


def _dot_nt(a, b):
    return lax.dot_general(a, b, (((1,), (1,)), ((), ())), preferred_element_type=F32)


def _dot_tn(a, b):
    return lax.dot_general(a, b, (((0,), (0,)), ((), ())), preferred_element_type=F32)


def _pad_rows(a, n):
    if a.shape[0] == n:
        return a
    return jnp.concatenate([a, jnp.zeros((n - a.shape[0], a.shape[1]), a.dtype)], axis=0)


def _any_spec():
    return pl.BlockSpec(memory_space=pl.ANY)


def _copy_kernel(*refs, aliased):
    a_ref, o_ref = refs[1:] if aliased else refs
    o_ref[...] = a_ref[...]


def _stack_rows(a, b):
    nl, ra, c = a.shape
    rb = b.shape[1]
    assert ra % rb == 0
    ta = _tile(ra, 1024, 8)
    shape = jax.ShapeDtypeStruct((nl, ra + rb, c), a.dtype)
    out = pl.pallas_call(
        functools.partial(_copy_kernel, aliased=False),
        grid=(nl, ra // ta),
        in_specs=[pl.BlockSpec((None, ta, c), lambda l, i: (l, i, 0))],
        out_specs=pl.BlockSpec((None, ta, c), lambda l, i: (l, i, 0)),
        out_shape=shape,
        compiler_params=_params("arbitrary", "arbitrary"),
        name="stack_rows_a",
    )(a)
    return pl.pallas_call(
        functools.partial(_copy_kernel, aliased=True),
        grid=(nl,),
        in_specs=[_any_spec(), pl.BlockSpec((None, rb, c), lambda l: (l, 0, 0))],
        out_specs=pl.BlockSpec((None, rb, c), lambda l: (l, ra // rb, 0)),
        out_shape=shape,
        input_output_aliases={0: 0},
        compiler_params=_params("arbitrary"),
        name="stack_rows_b",
    )(out, b)


def _repack_kernel(a_ref, s_ref, o_ref, *, j_shift, shift):
    j = pl.program_id(2)
    tn = a_ref.shape[1]

    @pl.when(j < j_shift)
    def _():
        o_ref[...] = a_ref[...].astype(o_ref.dtype)

    @pl.when(j >= j_shift)
    def _():
        a = pltpu.roll(a_ref[...].astype(F32), tn - shift, 1)
        nxt = pltpu.roll(s_ref[...].astype(F32), LANES - shift, 1)
        lane = lax.broadcasted_iota(jnp.int32, nxt.shape, 1)
        o_ref[:, :tn - LANES] = a[:, :tn - LANES].astype(o_ref.dtype)
        o_ref[:, tn - LANES:] = jnp.where(lane < LANES - shift, a[:, tn - LANES:], nxt).astype(o_ref.dtype)


def _repack_w_in(w_in, o_ff, shift, tn):
    depth, d, n = w_in.shape
    n_out = n - shift
    assert o_ff % tn == 0 and n_out % tn == 0 and shift < LANES
    j_shift = o_ff // tn
    nj = n_out // tn
    wb = jnp.pad(w_in.astype(BF16), ((0, 0), (0, 0), (0, n_out + LANES - n)))
    rows = _tile(d, 512, 16)
    per = tn // LANES
    return pl.pallas_call(
        functools.partial(_repack_kernel, j_shift=j_shift, shift=shift),
        grid=(depth, d // rows, nj),
        in_specs=[
            pl.BlockSpec((None, rows, tn), lambda l, r, j: (l, r, j)),
            pl.BlockSpec((None, rows, LANES), lambda l, r, j: (l, r, (j + 1) * per)),
        ],
        out_specs=pl.BlockSpec((None, rows, tn), lambda l, r, j: (l, r, j)),
        out_shape=jax.ShapeDtypeStruct((depth, d, n_out), BF16),
        compiler_params=_params("arbitrary", "arbitrary", "arbitrary"),
        name="repack_w_in",
    )(wb, wb)


def _in_proj_kernel(x_ref, g_ref, w_ref, wff_ref, zf_ref, kvb_ref, zg_ref, ff_ref, h_ref,
                    *, bounds, scale):
    j = pl.program_id(1)
    jq, jk, jr, jg = bounds

    @pl.when(j == 0)
    def _():
        h = _rms_rows(x_ref[...], g_ref[...]).astype(BF16)
        h_ref[...] = h
        ff_ref[...] = _dot(h, wff_ref[...])

    @pl.when(j < jq)
    def _():
        zf_ref[...] = _dot(h_ref[...], w_ref[...])

    @pl.when((j >= jq) & (j < jk))
    def _():
        zf_ref[...] = _dot(h_ref[...], w_ref[...]) * scale

    @pl.when((j >= jk) & (j < jr))
    def _():
        z = _dot(h_ref[...], w_ref[...])
        zf_ref[...] = z
        kvb_ref[...] = z.astype(BF16)

    @pl.when((j >= jr) & (j < jg))
    def _():
        zf_ref[...] = _dot(h_ref[...], w_ref[...])

    @pl.when(j >= jg)
    def _():
        zg_ref[...] = _dot(h_ref[...], w_ref[...]).astype(BF16)


def _in_proj(x, norm, w_main, w_ff, layer, tm, tn, widths, hd):
    m, d = x.shape
    assert all(w % tn == 0 for w in widths) and sum(widths) == w_main.shape[2]
    cuts = np.cumsum([w // tn for w in widths])
    jq, jk, jr, jg, nj = (int(c) for c in cuts)
    kern = functools.partial(_in_proj_kernel, bounds=(jq, jk, jr, jg), scale=hd ** -0.5)
    return pl.pallas_call(
        kern,
        grid=(m // tm, nj),
        in_specs=[
            pl.BlockSpec((tm, d), lambda i, j: (i, 0)),
            pl.BlockSpec((None, 1, d), lambda i, j: (layer, 0, 0)),
            pl.BlockSpec((None, d, tn), lambda i, j: (layer, 0, j)),
            pl.BlockSpec((None, d, LANES), lambda i, j: (layer, 0, 0)),
        ],
        out_specs=[
            pl.BlockSpec((tm, tn), lambda i, j: (i, jnp.minimum(j, jg - 1))),
            pl.BlockSpec((tm, tn), lambda i, j: (i, jnp.clip(j - jk, 0, jr - jk - 1))),
            pl.BlockSpec((tm, tn), lambda i, j: (i, jnp.clip(j - jg, 0, nj - jg - 1))),
            pl.BlockSpec((tm, LANES), lambda i, j: (i, 0)),
        ],
        out_shape=[
            jax.ShapeDtypeStruct((m, jg * tn), F32),
            jax.ShapeDtypeStruct((m, widths[2]), BF16),
            jax.ShapeDtypeStruct((m, widths[4]), BF16),
            jax.ShapeDtypeStruct((m, LANES), F32),
        ],
        scratch_shapes=[pltpu.VMEM((tm, d), BF16)],
        compiler_params=_params("arbitrary", "arbitrary"),
        name="in_proj",
    )(x, norm, w_main, w_ff)


def _bias_operators(heads, width):
    hd = width // heads
    selq = np.zeros((3 * LANES, width), np.float32)
    selk = np.zeros((3 * LANES, width), np.float32)
    cq = np.zeros((1, width), np.float32)
    ck = np.zeros((1, width), np.float32)
    for h in range(heads):
        for r in range(3):
            selq[r * LANES + h, h * hd + r] = 1.0
            selk[r * LANES + h, h * hd + 3 + r] = -1.0
            cq[0, h * hd + 3 + r] = 1.0
            ck[0, h * hd + r] = 1.0
    return jnp.asarray(selq, BF16), jnp.asarray(selk, BF16), jnp.asarray(cq), jnp.asarray(ck)


def _logf_prompt_kernel(*refs, heads, hd, aliased):
    if aliased:
        refs = refs[2:]
    (ff_ref, b_ref, q_ref, k_ref, k32_ref, v32_ref, tri_ref, selq_ref, selk_ref, cq_ref, ck_ref,
     lf_ref, qat_ref, ka_ref, vt_ref, kp_ref, vp_ref, carry_ref) = refs
    blk = ff_ref.shape[0]
    for h in range(heads):
        kp_ref[pl.ds(h, blk, stride=heads), :] = k32_ref[:, h * hd:(h + 1) * hd]
        vp_ref[pl.ds(h, blk, stride=heads), :] = v32_ref[:, h * hd:(h + 1) * hd]

    @pl.when(pl.program_id(1) == 0)
    def _():
        carry_ref[...] = jnp.zeros(carry_ref.shape, F32)

    lf = _log_sigmoid(ff_ref[...] + b_ref[...])
    lf_ref[...] = lf
    cb = _dot(tri_ref[...], jnp.concatenate(_split3(lf), axis=0)) + carry_ref[...]
    carry_ref[...] = cb[blk - 1:blk, :]
    c3 = jnp.concatenate(_split3(cb), axis=1)
    qb = _dot(c3, selq_ref[...]) + cq_ref[...]
    kb = (_dot(c3, selk_ref[...]) + ck_ref[...]).astype(BF16)
    k = k_ref[...]
    for h in range(heads):
        hs = slice(h * hd, (h + 1) * hd)
        qat_ref[2 * h * hd:(2 * h + 1) * hd, :] = q_ref[:, hs].T.astype(BF16)
        qat_ref[(2 * h + 1) * hd:(2 * h + 2) * hd, :] = qb[:, hs].T.astype(BF16)
        vt_ref[hs, :] = v32_ref[:, hs].T.astype(BF16)
        ka_ref[:, 2 * h * hd:(2 * h + 1) * hd] = k[:, hs]
        ka_ref[:, (2 * h + 1) * hd:(2 * h + 2) * hd] = kb[:, hs]


def _logf_prompt(ff, b_fox_pad, zf, kvb, layer, depth, batch, seq, heads, hd, qcol, prev):
    width = heads * hd
    blk = _tile(seq, 512, LANES)
    nb = seq // blk
    tri = np.tril(np.ones((blk, blk), np.float32))
    tri3 = jnp.asarray(np.concatenate([tri, tri, tri], axis=1), BF16)
    ops = _bias_operators(heads, width)
    full = lambda a: pl.BlockSpec(a.shape, lambda b, i: (0,) * a.ndim)
    rows = lambda c, w: pl.BlockSpec((blk, w), lambda b, i: (b * nb + i, c))
    cache = pl.BlockSpec((None, blk * heads, hd), lambda b, i: (layer, b * nb + i, 0))
    aliased = prev is not None
    in_specs = [
        rows(0, LANES),
        pl.BlockSpec((None, 1, LANES), lambda b, i: (layer, 0, 0)),
        rows(qcol, width),
        rows(0, width),
        rows(qcol + 1, width),
        rows(qcol + 2, width),
        full(tri3), full(ops[0]), full(ops[1]), full(ops[2]), full(ops[3]),
    ]
    args = [ff, b_fox_pad, zf, kvb, zf, zf, tri3, *ops]
    if aliased:
        in_specs = [_any_spec(), _any_spec()] + in_specs
        args = list(prev) + args
    cache_shape = jax.ShapeDtypeStruct((depth, batch * seq * heads, hd), F32)
    return pl.pallas_call(
        functools.partial(_logf_prompt_kernel, heads=heads, hd=hd, aliased=aliased),
        grid=(batch, nb),
        in_specs=in_specs,
        out_specs=[
            pl.BlockSpec((None, blk, LANES), lambda b, i: (b, i, 0)),
            pl.BlockSpec((2 * width, blk), lambda b, i: (0, b * nb + i)),
            rows(0, 2 * width),
            pl.BlockSpec((width, blk), lambda b, i: (0, b * nb + i)),
            cache,
            cache,
        ],
        out_shape=[
            jax.ShapeDtypeStruct((batch, seq, LANES), F32),
            jax.ShapeDtypeStruct((2 * width, batch * seq), BF16),
            jax.ShapeDtypeStruct((batch * seq, 2 * width), BF16),
            jax.ShapeDtypeStruct((width, batch * seq), BF16),
            cache_shape,
            cache_shape,
        ],
        scratch_shapes=[pltpu.VMEM((1, LANES), F32)],
        input_output_aliases={0: 4, 1: 5} if aliased else {},
        compiler_params=_params("arbitrary", "arbitrary"),
        name="logf_prompt",
    )(*args)


def _conv_kernel(ab_ref, ac_ref, av_ref, w_ref, prev_ref, m_ref, newc_ref, carry_ref):
    i = pl.program_id(1)
    t = ac_ref.shape[0]

    @pl.when(i == 0)
    def _():
        carry_ref[...] = prev_ref[...]

    u = ac_ref[...] * av_ref[...]
    row = lax.broadcasted_iota(jnp.int32, u.shape, 0)
    p1 = carry_ref[1:2, :]
    p2 = carry_ref[0:1, :]
    u1 = jnp.where(row == 0, p1, pltpu.roll(u, 1, 0))
    u2 = pltpu.roll(u, 2, 0)
    u2 = jnp.where(row == 0, p2, jnp.where(row == 1, p1, u2))
    w = w_ref[...]
    conv = w[0:1, :] * u2 + w[1:2, :] * u1 + w[2:3, :] * u
    m_ref[...] = (ab_ref[...] * conv).astype(m_ref.dtype)
    last2 = u[t - 2:t, :]
    carry_ref[...] = last2
    newc_ref[...] = last2


def _conv(zf, conv_w, prev, layer, row0, n_seq, seq, tile, width, out_rows, out_dtype):
    nt = seq // tile
    rb0 = row0 // tile
    rows = lambda c: pl.BlockSpec((tile, width), lambda n, i: (rb0 + n * nt + i, c))
    return pl.pallas_call(
        _conv_kernel,
        grid=(n_seq, nt),
        in_specs=[rows(0), rows(1), rows(2),
                  pl.BlockSpec((None, 3, width), lambda n, i: (layer, 0, 0)),
                  pl.BlockSpec((None, 2, width), lambda n, i: (n, 0, 0))],
        out_specs=[
            pl.BlockSpec((tile, width), lambda n, i: (n * nt + i, 0)),
            pl.BlockSpec((None, 2, width), lambda n, i: (n, 0, 0)),
        ],
        out_shape=[
            jax.ShapeDtypeStruct((out_rows, width), out_dtype),
            jax.ShapeDtypeStruct((n_seq, 2, width), F32),
        ],
        scratch_shapes=[pltpu.VMEM((2, width), F32)],
        compiler_params=_params("arbitrary", "arbitrary"),
        name="gated_conv",
    )(zf, zf, zf, conv_w, prev)


def _fox_prompt_kernel(qat_ref, ka_ref, vt_ref, o_ref, m_ref, l_ref, acc_ref, *, heads, hd):
    qi = pl.program_id(1)
    ki = pl.program_id(2)
    tq = qat_ref.shape[1]
    tk = ka_ref.shape[0]

    @pl.when(ki == 0)
    def _():
        m_ref[...] = jnp.full(m_ref.shape, -jnp.inf, F32)
        l_ref[...] = jnp.zeros(l_ref.shape, F32)
        acc_ref[...] = jnp.zeros(acc_ref.shape, F32)

    def step(masked):
        if masked:
            krow = lax.broadcasted_iota(jnp.int32, (tk, tq), 0)
            qcol = lax.broadcasted_iota(jnp.int32, (tk, tq), 1)
            keep = krow <= qcol
        for h in range(heads):
            aug = slice(2 * h * hd, (2 * h + 2) * hd)
            st = _dot(ka_ref[:, aug], qat_ref[aug, :])
            if masked:
                st = jnp.where(keep, st, -jnp.inf)
            m_prev = m_ref[h]
            m_new = jnp.maximum(m_prev, jnp.max(st, axis=0, keepdims=True))
            alpha = jnp.exp(m_prev - m_new)
            p = jnp.exp(st - m_new)
            l_ref[h] = alpha * l_ref[h] + jnp.sum(p, axis=0, keepdims=True)
            acc_ref[h] = alpha * acc_ref[h] + _dot(vt_ref[h * hd:(h + 1) * hd, :], p.astype(BF16))
            m_ref[h] = m_new

    @pl.when(ki < qi)
    def _():
        step(False)

    @pl.when(ki == qi)
    def _():
        step(True)
        for h in range(heads):
            o_ref[:, h * hd:(h + 1) * hd] = (acc_ref[h] / l_ref[h]).T.astype(o_ref.dtype)


def _fox_prompt(qaug_t, kaug, v_t, m_rows, batch, seq, heads, hd, tq):
    nq = seq // tq
    width = heads * hd
    kern = functools.partial(_fox_prompt_kernel, heads=heads, hd=hd)
    qmap = lambda b, qi, ki: (b * nq + qi, 0)
    kmap = lambda b, qi, ki: (b * nq + jnp.minimum(ki, qi), 0)
    return pl.pallas_call(
        kern,
        grid=(batch, nq, nq),
        in_specs=[
            pl.BlockSpec((2 * width, tq), lambda b, qi, ki: (0, b * nq + qi)),
            pl.BlockSpec((tq, 2 * width), kmap),
            pl.BlockSpec((width, tq), lambda b, qi, ki: (0, b * nq + jnp.minimum(ki, qi))),
        ],
        out_specs=pl.BlockSpec((tq, width), qmap),
        out_shape=jax.ShapeDtypeStruct((m_rows, width), BF16),
        scratch_shapes=[
            pltpu.VMEM((heads, 1, tq), F32),
            pltpu.VMEM((heads, 1, tq), F32),
            pltpu.VMEM((heads, hd, tq), F32),
        ],
        compiler_params=_params("arbitrary", "arbitrary", "arbitrary"),
        name="fox_prompt",
    )(qaug_t, kaug, v_t)


def _gather_rows_kernel(pt_ref, *refs):
    n_in = len(refs) - 1
    out_ref = refs[-1]
    for r in range(n_in):
        out_ref[:, r, :] = refs[r][:, 0, :]


def _gather_logf_pages(cache_logf_flat, page_table, group):
    depth, _, _, width = cache_logf_flat.shape
    nb, n_pages = page_table.shape
    ng = n_pages // group

    def in_map(r):
        return lambda b, g, pt: (0, pt[b, g * group + r], 0, 0)

    return pl.pallas_call(
        _gather_rows_kernel,
        grid_spec=pltpu.PrefetchScalarGridSpec(
            num_scalar_prefetch=1,
            grid=(nb, ng),
            in_specs=[pl.BlockSpec((depth, None, 1, width), in_map(r)) for r in range(group)],
            out_specs=pl.BlockSpec((depth, None, group, width), lambda b, g, pt: (0, b, g, 0)),
        ),
        out_shape=jax.ShapeDtypeStruct((depth, nb, n_pages, width), F32),
        compiler_params=_params("arbitrary", "arbitrary"),
        name="gather_logf",
    )(page_table, *([cache_logf_flat] * group))


def _past_cumsum_kernel(x_ref, t3_ref, s3_ref, l3_ref, e3_ref, c_ref, tot_ref):
    x3 = jnp.concatenate(_split3(x_ref[...]), axis=1)
    y = _dot(x3, t3_ref[...])
    tot = _dot(x3, s3_ref[...])
    offs = _dot(l3_ref[...], jnp.concatenate(_split3(tot), axis=0))
    c_ref[...] = y + _dot(jnp.concatenate(_split3(offs), axis=1), e3_ref[...])
    n = tot.shape[0]
    tot_ref[...] = offs[n - 1:n, :] + tot[n - 1:n, :]


def _past_cumsum(lf_pages, heads):
    depth, nb, n_pages, width = lf_pages.shape
    pos = np.arange(width) // heads
    hd = np.arange(width) % heads
    t = ((hd[:, None] == hd[None, :]) & (pos[:, None] <= pos[None, :])).astype(np.float32)
    ssum = (hd[:, None] == np.arange(LANES)[None, :]).astype(np.float32)
    lstrict = np.tril(np.ones((n_pages, n_pages), np.float32), -1)
    e = (np.arange(LANES)[:, None] == hd[None, :]).astype(np.float32)
    t3 = jnp.asarray(np.concatenate([t, t, t], axis=0), BF16)
    s3 = jnp.asarray(np.concatenate([ssum, ssum, ssum], axis=0), BF16)
    l3 = jnp.asarray(np.concatenate([lstrict, lstrict, lstrict], axis=1), BF16)
    e3 = jnp.asarray(np.concatenate([e, e, e], axis=0), BF16)
    x = lf_pages.reshape(depth * nb, n_pages, width)
    full = lambda a: pl.BlockSpec(a.shape, lambda g: (0,) * a.ndim)
    return pl.pallas_call(
        _past_cumsum_kernel,
        grid=(depth * nb,),
        in_specs=[pl.BlockSpec((None, n_pages, width), lambda g: (g, 0, 0)),
                  full(t3), full(s3), full(l3), full(e3)],
        out_specs=[pl.BlockSpec((None, n_pages, width), lambda g: (g, 0, 0)),
                   pl.BlockSpec((None, 1, LANES), lambda g: (g, 0, 0))],
        out_shape=[jax.ShapeDtypeStruct((depth * nb, n_pages, width), F32),
                   jax.ShapeDtypeStruct((depth * nb, 1, LANES), F32)],
        compiler_params=_params("arbitrary"),
        name="past_cumsum",
    )(x, t3, s3, l3, e3)


def _fox_sample_kernel(pt_ref, *refs, group, heads, hd, nt, aliased):
    if aliased:
        refs = refs[2:]
    q_ref, kn_ref, vn_ref, ff_ref, b_ref, ctot_ref, cp_ref = refs[:7]
    refs = refs[7:]
    k_refs = refs[:group]
    v_refs = refs[group:2 * group]
    o_ref, lf_ref, ks_ref, vs_ref = refs[2 * group:2 * group + 4]
    qall_ref, cqm_ref, m_ref, l_ref, acc_ref = refs[2 * group + 4:]
    g = pl.program_id(1)
    ng = pl.num_programs(1)
    nrow = heads * nt
    prow = k_refs[0].shape[0]
    assert heads & (heads - 1) == 0 and nt & (nt - 1) == 0
    hbits = heads.bit_length() - 1
    tbits = nt.bit_length() - 1

    def head_of_col(cols):
        return jnp.bitwise_and(cols, heads - 1)

    @pl.when(g == 0)
    def _():
        lf = _log_sigmoid(ff_ref[...] + b_ref[...])
        lf_ref[...] = lf
        trow = lax.broadcasted_iota(jnp.int32, lf.shape, 0)
        tlane = lax.broadcasted_iota(jnp.int32, lf.shape, 1)
        cs = lf
        sh = 1
        while sh < nt:
            cs = cs + jnp.where(trow >= sh, pltpu.roll(cs, sh, 0), 0.0)
            sh *= 2
        cn = ctot_ref[...] + cs
        cq = jnp.concatenate([cn[:, h:h + 1] for h in range(heads)], axis=0)
        prows = lax.broadcasted_iota(jnp.int32, (nrow, prow), 0)
        pcols = lax.broadcasted_iota(jnp.int32, (nrow, prow), 1)
        cqm_ref[...] = jnp.where(head_of_col(pcols) == jnp.right_shift(prows, tbits), cq, -jnp.inf)
        qf = q_ref[...]
        qall = jnp.concatenate([qf[:, h * hd:(h + 1) * hd] for h in range(heads)], axis=0)
        qall_ref[...] = qall.astype(BF16)
        cnm = jnp.where(tlane < heads, cn, 0.0)
        ck = jnp.zeros((1, LANES), F32)
        for t in range(nt):
            rolled = pltpu.roll(cnm, t * heads, 1)
            ck = ck + jnp.sum(jnp.where(trow == t, rolled, 0.0), axis=0, keepdims=True)
        for h in range(heads):
            ks_ref[pl.ds(h, nt, stride=heads), :] = kn_ref[:, h * hd:(h + 1) * hd]
            vs_ref[pl.ds(h, nt, stride=heads), :] = vn_ref[:, h * hd:(h + 1) * hd]
        kn = _pad_rows(ks_ref[...], LANES).astype(BF16)
        vn = _pad_rows(vs_ref[...], LANES).astype(BF16)
        s = _dot_nt(qall_ref[...], kn) + cq - ck
        rows = lax.broadcasted_iota(jnp.int32, s.shape, 0)
        cols = lax.broadcasted_iota(jnp.int32, s.shape, 1)
        ok = (head_of_col(cols) == jnp.right_shift(rows, tbits))
        ok = ok & (jnp.right_shift(cols, hbits) <= jnp.bitwise_and(rows, nt - 1)) & (cols < nrow)
        s = jnp.where(ok, s, -jnp.inf)
        m0 = jnp.max(s, axis=-1, keepdims=True)
        p = jnp.exp(s - m0)
        m_ref[...] = m0
        l_ref[...] = jnp.sum(p, axis=-1, keepdims=True)
        acc_ref[...] = _dot(p.astype(BF16), vn)

    qall = qall_ref[...]
    cqm = cqm_ref[...]
    ss = []
    for r in range(group):
        kb = k_refs[r][...].astype(BF16)
        ss.append(_dot_nt(qall, kb) + (cqm - cp_ref[r:r + 1, :]))
    mx = ss[0]
    for s in ss[1:]:
        mx = jnp.maximum(mx, s)
    m_prev = m_ref[...]
    m_new = jnp.maximum(m_prev, jnp.max(mx, axis=-1, keepdims=True))
    alpha = jnp.exp(m_prev - m_new)
    acc = alpha * acc_ref[...]
    psum = jnp.zeros((nrow, prow), F32)
    for r in range(group):
        p = jnp.exp(ss[r] - m_new)
        psum = psum + p
        acc = acc + _dot(p.astype(BF16), v_refs[r][...].astype(BF16))
    l_ref[...] = alpha * l_ref[...] + jnp.sum(psum, axis=-1, keepdims=True)
    acc_ref[...] = acc
    m_ref[...] = m_new

    @pl.when(g == ng - 1)
    def _():
        o = acc_ref[...] / l_ref[...]
        for h in range(heads):
            o_ref[:, h * hd:(h + 1) * hd] = o[h * nt:(h + 1) * nt, :]


def _fox_sample(zf, ff, b_fox_pad, ctot, c_past, cache_k, cache_v, page_table, layer, depth,
                row0, nt, heads, hd, group, qcol, prev):
    nb, n_pages = page_table.shape
    prow = cache_k.shape[2]
    width = heads * hd
    ng = n_pages // group
    rb0 = row0 // nt
    nrow = heads * nt
    aliased = prev is not None

    def page_map(r):
        return lambda b, g, pt: (layer, pt[b, g * group + r], 0, 0)

    kern = functools.partial(_fox_sample_kernel, group=group, heads=heads, hd=hd, nt=nt, aliased=aliased)
    new_rows = pl.BlockSpec((None, nrow, hd), lambda b, g, pt: (layer, b, 0))
    new_shape = jax.ShapeDtypeStruct((depth, nb * nrow, hd), F32)
    in_specs = ([_any_spec(), _any_spec()] if aliased else []) + [
        pl.BlockSpec((nt, width), lambda b, g, pt: (rb0 + b, qcol)),
        pl.BlockSpec((nt, width), lambda b, g, pt: (rb0 + b, qcol + 1)),
        pl.BlockSpec((nt, width), lambda b, g, pt: (rb0 + b, qcol + 2)),
        pl.BlockSpec((nt, LANES), lambda b, g, pt: (rb0 + b, 0)),
        pl.BlockSpec((None, 1, LANES), lambda b, g, pt: (layer, 0, 0)),
        pl.BlockSpec((None, 1, LANES), lambda b, g, pt: (layer * nb + b, 0, 0)),
        pl.BlockSpec((None, group, c_past.shape[2]), lambda b, g, pt: (layer * nb + b, g, 0)),
    ]
    in_specs += [pl.BlockSpec((None, None, prow, hd), page_map(r)) for r in range(group)]
    in_specs += [pl.BlockSpec((None, None, prow, hd), page_map(r)) for r in range(group)]
    return pl.pallas_call(
        kern,
        grid_spec=pltpu.PrefetchScalarGridSpec(
            num_scalar_prefetch=1,
            grid=(nb, ng),
            in_specs=in_specs,
            out_specs=[pl.BlockSpec((nt, width), lambda b, g, pt: (b, 0)),
                       pl.BlockSpec((None, nt, LANES), lambda b, g, pt: (b, 0, 0)),
                       new_rows, new_rows],
            scratch_shapes=[
                pltpu.VMEM((nrow, hd), BF16),
                pltpu.VMEM((nrow, prow), F32),
                pltpu.VMEM((nrow, 1), F32),
                pltpu.VMEM((nrow, 1), F32),
                pltpu.VMEM((nrow, hd), F32),
            ],
        ),
        out_shape=[jax.ShapeDtypeStruct((nb * nt, width), F32),
                   jax.ShapeDtypeStruct((nb, nt, LANES), F32),
                   new_shape, new_shape],
        input_output_aliases={1: 2, 2: 3} if aliased else {},
        compiler_params=_params("arbitrary", "arbitrary"),
        name="fox_sample",
    )(page_table, *(prev if aliased else ()), zf, zf, zf, ff, b_fox_pad, ctot, c_past,
      *([cache_k] * group), *([cache_v] * group))


def _hgrn_operators(c, real):
    t = np.arange(c)[:, None]
    j = np.arange(c)[None, :]
    mats = [(j <= t), (j > t)]
    masks = []
    m = 1
    while m < real:
        blk = t // (2 * m)
        mid = blk * (2 * m) + m - 1
        second = t > mid
        if m < SUBLANES:
            mats.append(np.where(second, (j > mid) & (j <= t), (j > t) & (j <= mid)))
        s = np.arange(c)[None, :]
        masks.append((blk == (s // (2 * m))) & second & (s <= (s // (2 * m)) * (2 * m) + m - 1))
        m *= 2
    a = np.concatenate(mats, axis=0).astype(np.float32)
    a3 = np.concatenate([a, a, a], axis=1)
    return jnp.asarray(a3, BF16), jnp.asarray(np.stack(masks).astype(np.float32)), len(masks)


def _hgrn_kernel(q_ref, f_ref, i_ref, g_ref, lbraw_ref, gn_ref, a3_ref, msk_ref, s0_ref,
                 o_ref, sfin_ref, st_ref, *, layer, heads, dk, n_lev, real):
    c = HGRN_CHUNK
    ci = pl.program_id(1)
    nc = pl.num_programs(1)

    @pl.when(ci == 0)
    def _():
        for h in range(heads):
            st_ref[h] = s0_ref[h].T

    raw = lbraw_ref[...]
    e = jnp.exp(raw - jnp.max(raw, axis=0, keepdims=True))
    sm = e / jnp.sum(e, axis=0, keepdims=True)
    cs = sm[0:1]
    for r in range(1, layer + 1):
        cs = cs + sm[r:r + 1]
    lb = cs - sm[0:1]
    log_lb = jnp.log(lb)
    log_1m = jnp.log1p(-lb)

    zf = _pad_rows(f_ref[...], c)
    q = _pad_rows(q_ref[...], c)
    v = _pad_rows(i_ref[...], c)
    b = log_1m + _log_sigmoid(zf)
    mx = jnp.maximum(log_lb, b)
    g = mx + jnp.log(1.0 + jnp.exp(-jnp.abs(log_lb - b)))
    kr = (1.0 - lb) * _sigmoid(-zf)
    if real < c:
        valid = lax.broadcasted_iota(jnp.int32, g.shape, 0) < real
        g = jnp.where(valid, g, 0.0)
        kr = jnp.where(valid, kr, 0.0)

    e = _dot(a3_ref[...], jnp.concatenate(_split3(g), axis=0))
    eq = e[0:c]
    xq = jnp.exp(eq)
    xk = jnp.exp(e[c:2 * c])
    xlast = xq[c - 1:c]
    n_mx = e.shape[0] // c - 2

    def level_factor(l):
        if l < n_mx:
            return jnp.exp(e[(2 + l) * c:(3 + l) * c])
        m = 1 << l
        mids = [jnp.broadcast_to(eq[b0 + m - 1:b0 + m, :], (2 * m, eq.shape[1]))
                for b0 in range(0, c, 2 * m)]
        r = mids[0] if len(mids) == 1 else jnp.concatenate(mids, axis=0)
        return jnp.exp(-jnp.abs(eq - r))

    xls = [level_factor(l) for l in range(n_lev)]
    qe = (q * xq).astype(BF16)
    ke = (kr * xk).astype(BF16)
    vb = v.astype(BF16)
    for h in range(heads):
        hs = slice(h * dk, (h + 1) * dk)
        st = st_ref[h]
        o = _dot_nt(qe[:, hs], st.astype(BF16))
        att = jnp.zeros((c, c), F32)
        for l in range(n_lev):
            xl = xls[l][:, hs]
            p = _dot_nt((q[:, hs] * xl).astype(BF16), (kr[:, hs] * xl).astype(BF16))
            att = att + p * msk_ref[l]
        o = o + _dot(att.astype(BF16), vb[:, hs])
        o = o + jnp.sum(q[:, hs] * kr[:, hs], axis=-1, keepdims=True) * v[:, hs]
        st_ref[h] = st * xlast[:, hs] + _dot_tn(vb[:, hs], ke[:, hs])
        on = _rms_rows(o, gn_ref[:, hs])
        gg = g_ref[:, hs]
        o_ref[:, hs] = (on[:real] * (gg * _sigmoid(gg))).astype(o_ref.dtype)

    @pl.when(ci == nc - 1)
    def _():
        for h in range(heads):
            sfin_ref[h] = st_ref[h].T


def _hgrn(zf, hgrn_lb, hgrn_norm, s0, layer, row0, n_seq, seq, heads, dk, col0, out_rows, out_dtype):
    c = HGRN_CHUNK
    real = min(seq, c)
    assert seq % real == 0
    nc = seq // real
    width = heads * dk
    rb0 = row0 // real
    a3, masks, n_lev = _hgrn_operators(c, real)
    depth = hgrn_lb.shape[0]
    blk = lambda col: pl.BlockSpec((real, width), lambda n, i: (rb0 + n * nc + i, col))
    kern = functools.partial(_hgrn_kernel, layer=layer, heads=heads, dk=dk, n_lev=n_lev, real=real)
    in_specs = [
        blk(col0), blk(col0 + 1), blk(col0 + 2), blk(col0 + 3),
        pl.BlockSpec((depth, width), lambda n, i: (0, 0)),
        pl.BlockSpec((None, 1, width), lambda n, i: (layer, 0, 0)),
        pl.BlockSpec(a3.shape, lambda n, i: (0, 0)),
        pl.BlockSpec(masks.shape, lambda n, i: (0, 0, 0)),
        pl.BlockSpec((None, heads, dk, dk), lambda n, i: (n, 0, 0, 0)),
    ]
    return pl.pallas_call(
        kern,
        grid=(n_seq, nc),
        in_specs=in_specs,
        out_specs=[
            pl.BlockSpec((real, width), lambda n, i: (n * nc + i, 0)),
            pl.BlockSpec((None, heads, dk, dk), lambda n, i: (n, 0, 0, 0)),
        ],
        out_shape=[
            jax.ShapeDtypeStruct((out_rows, width), out_dtype),
            jax.ShapeDtypeStruct((n_seq, heads, dk, dk), F32),
        ],
        scratch_shapes=[pltpu.VMEM((heads, dk, dk), F32)],
        compiler_params=_params("arbitrary", "arbitrary"),
        name="hgrn",
    )(zf, zf, zf, zf, hgrn_lb, hgrn_norm, a3, masks, s0)


def _patch_rows_kernel(*refs):
    n = len(refs) // 3
    for small_ref, out_ref in zip(refs[n:2 * n], refs[2 * n:]):
        out_ref[...] = small_ref[...].astype(out_ref.dtype)


def _patch_rows(bigs, smalls, row0):
    rows = smalls[0].shape[0]
    assert row0 % rows == 0
    n = len(bigs)
    return pl.pallas_call(
        _patch_rows_kernel,
        grid=(1,),
        in_specs=[_any_spec()] * n + [pl.BlockSpec(s.shape, lambda i: (0, 0)) for s in smalls],
        out_specs=[pl.BlockSpec((rows, b.shape[1]), lambda i: (row0 // rows, 0)) for b in bigs],
        out_shape=[jax.ShapeDtypeStruct(b.shape, b.dtype) for b in bigs],
        input_output_aliases={i: i for i in range(n)},
        compiler_params=_params("arbitrary"),
        name="patch_rows",
    )(*bigs, *smalls)


def _merge_kernel(ma_ref, ob_ref, oc_ref, wa_ref, wb_ref, wc_ref, ga_ref, gb_ref, gc_ref, mix_ref):
    ya = _dot(ma_ref[...], wa_ref[...])
    yb = _dot(ob_ref[...], wb_ref[...])
    yc = _dot(oc_ref[...], wc_ref[...])
    gate = lambda r: _sigmoid(r[...].astype(F32))
    mix = gate(ga_ref) * ya + gate(gb_ref) * yb + gate(gc_ref) * yc
    mix_ref[...] = mix.astype(mix_ref.dtype)


def _merge(m_a, o_b, o_c, w_a, w_b, w_c, zg, layer, tm, tn):
    m, k = m_a.shape
    d = w_a.shape[2]
    nj = d // tn
    act = pl.BlockSpec((tm, k), lambda i, j: (i, 0))
    wsp = pl.BlockSpec((None, k, tn), lambda i, j: (layer, 0, j))
    gate = lambda r: pl.BlockSpec((tm, tn), lambda i, j: (i, r * nj + j))
    return pl.pallas_call(
        _merge_kernel,
        grid=(m // tm, nj),
        in_specs=[act, act, act, wsp, wsp, wsp, gate(0), gate(1), gate(2)],
        out_specs=pl.BlockSpec((tm, tn), lambda i, j: (i, j)),
        out_shape=jax.ShapeDtypeStruct((m, d), BF16),
        compiler_params=_params("arbitrary", "arbitrary"),
        name="merge",
    )(m_a, o_b, o_c, w_a, w_b, w_c, zg, zg, zg)


def _residual_matmul_kernel(a_ref, w_ref, x_ref, o_ref):
    o_ref[...] = x_ref[...] + _dot(a_ref[...], w_ref[...])


def _weight_spec(shape, index_map, resident):
    if resident:
        return pl.BlockSpec(shape, index_map, pipeline_mode=pl.Buffered(1))
    return pl.BlockSpec(shape, index_map)


def _residual_matmul(a, w, x, layer, tm, tn):
    m, k = a.shape
    d = w.shape[2]
    return pl.pallas_call(
        _residual_matmul_kernel,
        grid=(m // tm, d // tn),
        in_specs=[
            pl.BlockSpec((tm, k), lambda i, j: (i, 0)),
            _weight_spec((None, k, tn), lambda i, j: (layer, 0, j), tn == d),
            pl.BlockSpec((tm, tn), lambda i, j: (i, j)),
        ],
        out_specs=pl.BlockSpec((tm, tn), lambda i, j: (i, j)),
        out_shape=jax.ShapeDtypeStruct((m, d), F32),
        compiler_params=_params("arbitrary", "arbitrary"),
        name="residual_matmul",
    )(a, w, x)


def _swiglu_up_kernel(x_ref, g_ref, wg_ref, wu_ref, a_ref, h_ref):
    @pl.when(pl.program_id(1) == 0)
    def _():
        h_ref[...] = _rms_rows(x_ref[...], g_ref[...]).astype(BF16)

    h = h_ref[...]
    gate = _dot(h, wg_ref[...])
    up = _dot(h, wu_ref[...])
    a_ref[...] = (gate * _sigmoid(gate) * up).astype(a_ref.dtype)


def _swiglu_up(x, norm, w_gate_up, layer, tm, tn):
    m, d = x.shape
    f = w_gate_up.shape[2] // 2
    nj = f // tn
    return pl.pallas_call(
        _swiglu_up_kernel,
        grid=(m // tm, nj),
        in_specs=[
            pl.BlockSpec((tm, d), lambda i, j: (i, 0)),
            pl.BlockSpec((None, 1, d), lambda i, j: (layer, 0, 0)),
            pl.BlockSpec((None, d, tn), lambda i, j: (layer, 0, j)),
            pl.BlockSpec((None, d, tn), lambda i, j: (layer, 0, nj + j)),
        ],
        out_specs=pl.BlockSpec((tm, tn), lambda i, j: (i, j)),
        out_shape=jax.ShapeDtypeStruct((m, f), BF16),
        scratch_shapes=[pltpu.VMEM((tm, d), BF16)],
        compiler_params=_params("arbitrary", "arbitrary"),
        name="swiglu_up",
    )(x, norm, w_gate_up, w_gate_up)


def _ple_kernel(x_ref, g_ref, wg_ref, p_ref, wp_ref, o_ref, h_ref, *, tn):
    j = pl.program_id(1)

    @pl.when(j == 0)
    def _():
        h_ref[...] = _rms_rows(x_ref[...], g_ref[...]).astype(BF16)

    gate = _sigmoid(_dot(h_ref[...], wg_ref[...]))
    emb = _dot(p_ref[...].astype(BF16), wp_ref[...])
    xs = x_ref[:, pl.ds(pl.multiple_of(j * tn, tn), tn)]
    o_ref[...] = xs + gate * emb


def _ple(x, norm, w_gate, p, w_proj, layer, tm, tn):
    m, d = x.shape
    pd = p.shape[2]
    return pl.pallas_call(
        functools.partial(_ple_kernel, tn=tn),
        grid=(m // tm, d // tn),
        in_specs=[
            pl.BlockSpec((tm, d), lambda i, j: (i, 0)),
            pl.BlockSpec((None, 1, d), lambda i, j: (layer, 0, 0)),
            _weight_spec((None, d, tn), lambda i, j: (layer, 0, j), tn == d),
            pl.BlockSpec((None, tm, pd), lambda i, j: (layer, i, 0)),
            _weight_spec((None, pd, tn), lambda i, j: (layer, 0, j), tn == d),
        ],
        out_specs=pl.BlockSpec((tm, tn), lambda i, j: (i, j)),
        out_shape=jax.ShapeDtypeStruct((m, d), F32),
        scratch_shapes=[pltpu.VMEM((tm, d), BF16)],
        compiler_params=_params("arbitrary", "arbitrary"),
        name="ple",
    )(x, norm, w_gate, p, w_proj)


def _final_norm_kernel(x_ref, g_ref, o_ref):
    o_ref[...] = _rms_rows(x_ref[...], g_ref[...])


def _final_norm(x, g, tm):
    m, d = x.shape
    return pl.pallas_call(
        _final_norm_kernel,
        grid=(m // tm,),
        in_specs=[pl.BlockSpec((tm, d), lambda i: (i, 0)), pl.BlockSpec((1, d), lambda i: (0, 0))],
        out_specs=pl.BlockSpec((tm, d), lambda i: (i, 0)),
        out_shape=jax.ShapeDtypeStruct((m, d), F32),
        compiler_params=_params("arbitrary"),
        name="final_norm",
    )(x, g)


def kernel(x_prompt, x_sample, cache_k, cache_v, cache_logf, state_conv, state_hgrn, page_table,
           p_prompt, p_sample, norm1, w_in, b_fox, conv_w, hgrn_lb, hgrn_norm, w_a, w_b, w_c, w_o,
           norm2, w_gate_up, w_down, ple_norm, ple_gate, ple_proj, final_norm):
    bp, seq, d = x_prompt.shape
    db, dseq, _ = x_sample.shape
    depth = w_in.shape[0]
    heads, hd = cache_k.shape[3], cache_k.shape[4]
    page = cache_k.shape[2]
    n_pool = cache_k.shape[1]
    cw = w_a.shape[1]
    fw = w_b.shape[1]
    rw = w_c.shape[1]
    rheads = state_hgrn.shape[2]
    dk = state_hgrn.shape[3]
    assert fw == heads * hd and rw == rheads * dk and state_hgrn.shape[4] == dk and cw == fw == rw
    mp = bp * seq
    ms = db * dseq
    m = mp + ms

    o_ff = 3 * cw + 3 * fw
    widths = (3 * cw, fw, 2 * fw, 4 * rw, 3 * d)
    w_main = _repack_w_in(w_in, o_ff, heads, fw)
    w_ff = jnp.pad(w_in[:, :, o_ff:o_ff + heads], ((0, 0), (0, 0), (0, LANES - heads))).astype(BF16)
    assert w_main.shape[2] == sum(widths)
    tn_in = fw
    qcol = 3 * cw // tn_in
    rcol = qcol + 3

    bf = lambda a: a.astype(BF16)
    w_a_b, w_b_b, w_c_b, w_o_b = bf(w_a), bf(w_b), bf(w_c), bf(w_o)
    w_gu_b, w_dn_b, w_pg_b, w_pp_b = bf(w_gate_up), bf(w_down), bf(ple_gate), bf(ple_proj)
    row3 = lambda a: a.reshape(a.shape[0], 1, a.shape[1])
    norm1_r, norm2_r, ple_norm_r, hnorm_r = row3(norm1), row3(norm2), row3(ple_norm), row3(hgrn_norm)
    b_fox_pad = row3(jnp.pad(b_fox, ((0, 0), (0, LANES - heads))))

    x = _stack_rows(x_prompt.reshape(1, mp, d), x_sample.reshape(1, ms, d)).reshape(m, d)
    p_all = _stack_rows(p_prompt.reshape(depth, mp, -1), p_sample.reshape(depth, ms, -1))
    ck = cache_k.reshape(depth, n_pool, page * heads, hd)
    cv = cache_v.reshape(depth, n_pool, page * heads, hd)

    n_pages = page_table.shape[1]
    group = _tile(n_pages, 16, 1)
    lf_flat = cache_logf.reshape(depth, n_pool, 1, page * heads)
    lf_pages = _gather_logf_pages(lf_flat, page_table, group)
    c_past, c_tot = _past_cumsum(lf_pages, heads)

    tm = _tile(m, 704, 16)
    tq =_tile(seq, 512, LANES)
    conv_tile = _tile(seq, 512, 8)
    zeros_conv = jnp.zeros((bp, 2, cw), F32)
    zeros_state = jnp.zeros((bp, rheads, dk, dk), F32)

    outs = {k: [] for k in ("lfp", "cp", "sp", "lfs", "cs", "ss")}
    kvp = None
    kvs = None
    for i in range(depth):
        zf, kvb, zg, ff = _in_proj(x, norm1_r, w_main, w_ff, i, tm, tn_in, widths, hd)

        m_a, conv_p = _conv(zf, conv_w, zeros_conv, i, 0, bp, seq, conv_tile, cw, m, BF16)
        ma_s, conv_s = _conv(zf, conv_w, state_conv[i], i, mp, db, dseq, dseq, cw, ms, F32)

        lf_p, qaug_t, kaug, v_t, kp, vp = _logf_prompt(ff, b_fox_pad, zf, kvb, i, depth, bp, seq,
                                                       heads, hd, qcol, kvp)
        kvp = (kp, vp)
        o_b = _fox_prompt(qaug_t, kaug, v_t, m, bp, seq, heads, hd, tq)
        ob_s, lf_s, ks, vs = _fox_sample(zf, ff, b_fox_pad, c_tot, c_past, ck, cv, page_table, i,
                                         depth, mp, dseq, heads, hd, group, qcol, kvs)
        kvs = (ks, vs)

        o_c, st_p = _hgrn(zf, hgrn_lb, hnorm_r, zeros_state, i, 0, bp, seq, rheads, dk, rcol, m, BF16)
        oc_s, st_s = _hgrn(zf, hgrn_lb, hnorm_r, state_hgrn[i], i, mp, db, dseq, rheads, dk, rcol,
                           ms, F32)

        m_a, o_b, o_c = _patch_rows((m_a, o_b, o_c), (ma_s, ob_s, oc_s), mp)
        mix = _merge(m_a, o_b, o_c, w_a_b, w_b_b, w_c_b, zg, i, tm, 1024)
        x = _residual_matmul(mix, w_o_b, x, i, tm, d)

        act = _swiglu_up(x, norm2_r, w_gu_b, i, tm, _tile(w_gate_up.shape[2] // 2, 512, 2 * LANES))
        x = _residual_matmul(act, w_dn_b, x, i, tm, 512)
        x = _ple(x, ple_norm_r, w_pg_b, p_all, w_pp_b, i, tm, d)

        outs["lfp"].append(lf_p[:, :, :heads])
        outs["lfs"].append(lf_s[:, :, :heads])
        outs["cp"].append(conv_p)
        outs["cs"].append(conv_s)
        outs["sp"].append(st_p)
        outs["ss"].append(st_s)

    y = _final_norm(x, final_norm.reshape(1, d), tm)
    st = lambda k: jnp.stack(outs[k])
    return (y[:mp].reshape(bp, seq, d), y[mp:].reshape(db, dseq, d),
            kp.reshape(depth, bp, seq, heads, hd), vp.reshape(depth, bp, seq, heads, hd),
            st("lfp"), st("cp"), st("sp"),
            ks.reshape(depth, db, dseq, heads, hd), vs.reshape(depth, db, dseq, heads, hd),
            st("lfs"), st("cs"), st("ss"))
```

```python
import functools

import numpy as np
import jax
import jax.numpy as jnp
from jax import lax
from jax.experimental import pallas as pl
from jax.experimental.pallas import tpu as pltpu

F32 = jnp.float32
BF16 = jnp.bfloat16
RMS_EPS = 1e-6
LANES = 128
SUBLANES = 8
HGRN_CHUNK = 128
VMEM_LIMIT = 56 * 1024 * 1024


def _tile(n, target, align):
    best = None
    for t in range(align, min(n, target) + 1, align):
        if n % t == 0:
            best = t
    assert best is not None, (n, target, align)
    return best


def _params(*sem):
    return pltpu.CompilerParams(dimension_semantics=sem, vmem_limit_bytes=VMEM_LIMIT)


def _sigmoid(x):
    return 1.0 / (1.0 + jnp.exp(-x))


def _log_sigmoid(x):
    return jnp.minimum(x, 0.0) - jnp.log(1.0 + jnp.exp(-jnp.abs(x)))


def _rms_rows(x, g):
    ms = jnp.mean(x * x, axis=-1, keepdims=True)
    return x * lax.rsqrt(ms + RMS_EPS) * g


def _split3(x):
    hi = x.astype(BF16)
    r1 = x - hi.astype(F32)
    mid = r1.astype(BF16)
    lo = (r1 - mid.astype(F32)).astype(BF16)
    return hi, mid, lo


def _dot(a, b):
    return jnp.dot(a, b, preferred_element_type=F32)


def _dot_nt(a, b):
    return lax.dot_general(a, b, (((1,), (1,)), ((), ())), preferred_element_type=F32)


def _dot_tn(a, b):
    return lax.dot_general(a, b, (((0,), (0,)), ((), ())), preferred_element_type=F32)


def _pad_rows(a, n):
    if a.shape[0] == n:
        return a
    return jnp.concatenate([a, jnp.zeros((n - a.shape[0], a.shape[1]), a.dtype)], axis=0)


def _any_spec():
    return pl.BlockSpec(memory_space=pl.ANY)


def _copy_kernel(*refs, aliased):
    a_ref, o_ref = refs[1:] if aliased else refs
    o_ref[...] = a_ref[...]


def _stack_rows(a, b):
    nl, ra, c = a.shape
    rb = b.shape[1]
    assert ra % rb == 0
    ta = _tile(ra, 1024, 8)
    shape = jax.ShapeDtypeStruct((nl, ra + rb, c), a.dtype)
    out = pl.pallas_call(
        functools.partial(_copy_kernel, aliased=False),
        grid=(nl, ra // ta),
        in_specs=[pl.BlockSpec((None, ta, c), lambda l, i: (l, i, 0))],
        out_specs=pl.BlockSpec((None, ta, c), lambda l, i: (l, i, 0)),
        out_shape=shape,
        compiler_params=_params("arbitrary", "arbitrary"),
        name="stack_rows_a",
    )(a)
    return pl.pallas_call(
        functools.partial(_copy_kernel, aliased=True),
        grid=(nl,),
        in_specs=[_any_spec(), pl.BlockSpec((None, rb, c), lambda l: (l, 0, 0))],
        out_specs=pl.BlockSpec((None, rb, c), lambda l: (l, ra // rb, 0)),
        out_shape=shape,
        input_output_aliases={0: 0},
        compiler_params=_params("arbitrary"),
        name="stack_rows_b",
    )(out, b)


def _in_proj_kernel(x_ref, g_ref, w_ref, wff_ref, zf_ref, kvb_ref, zg_ref, ff_ref, h_ref,
                    *, bounds, scale):
    j = pl.program_id(1)
    jq, jk, jr, jg = bounds

    @pl.when(j == 0)
    def _():
        h = _rms_rows(x_ref[...], g_ref[...]).astype(BF16)
        h_ref[...] = h
        ff_ref[...] = _dot(h, wff_ref[...])

    @pl.when(j < jq)
    def _():
        zf_ref[...] = _dot(h_ref[...], w_ref[...])

    @pl.when((j >= jq) & (j < jk))
    def _():
        zf_ref[...] = _dot(h_ref[...], w_ref[...]) * scale

    @pl.when((j >= jk) & (j < jr))
    def _():
        z = _dot(h_ref[...], w_ref[...])
        zf_ref[...] = z
        kvb_ref[...] = z.astype(BF16)

    @pl.when((j >= jr) & (j < jg))
    def _():
        zf_ref[...] = _dot(h_ref[...], w_ref[...])

    @pl.when(j >= jg)
    def _():
        zg_ref[...] = _dot(h_ref[...], w_ref[...]).astype(BF16)


def _in_proj(x, norm, w_main, w_ff, layer, tm, tn, widths, hd):
    m, d = x.shape
    assert all(w % tn == 0 for w in widths) and sum(widths) == w_main.shape[2]
    cuts = np.cumsum([w // tn for w in widths])
    jq, jk, jr, jg, nj = (int(c) for c in cuts)
    kern = functools.partial(_in_proj_kernel, bounds=(jq, jk, jr, jg), scale=hd ** -0.5)
    return pl.pallas_call(
        kern,
        grid=(m // tm, nj),
        in_specs=[
            pl.BlockSpec((tm, d), lambda i, j: (i, 0)),
            pl.BlockSpec((None, 1, d), lambda i, j: (layer, 0, 0)),
            pl.BlockSpec((None, d, tn), lambda i, j: (layer, 0, j)),
            pl.BlockSpec((None, d, LANES), lambda i, j: (layer, 0, 0)),
        ],
        out_specs=[
            pl.BlockSpec((tm, tn), lambda i, j: (i, jnp.minimum(j, jg - 1))),
            pl.BlockSpec((tm, tn), lambda i, j: (i, jnp.clip(j - jk, 0, jr - jk - 1))),
            pl.BlockSpec((tm, tn), lambda i, j: (i, jnp.clip(j - jg, 0, nj - jg - 1))),
            pl.BlockSpec((tm, LANES), lambda i, j: (i, 0)),
        ],
        out_shape=[
            jax.ShapeDtypeStruct((m, jg * tn), F32),
            jax.ShapeDtypeStruct((m, widths[2]), BF16),
            jax.ShapeDtypeStruct((m, widths[4]), BF16),
            jax.ShapeDtypeStruct((m, LANES), F32),
        ],
        scratch_shapes=[pltpu.VMEM((tm, d), BF16)],
        compiler_params=_params("arbitrary", "arbitrary"),
        name="in_proj",
    )(x, norm, w_main, w_ff)


def _bias_operators(heads, width):
    hd = width // heads
    selq = np.zeros((3 * LANES, width), np.float32)
    selk = np.zeros((3 * LANES, width), np.float32)
    cq = np.zeros((1, width), np.float32)
    ck = np.zeros((1, width), np.float32)
    for h in range(heads):
        for r in range(3):
            selq[r * LANES + h, h * hd + r] = 1.0
            selk[r * LANES + h, h * hd + 3 + r] = -1.0
            cq[0, h * hd + 3 + r] = 1.0
            ck[0, h * hd + r] = 1.0
    return jnp.asarray(selq, BF16), jnp.asarray(selk, BF16), jnp.asarray(cq), jnp.asarray(ck)


def _logf_prompt_kernel(*refs, heads, hd, aliased):
    if aliased:
        refs = refs[2:]
    (ff_ref, b_ref, q_ref, k_ref, k32_ref, v32_ref, tri_ref, selq_ref, selk_ref, cq_ref, ck_ref,
     lf_ref, qat_ref, ka_ref, vt_ref, kp_ref, vp_ref, carry_ref) = refs
    blk = ff_ref.shape[0]
    for h in range(heads):
        kp_ref[pl.ds(h, blk, stride=heads), :] = k32_ref[:, h * hd:(h + 1) * hd]
        vp_ref[pl.ds(h, blk, stride=heads), :] = v32_ref[:, h * hd:(h + 1) * hd]

    @pl.when(pl.program_id(1) == 0)
    def _():
        carry_ref[...] = jnp.zeros(carry_ref.shape, F32)

    lf = _log_sigmoid(ff_ref[...] + b_ref[...])
    lf_ref[...] = lf
    cb = _dot(tri_ref[...], jnp.concatenate(_split3(lf), axis=0)) + carry_ref[...]
    carry_ref[...] = cb[blk - 1:blk, :]
    c3 = jnp.concatenate(_split3(cb), axis=1)
    qb = _dot(c3, selq_ref[...]) + cq_ref[...]
    kb = (_dot(c3, selk_ref[...]) + ck_ref[...]).astype(BF16)
    k = k_ref[...]
    for h in range(heads):
        hs = slice(h * hd, (h + 1) * hd)
        qat_ref[2 * h * hd:(2 * h + 1) * hd, :] = q_ref[:, hs].T.astype(BF16)
        qat_ref[(2 * h + 1) * hd:(2 * h + 2) * hd, :] = qb[:, hs].T.astype(BF16)
        vt_ref[hs, :] = v32_ref[:, hs].T.astype(BF16)
        ka_ref[:, 2 * h * hd:(2 * h + 1) * hd] = k[:, hs]
        ka_ref[:, (2 * h + 1) * hd:(2 * h + 2) * hd] = kb[:, hs]


def _logf_prompt(ff, b_fox_pad, zf, kvb, layer, depth, batch, seq, heads, hd, qcol, prev):
    width = heads * hd
    blk = _tile(seq, 512, LANES)
    nb = seq // blk
    tri = np.tril(np.ones((blk, blk), np.float32))
    tri3 = jnp.asarray(np.concatenate([tri, tri, tri], axis=1), BF16)
    ops = _bias_operators(heads, width)
    full = lambda a: pl.BlockSpec(a.shape, lambda b, i: (0,) * a.ndim)
    rows = lambda c, w: pl.BlockSpec((blk, w), lambda b, i: (b * nb + i, c))
    cache = pl.BlockSpec((None, blk * heads, hd), lambda b, i: (layer, b * nb + i, 0))
    aliased = prev is not None
    in_specs = [
        rows(0, LANES),
        pl.BlockSpec((None, 1, LANES), lambda b, i: (layer, 0, 0)),
        rows(qcol, width),
        rows(0, width),
        rows(qcol + 1, width),
        rows(qcol + 2, width),
        full(tri3), full(ops[0]), full(ops[1]), full(ops[2]), full(ops[3]),
    ]
    args = [ff, b_fox_pad, zf, kvb, zf, zf, tri3, *ops]
    if aliased:
        in_specs = [_any_spec(), _any_spec()] + in_specs
        args = list(prev) + args
    cache_shape = jax.ShapeDtypeStruct((depth, batch * seq * heads, hd), F32)
    return pl.pallas_call(
        functools.partial(_logf_prompt_kernel, heads=heads, hd=hd, aliased=aliased),
        grid=(batch, nb),
        in_specs=in_specs,
        out_specs=[
            pl.BlockSpec((None, blk, LANES), lambda b, i: (b, i, 0)),
            pl.BlockSpec((2 * width, blk), lambda b, i: (0, b * nb + i)),
            rows(0, 2 * width),
            pl.BlockSpec((width, blk), lambda b, i: (0, b * nb + i)),
            cache,
            cache,
        ],
        out_shape=[
            jax.ShapeDtypeStruct((batch, seq, LANES), F32),
            jax.ShapeDtypeStruct((2 * width, batch * seq), BF16),
            jax.ShapeDtypeStruct((batch * seq, 2 * width), BF16),
            jax.ShapeDtypeStruct((width, batch * seq), BF16),
            cache_shape,
            cache_shape,
        ],
        scratch_shapes=[pltpu.VMEM((1, LANES), F32)],
        input_output_aliases={0: 4, 1: 5} if aliased else {},
        compiler_params=_params("arbitrary", "arbitrary"),
        name="logf_prompt",
    )(*args)


def _conv_kernel(ab_ref, ac_ref, av_ref, w_ref, prev_ref, m_ref, newc_ref, carry_ref):
    i = pl.program_id(1)
    t = ac_ref.shape[0]

    @pl.when(i == 0)
    def _():
        carry_ref[...] = prev_ref[...]

    u = ac_ref[...] * av_ref[...]
    row = lax.broadcasted_iota(jnp.int32, u.shape, 0)
    p1 = carry_ref[1:2, :]
    p2 = carry_ref[0:1, :]
    u1 = jnp.where(row == 0, p1, pltpu.roll(u, 1, 0))
    u2 = pltpu.roll(u, 2, 0)
    u2 = jnp.where(row == 0, p2, jnp.where(row == 1, p1, u2))
    w = w_ref[...]
    conv = w[0:1, :] * u2 + w[1:2, :] * u1 + w[2:3, :] * u
    m_ref[...] = (ab_ref[...] * conv).astype(m_ref.dtype)
    last2 = u[t - 2:t, :]
    carry_ref[...] = last2
    newc_ref[...] = last2


def _conv(zf, conv_w, prev, layer, row0, n_seq, seq, tile, width, out_rows, out_dtype):
    nt = seq // tile
    rb0 = row0 // tile
    rows = lambda c: pl.BlockSpec((tile, width), lambda n, i: (rb0 + n * nt + i, c))
    return pl.pallas_call(
        _conv_kernel,
        grid=(n_seq, nt),
        in_specs=[rows(0), rows(1), rows(2),
                  pl.BlockSpec((None, 3, width), lambda n, i: (layer, 0, 0)),
                  pl.BlockSpec((None, 2, width), lambda n, i: (n, 0, 0))],
        out_specs=[
            pl.BlockSpec((tile, width), lambda n, i: (n * nt + i, 0)),
            pl.BlockSpec((None, 2, width), lambda n, i: (n, 0, 0)),
        ],
        out_shape=[
            jax.ShapeDtypeStruct((out_rows, width), out_dtype),
            jax.ShapeDtypeStruct((n_seq, 2, width), F32),
        ],
        scratch_shapes=[pltpu.VMEM((2, width), F32)],
        compiler_params=_params("arbitrary", "arbitrary"),
        name="gated_conv",
    )(zf, zf, zf, conv_w, prev)


def _fox_prompt_kernel(qat_ref, ka_ref, vt_ref, o_ref, m_ref, l_ref, acc_ref, *, heads, hd):
    qi = pl.program_id(1)
    ki = pl.program_id(2)
    tq = qat_ref.shape[1]
    tk = ka_ref.shape[0]

    @pl.when(ki == 0)
    def _():
        m_ref[...] = jnp.full(m_ref.shape, -jnp.inf, F32)
        l_ref[...] = jnp.zeros(l_ref.shape, F32)
        acc_ref[...] = jnp.zeros(acc_ref.shape, F32)

    def step(masked):
        if masked:
            krow = lax.broadcasted_iota(jnp.int32, (tk, tq), 0)
            qcol = lax.broadcasted_iota(jnp.int32, (tk, tq), 1)
            keep = krow <= qcol
        for h in range(heads):
            aug = slice(2 * h * hd, (2 * h + 2) * hd)
            st = _dot(ka_ref[:, aug], qat_ref[aug, :])
            if masked:
                st = jnp.where(keep, st, -jnp.inf)
            m_prev = m_ref[h]
            m_new = jnp.maximum(m_prev, jnp.max(st, axis=0, keepdims=True))
            alpha = jnp.exp(m_prev - m_new)
            p = jnp.exp(st - m_new)
            l_ref[h] = alpha * l_ref[h] + jnp.sum(p, axis=0, keepdims=True)
            acc_ref[h] = alpha * acc_ref[h] + _dot(vt_ref[h * hd:(h + 1) * hd, :], p.astype(BF16))
            m_ref[h] = m_new

    @pl.when(ki < qi)
    def _():
        step(False)

    @pl.when(ki == qi)
    def _():
        step(True)
        for h in range(heads):
            o_ref[:, h * hd:(h + 1) * hd] = (acc_ref[h] / l_ref[h]).T.astype(o_ref.dtype)


def _fox_prompt(qaug_t, kaug, v_t, m_rows, batch, seq, heads, hd, tq):
    nq = seq // tq
    width = heads * hd
    kern = functools.partial(_fox_prompt_kernel, heads=heads, hd=hd)
    qmap = lambda b, qi, ki: (b * nq + qi, 0)
    kmap = lambda b, qi, ki: (b * nq + jnp.minimum(ki, qi), 0)
    return pl.pallas_call(
        kern,
        grid=(batch, nq, nq),
        in_specs=[
            pl.BlockSpec((2 * width, tq), lambda b, qi, ki: (0, b * nq + qi)),
            pl.BlockSpec((tq, 2 * width), kmap),
            pl.BlockSpec((width, tq), lambda b, qi, ki: (0, b * nq + jnp.minimum(ki, qi))),
        ],
        out_specs=pl.BlockSpec((tq, width), qmap),
        out_shape=jax.ShapeDtypeStruct((m_rows, width), BF16),
        scratch_shapes=[
            pltpu.VMEM((heads, 1, tq), F32),
            pltpu.VMEM((heads, 1, tq), F32),
            pltpu.VMEM((heads, hd, tq), F32),
        ],
        compiler_params=_params("arbitrary", "arbitrary", "arbitrary"),
        name="fox_prompt",
    )(qaug_t, kaug, v_t)


def _gather_rows_kernel(pt_ref, *refs):
    n_in = len(refs) - 1
    out_ref = refs[-1]
    for r in range(n_in):
        out_ref[:, r, :] = refs[r][:, 0, :]


def _gather_logf_pages(cache_logf_flat, page_table, group):
    depth, _, _, width = cache_logf_flat.shape
    nb, n_pages = page_table.shape
    ng = n_pages // group

    def in_map(r):
        return lambda b, g, pt: (0, pt[b, g * group + r], 0, 0)

    return pl.pallas_call(
        _gather_rows_kernel,
        grid_spec=pltpu.PrefetchScalarGridSpec(
            num_scalar_prefetch=1,
            grid=(nb, ng),
            in_specs=[pl.BlockSpec((depth, None, 1, width), in_map(r)) for r in range(group)],
            out_specs=pl.BlockSpec((depth, None, group, width), lambda b, g, pt: (0, b, g, 0)),
        ),
        out_shape=jax.ShapeDtypeStruct((depth, nb, n_pages, width), F32),
        compiler_params=_params("arbitrary", "arbitrary"),
        name="gather_logf",
    )(page_table, *([cache_logf_flat] * group))


def _past_cumsum_kernel(x_ref, t3_ref, s3_ref, l3_ref, e3_ref, c_ref, tot_ref):
    x3 = jnp.concatenate(_split3(x_ref[...]), axis=1)
    y = _dot(x3, t3_ref[...])
    tot = _dot(x3, s3_ref[...])
    offs = _dot(l3_ref[...], jnp.concatenate(_split3(tot), axis=0))
    c_ref[...] = y + _dot(jnp.concatenate(_split3(offs), axis=1), e3_ref[...])
    n = tot.shape[0]
    tot_ref[...] = offs[n - 1:n, :] + tot[n - 1:n, :]


def _past_cumsum(lf_pages, heads):
    depth, nb, n_pages, width = lf_pages.shape
    pos = np.arange(width) // heads
    hd = np.arange(width) % heads
    t = ((hd[:, None] == hd[None, :]) & (pos[:, None] <= pos[None, :])).astype(np.float32)
    ssum = (hd[:, None] == np.arange(LANES)[None, :]).astype(np.float32)
    lstrict = np.tril(np.ones((n_pages, n_pages), np.float32), -1)
    e = (np.arange(LANES)[:, None] == hd[None, :]).astype(np.float32)
    t3 = jnp.asarray(np.concatenate([t, t, t], axis=0), BF16)
    s3 = jnp.asarray(np.concatenate([ssum, ssum, ssum], axis=0), BF16)
    l3 = jnp.asarray(np.concatenate([lstrict, lstrict, lstrict], axis=1), BF16)
    e3 = jnp.asarray(np.concatenate([e, e, e], axis=0), BF16)
    x = lf_pages.reshape(depth * nb, n_pages, width)
    full = lambda a: pl.BlockSpec(a.shape, lambda g: (0,) * a.ndim)
    return pl.pallas_call(
        _past_cumsum_kernel,
        grid=(depth * nb,),
        in_specs=[pl.BlockSpec((None, n_pages, width), lambda g: (g, 0, 0)),
                  full(t3), full(s3), full(l3), full(e3)],
        out_specs=[pl.BlockSpec((None, n_pages, width), lambda g: (g, 0, 0)),
                   pl.BlockSpec((None, 1, LANES), lambda g: (g, 0, 0))],
        out_shape=[jax.ShapeDtypeStruct((depth * nb, n_pages, width), F32),
                   jax.ShapeDtypeStruct((depth * nb, 1, LANES), F32)],
        compiler_params=_params("arbitrary"),
        name="past_cumsum",
    )(x, t3, s3, l3, e3)


def _fox_sample_kernel(pt_ref, *refs, group, heads, hd, nt, aliased):
    if aliased:
        refs = refs[2:]
    q_ref, kn_ref, vn_ref, ff_ref, b_ref, ctot_ref, cp_ref = refs[:7]
    refs = refs[7:]
    k_refs = refs[:group]
    v_refs = refs[group:2 * group]
    o_ref, lf_ref, ks_ref, vs_ref = refs[2 * group:2 * group + 4]
    qall_ref, cqm_ref, m_ref, l_ref, acc_ref = refs[2 * group + 4:]
    g = pl.program_id(1)
    ng = pl.num_programs(1)
    nrow = heads * nt
    prow = k_refs[0].shape[0]
    assert heads & (heads - 1) == 0 and nt & (nt - 1) == 0
    hbits = heads.bit_length() - 1
    tbits = nt.bit_length() - 1

    def head_of_col(cols):
        return jnp.bitwise_and(cols, heads - 1)

    @pl.when(g == 0)
    def _():
        lf = _log_sigmoid(ff_ref[...] + b_ref[...])
        lf_ref[...] = lf
        trow = lax.broadcasted_iota(jnp.int32, lf.shape, 0)
        tlane = lax.broadcasted_iota(jnp.int32, lf.shape, 1)
        cs = lf
        sh = 1
        while sh < nt:
            cs = cs + jnp.where(trow >= sh, pltpu.roll(cs, sh, 0), 0.0)
            sh *= 2
        cn = ctot_ref[...] + cs
        cq = jnp.concatenate([cn[:, h:h + 1] for h in range(heads)], axis=0)
        prows = lax.broadcasted_iota(jnp.int32, (nrow, prow), 0)
        pcols = lax.broadcasted_iota(jnp.int32, (nrow, prow), 1)
        cqm_ref[...] = jnp.where(head_of_col(pcols) == jnp.right_shift(prows, tbits), cq, -jnp.inf)
        qf = q_ref[...]
        qall = jnp.concatenate([qf[:, h * hd:(h + 1) * hd] for h in range(heads)], axis=0)
        qall_ref[...] = qall.astype(BF16)
        cnm = jnp.where(tlane < heads, cn, 0.0)
        ck = jnp.zeros((1, LANES), F32)
        for t in range(nt):
            rolled = pltpu.roll(cnm, t * heads, 1)
            ck = ck + jnp.sum(jnp.where(trow == t, rolled, 0.0), axis=0, keepdims=True)
        for h in range(heads):
            ks_ref[pl.ds(h, nt, stride=heads), :] = kn_ref[:, h * hd:(h + 1) * hd]
            vs_ref[pl.ds(h, nt, stride=heads), :] = vn_ref[:, h * hd:(h + 1) * hd]
        kn = _pad_rows(ks_ref[...], LANES).astype(BF16)
        vn = _pad_rows(vs_ref[...], LANES).astype(BF16)
        s = _dot_nt(qall_ref[...], kn) + cq - ck
        rows = lax.broadcasted_iota(jnp.int32, s.shape, 0)
        cols = lax.broadcasted_iota(jnp.int32, s.shape, 1)
        ok = (head_of_col(cols) == jnp.right_shift(rows, tbits))
        ok = ok & (jnp.right_shift(cols, hbits) <= jnp.bitwise_and(rows, nt - 1)) & (cols < nrow)
        s = jnp.where(ok, s, -jnp.inf)
        m0 = jnp.max(s, axis=-1, keepdims=True)
        p = jnp.exp(s - m0)
        m_ref[...] = m0
        l_ref[...] = jnp.sum(p, axis=-1, keepdims=True)
        acc_ref[...] = _dot(p.astype(BF16), vn)

    qall = qall_ref[...]
    cqm = cqm_ref[...]
    ss = []
    for r in range(group):
        kb = k_refs[r][...].astype(BF16)
        ss.append(_dot_nt(qall, kb) + (cqm - cp_ref[r:r + 1, :]))
    mx = ss[0]
    for s in ss[1:]:
        mx = jnp.maximum(mx, s)
    m_prev = m_ref[...]
    m_new = jnp.maximum(m_prev, jnp.max(mx, axis=-1, keepdims=True))
    alpha = jnp.exp(m_prev - m_new)
    acc = alpha * acc_ref[...]
    psum = jnp.zeros((nrow, prow), F32)
    for r in range(group):
        p = jnp.exp(ss[r] - m_new)
        psum = psum + p
        acc = acc + _dot(p.astype(BF16), v_refs[r][...].astype(BF16))
    l_ref[...] = alpha * l_ref[...] + jnp.sum(psum, axis=-1, keepdims=True)
    acc_ref[...] = acc
    m_ref[...] = m_new

    @pl.when(g == ng - 1)
    def _():
        o = acc_ref[...] / l_ref[...]
        for h in range(heads):
            o_ref[:, h * hd:(h + 1) * hd] = o[h * nt:(h + 1) * nt, :]


def _fox_sample(zf, ff, b_fox_pad, ctot, c_past, cache_k, cache_v, page_table, layer, depth,
                row0, nt, heads, hd, group, qcol, prev):
    nb, n_pages = page_table.shape
    prow = cache_k.shape[2]
    width = heads * hd
    ng = n_pages // group
    rb0 = row0 // nt
    nrow = heads * nt
    aliased = prev is not None

    def page_map(r):
        return lambda b, g, pt: (layer, pt[b, g * group + r], 0, 0)

    kern = functools.partial(_fox_sample_kernel, group=group, heads=heads, hd=hd, nt=nt, aliased=aliased)
    new_rows = pl.BlockSpec((None, nrow, hd), lambda b, g, pt: (layer, b, 0))
    new_shape = jax.ShapeDtypeStruct((depth, nb * nrow, hd), F32)
    in_specs = ([_any_spec(), _any_spec()] if aliased else []) + [
        pl.BlockSpec((nt, width), lambda b, g, pt: (rb0 + b, qcol)),
        pl.BlockSpec((nt, width), lambda b, g, pt: (rb0 + b, qcol + 1)),
        pl.BlockSpec((nt, width), lambda b, g, pt: (rb0 + b, qcol + 2)),
        pl.BlockSpec((nt, LANES), lambda b, g, pt: (rb0 + b, 0)),
        pl.BlockSpec((None, 1, LANES), lambda b, g, pt: (layer, 0, 0)),
        pl.BlockSpec((None, 1, LANES), lambda b, g, pt: (layer * nb + b, 0, 0)),
        pl.BlockSpec((None, group, c_past.shape[2]), lambda b, g, pt: (layer * nb + b, g, 0)),
    ]
    in_specs += [pl.BlockSpec((None, None, prow, hd), page_map(r)) for r in range(group)]
    in_specs += [pl.BlockSpec((None, None, prow, hd), page_map(r)) for r in range(group)]
    return pl.pallas_call(
        kern,
        grid_spec=pltpu.PrefetchScalarGridSpec(
            num_scalar_prefetch=1,
            grid=(nb, ng),
            in_specs=in_specs,
            out_specs=[pl.BlockSpec((nt, width), lambda b, g, pt: (b, 0)),
                       pl.BlockSpec((None, nt, LANES), lambda b, g, pt: (b, 0, 0)),
                       new_rows, new_rows],
            scratch_shapes=[
                pltpu.VMEM((nrow, hd), BF16),
                pltpu.VMEM((nrow, prow), F32),
                pltpu.VMEM((nrow, 1), F32),
                pltpu.VMEM((nrow, 1), F32),
                pltpu.VMEM((nrow, hd), F32),
            ],
        ),
        out_shape=[jax.ShapeDtypeStruct((nb * nt, width), F32),
                   jax.ShapeDtypeStruct((nb, nt, LANES), F32),
                   new_shape, new_shape],
        input_output_aliases={1: 2, 2: 3} if aliased else {},
        compiler_params=_params("arbitrary", "arbitrary"),
        name="fox_sample",
    )(page_table, *(prev if aliased else ()), zf, zf, zf, ff, b_fox_pad, ctot, c_past,
      *([cache_k] * group), *([cache_v] * group))


def _hgrn_operators(c, real):
    t = np.arange(c)[:, None]
    j = np.arange(c)[None, :]
    mats = [(j <= t), (j > t)]
    masks = []
    m = 1
    while m < real:
        blk = t // (2 * m)
        mid = blk * (2 * m) + m - 1
        second = t > mid
        if m < SUBLANES:
            mats.append(np.where(second, (j > mid) & (j <= t), (j > t) & (j <= mid)))
        s = np.arange(c)[None, :]
        masks.append((blk == (s // (2 * m))) & second & (s <= (s // (2 * m)) * (2 * m) + m - 1))
        m *= 2
    a = np.concatenate(mats, axis=0).astype(np.float32)
    a3 = np.concatenate([a, a, a], axis=1)
    return jnp.asarray(a3, BF16), jnp.asarray(np.stack(masks).astype(np.float32)), len(masks)


def _hgrn_kernel(q_ref, f_ref, i_ref, g_ref, lbraw_ref, gn_ref, a3_ref, msk_ref, s0_ref,
                 o_ref, sfin_ref, st_ref, *, layer, heads, dk, n_lev, real):
    c = HGRN_CHUNK
    ci = pl.program_id(1)
    nc = pl.num_programs(1)

    @pl.when(ci == 0)
    def _():
        for h in range(heads):
            st_ref[h] = s0_ref[h].T

    raw = lbraw_ref[...]
    e = jnp.exp(raw - jnp.max(raw, axis=0, keepdims=True))
    sm = e / jnp.sum(e, axis=0, keepdims=True)
    cs = sm[0:1]
    for r in range(1, layer + 1):
        cs = cs + sm[r:r + 1]
    lb = cs - sm[0:1]
    log_lb = jnp.log(lb)
    log_1m = jnp.log1p(-lb)

    zf = _pad_rows(f_ref[...], c)
    q = _pad_rows(q_ref[...], c)
    v = _pad_rows(i_ref[...], c)
    b = log_1m + _log_sigmoid(zf)
    mx = jnp.maximum(log_lb, b)
    g = mx + jnp.log(1.0 + jnp.exp(-jnp.abs(log_lb - b)))
    kr = (1.0 - lb) * _sigmoid(-zf)
    if real < c:
        valid = lax.broadcasted_iota(jnp.int32, g.shape, 0) < real
        g = jnp.where(valid, g, 0.0)
        kr = jnp.where(valid, kr, 0.0)

    e = _dot(a3_ref[...], jnp.concatenate(_split3(g), axis=0))
    eq = e[0:c]
    xq = jnp.exp(eq)
    xk = jnp.exp(e[c:2 * c])
    xlast = xq[c - 1:c]
    n_mx = e.shape[0] // c - 2

    def level_factor(l):
        if l < n_mx:
            return jnp.exp(e[(2 + l) * c:(3 + l) * c])
        m = 1 << l
        mids = [jnp.broadcast_to(eq[b0 + m - 1:b0 + m, :], (2 * m, eq.shape[1]))
                for b0 in range(0, c, 2 * m)]
        r = mids[0] if len(mids) == 1 else jnp.concatenate(mids, axis=0)
        return jnp.exp(-jnp.abs(eq - r))

    xls = [level_factor(l) for l in range(n_lev)]
    qe = (q * xq).astype(BF16)
    ke = (kr * xk).astype(BF16)
    vb = v.astype(BF16)
    for h in range(heads):
        hs = slice(h * dk, (h + 1) * dk)
        st = st_ref[h]
        o = _dot_nt(qe[:, hs], st.astype(BF16))
        att = jnp.zeros((c, c), F32)
        for l in range(n_lev):
            xl = xls[l][:, hs]
            p = _dot_nt((q[:, hs] * xl).astype(BF16), (kr[:, hs] * xl).astype(BF16))
            att = att + p * msk_ref[l]
        o = o + _dot(att.astype(BF16), vb[:, hs])
        o = o + jnp.sum(q[:, hs] * kr[:, hs], axis=-1, keepdims=True) * v[:, hs]
        st_ref[h] = st * xlast[:, hs] + _dot_tn(vb[:, hs], ke[:, hs])
        on = _rms_rows(o, gn_ref[:, hs])
        gg = g_ref[:, hs]
        o_ref[:, hs] = (on[:real] * (gg * _sigmoid(gg))).astype(o_ref.dtype)

    @pl.when(ci == nc - 1)
    def _():
        for h in range(heads):
            sfin_ref[h] = st_ref[h].T


def _hgrn(zf, hgrn_lb, hgrn_norm, s0, layer, row0, n_seq, seq, heads, dk, col0, out_rows, out_dtype):
    c = HGRN_CHUNK
    real = min(seq, c)
    assert seq % real == 0
    nc = seq // real
    width = heads * dk
    rb0 = row0 // real
    a3, masks, n_lev = _hgrn_operators(c, real)
    depth = hgrn_lb.shape[0]
    blk = lambda col: pl.BlockSpec((real, width), lambda n, i: (rb0 + n * nc + i, col))
    kern = functools.partial(_hgrn_kernel, layer=layer, heads=heads, dk=dk, n_lev=n_lev, real=real)
    in_specs = [
        blk(col0), blk(col0 + 1), blk(col0 + 2), blk(col0 + 3),
        pl.BlockSpec((depth, width), lambda n, i: (0, 0)),
        pl.BlockSpec((None, 1, width), lambda n, i: (layer, 0, 0)),
        pl.BlockSpec(a3.shape, lambda n, i: (0, 0)),
        pl.BlockSpec(masks.shape, lambda n, i: (0, 0, 0)),
        pl.BlockSpec((None, heads, dk, dk), lambda n, i: (n, 0, 0, 0)),
    ]
    return pl.pallas_call(
        kern,
        grid=(n_seq, nc),
        in_specs=in_specs,
        out_specs=[
            pl.BlockSpec((real, width), lambda n, i: (n * nc + i, 0)),
            pl.BlockSpec((None, heads, dk, dk), lambda n, i: (n, 0, 0, 0)),
        ],
        out_shape=[
            jax.ShapeDtypeStruct((out_rows, width), out_dtype),
            jax.ShapeDtypeStruct((n_seq, heads, dk, dk), F32),
        ],
        scratch_shapes=[pltpu.VMEM((heads, dk, dk), F32)],
        compiler_params=_params("arbitrary", "arbitrary"),
        name="hgrn",
    )(zf, zf, zf, zf, hgrn_lb, hgrn_norm, a3, masks, s0)


def _patch_rows_kernel(*refs):
    n = len(refs) // 3
    for small_ref, out_ref in zip(refs[n:2 * n], refs[2 * n:]):
        out_ref[...] = small_ref[...].astype(out_ref.dtype)


def _patch_rows(bigs, smalls, row0):
    rows = smalls[0].shape[0]
    assert row0 % rows == 0
    n = len(bigs)
    return pl.pallas_call(
        _patch_rows_kernel,
        grid=(1,),
        in_specs=[_any_spec()] * n + [pl.BlockSpec(s.shape, lambda i: (0, 0)) for s in smalls],
        out_specs=[pl.BlockSpec((rows, b.shape[1]), lambda i: (row0 // rows, 0)) for b in bigs],
        out_shape=[jax.ShapeDtypeStruct(b.shape, b.dtype) for b in bigs],
        input_output_aliases={i: i for i in range(n)},
        compiler_params=_params("arbitrary"),
        name="patch_rows",
    )(*bigs, *smalls)


def _merge_kernel(ma_ref, ob_ref, oc_ref, wa_ref, wb_ref, wc_ref, ga_ref, gb_ref, gc_ref, mix_ref):
    ya = _dot(ma_ref[...], wa_ref[...])
    yb = _dot(ob_ref[...], wb_ref[...])
    yc = _dot(oc_ref[...], wc_ref[...])
    gate = lambda r: _sigmoid(r[...].astype(F32))
    mix = gate(ga_ref) * ya + gate(gb_ref) * yb + gate(gc_ref) * yc
    mix_ref[...] = mix.astype(mix_ref.dtype)


def _merge(m_a, o_b, o_c, w_a, w_b, w_c, zg, layer, tm, tn):
    m, k = m_a.shape
    d = w_a.shape[2]
    nj = d // tn
    act = pl.BlockSpec((tm, k), lambda i, j: (i, 0))
    wsp = pl.BlockSpec((None, k, tn), lambda i, j: (layer, 0, j))
    gate = lambda r: pl.BlockSpec((tm, tn), lambda i, j: (i, r * nj + j))
    return pl.pallas_call(
        _merge_kernel,
        grid=(m // tm, nj),
        in_specs=[act, act, act, wsp, wsp, wsp, gate(0), gate(1), gate(2)],
        out_specs=pl.BlockSpec((tm, tn), lambda i, j: (i, j)),
        out_shape=jax.ShapeDtypeStruct((m, d), BF16),
        compiler_params=_params("arbitrary", "arbitrary"),
        name="merge",
    )(m_a, o_b, o_c, w_a, w_b, w_c, zg, zg, zg)


def _residual_matmul_kernel(a_ref, w_ref, x_ref, o_ref):
    o_ref[...] = x_ref[...] + _dot(a_ref[...], w_ref[...])


def _weight_spec(shape, index_map, resident):
    if resident:
        return pl.BlockSpec(shape, index_map, pipeline_mode=pl.Buffered(1))
    return pl.BlockSpec(shape, index_map)


def _residual_matmul(a, w, x, layer, tm, tn, single_buffer_a=False):
    m, k = a.shape
    d = w.shape[2]
    return pl.pallas_call(
        _residual_matmul_kernel,
        grid=(m // tm, d // tn),
        in_specs=[
            _weight_spec((tm, k), lambda i, j: (i, 0), single_buffer_a),
            _weight_spec((None, k, tn), lambda i, j: (layer, 0, j), tn == d),
            pl.BlockSpec((tm, tn), lambda i, j: (i, j)),
        ],
        out_specs=pl.BlockSpec((tm, tn), lambda i, j: (i, j)),
        out_shape=jax.ShapeDtypeStruct((m, d), F32),
        compiler_params=_params("arbitrary", "arbitrary"),
        name="residual_matmul",
    )(a, w, x)


def _swiglu_up_kernel(x_ref, g_ref, wg_ref, wu_ref, a_ref, h_ref):
    @pl.when(pl.program_id(1) == 0)
    def _():
        h_ref[...] = _rms_rows(x_ref[...], g_ref[...]).astype(BF16)

    h = h_ref[...]
    gate = _dot(h, wg_ref[...])
    up = _dot(h, wu_ref[...])
    a_ref[...] = (gate * _sigmoid(gate) * up).astype(a_ref.dtype)


def _swiglu_up(x, norm, w_gate_up, layer, tm, tn):
    m, d = x.shape
    f = w_gate_up.shape[2] // 2
    nj = f // tn
    return pl.pallas_call(
        _swiglu_up_kernel,
        grid=(m // tm, nj),
        in_specs=[
            pl.BlockSpec((tm, d), lambda i, j: (i, 0)),
            pl.BlockSpec((None, 1, d), lambda i, j: (layer, 0, 0)),
            pl.BlockSpec((None, d, tn), lambda i, j: (layer, 0, j)),
            pl.BlockSpec((None, d, tn), lambda i, j: (layer, 0, nj + j)),
        ],
        out_specs=pl.BlockSpec((tm, tn), lambda i, j: (i, j)),
        out_shape=jax.ShapeDtypeStruct((m, f), BF16),
        scratch_shapes=[pltpu.VMEM((tm, d), BF16)],
        compiler_params=_params("arbitrary", "arbitrary"),
        name="swiglu_up",
    )(x, norm, w_gate_up, w_gate_up)


def _ple_kernel(x_ref, g_ref, wg_ref, p_ref, wp_ref, o_ref, h_ref, *, tn):
    j = pl.program_id(1)

    @pl.when(j == 0)
    def _():
        h_ref[...] = _rms_rows(x_ref[...], g_ref[...]).astype(BF16)

    gate = _sigmoid(_dot(h_ref[...], wg_ref[...]))
    emb = _dot(p_ref[...].astype(BF16), wp_ref[...])
    xs = x_ref[:, pl.ds(pl.multiple_of(j * tn, tn), tn)]
    o_ref[...] = xs + gate * emb


def _ple(x, norm, w_gate, p, w_proj, layer, tm, tn):
    m, d = x.shape
    pd = p.shape[2]
    return pl.pallas_call(
        functools.partial(_ple_kernel, tn=tn),
        grid=(m // tm, d // tn),
        in_specs=[
            pl.BlockSpec((tm, d), lambda i, j: (i, 0)),
            pl.BlockSpec((None, 1, d), lambda i, j: (layer, 0, 0)),
            _weight_spec((None, d, tn), lambda i, j: (layer, 0, j), tn == d),
            pl.BlockSpec((None, tm, pd), lambda i, j: (layer, i, 0)),
            _weight_spec((None, pd, tn), lambda i, j: (layer, 0, j), tn == d),
        ],
        out_specs=pl.BlockSpec((tm, tn), lambda i, j: (i, j)),
        out_shape=jax.ShapeDtypeStruct((m, d), F32),
        scratch_shapes=[pltpu.VMEM((tm, d), BF16)],
        compiler_params=_params("arbitrary", "arbitrary"),
        name="ple",
    )(x, norm, w_gate, p, w_proj)


def _final_norm_kernel(x_ref, g_ref, o_ref):
    o_ref[...] = _rms_rows(x_ref[...], g_ref[...])


def _final_norm(x, g, row0, rows, tile):
    d = x.shape[1]
    rb0 = row0 // tile
    return pl.pallas_call(
        _final_norm_kernel,
        grid=(rows // tile,),
        in_specs=[pl.BlockSpec((tile, d), lambda i: (rb0 + i, 0)), pl.BlockSpec((1, d), lambda i: (0, 0))],
        out_specs=pl.BlockSpec((tile, d), lambda i: (i, 0)),
        out_shape=jax.ShapeDtypeStruct((rows, d), F32),
        compiler_params=_params("arbitrary"),
        name="final_norm",
    )(x, g)


def kernel(x_prompt, x_sample, cache_k, cache_v, cache_logf, state_conv, state_hgrn, page_table,
           p_prompt, p_sample, norm1, w_in, b_fox, conv_w, hgrn_lb, hgrn_norm, w_a, w_b, w_c, w_o,
           norm2, w_gate_up, w_down, ple_norm, ple_gate, ple_proj, final_norm):
    bp, seq, d = x_prompt.shape
    db, dseq, _ = x_sample.shape
    depth = w_in.shape[0]
    heads, hd = cache_k.shape[3], cache_k.shape[4]
    page = cache_k.shape[2]
    n_pool = cache_k.shape[1]
    cw = w_a.shape[1]
    fw = w_b.shape[1]
    rw = w_c.shape[1]
    rheads = state_hgrn.shape[2]
    dk = state_hgrn.shape[3]
    assert fw == heads * hd and rw == rheads * dk and state_hgrn.shape[4] == dk and cw == fw == rw
    mp = bp * seq
    ms = db * dseq
    m = mp + ms

    o_ff = 3 * cw + 3 * fw
    widths = (3 * cw, fw, 2 * fw, 4 * rw, 3 * d)
    w_main = jnp.concatenate([w_in[:, :, :o_ff], w_in[:, :, o_ff + heads:]], axis=2).astype(BF16)
    w_ff = jnp.pad(w_in[:, :, o_ff:o_ff + heads], ((0, 0), (0, 0), (0, LANES - heads))).astype(BF16)
    assert w_main.shape[2] == sum(widths)
    tn_in = fw
    qcol = 3 * cw // tn_in
    rcol = qcol + 3

    bf = lambda a: a.astype(BF16)
    w_a_b, w_b_b, w_c_b, w_o_b = bf(w_a), bf(w_b), bf(w_c), bf(w_o)
    w_gu_b, w_dn_b, w_pg_b, w_pp_b = bf(w_gate_up), bf(w_down), bf(ple_gate), bf(ple_proj)
    row3 = lambda a: a.reshape(a.shape[0], 1, a.shape[1])
    norm1_r, norm2_r, ple_norm_r, hnorm_r = row3(norm1), row3(norm2), row3(ple_norm), row3(hgrn_norm)
    b_fox_pad = row3(jnp.pad(b_fox, ((0, 0), (0, LANES - heads))))

    x = _stack_rows(x_prompt.reshape(1, mp, d), x_sample.reshape(1, ms, d)).reshape(m, d)
    p_all = _stack_rows(p_prompt.reshape(depth, mp, -1), p_sample.reshape(depth, ms, -1))
    ck = cache_k.reshape(depth, n_pool, page * heads, hd)
    cv = cache_v.reshape(depth, n_pool, page * heads, hd)

    n_pages = page_table.shape[1]
    group = _tile(n_pages, 16, 1)
    lf_flat = cache_logf.reshape(depth, n_pool, 1, page * heads)
    lf_pages = _gather_logf_pages(lf_flat, page_table, group)
    c_past, c_tot = _past_cumsum(lf_pages, heads)

    tm = _tile(m, 704, 16)
    tm_wide = _tile(m, 1408, 16)
    tq =_tile(seq, 512, LANES)
    conv_tile = _tile(seq, 512, 8)
    zeros_conv = jnp.zeros((bp, 2, cw), F32)
    zeros_state = jnp.zeros((bp, rheads, dk, dk), F32)

    outs = {k: [] for k in ("lfp", "cp", "sp", "lfs", "cs", "ss")}
    kvp = None
    kvs = None
    for i in range(depth):
        zf, kvb, zg, ff = _in_proj(x, norm1_r, w_main, w_ff, i, tm, tn_in, widths, hd)

        m_a, conv_p = _conv(zf, conv_w, zeros_conv, i, 0, bp, seq, conv_tile, cw, m, BF16)
        ma_s, conv_s = _conv(zf, conv_w, state_conv[i], i, mp, db, dseq, dseq, cw, ms, F32)

        lf_p, qaug_t, kaug, v_t, kp, vp = _logf_prompt(ff, b_fox_pad, zf, kvb, i, depth, bp, seq,
                                                       heads, hd, qcol, kvp)
        kvp = (kp, vp)
        o_b = _fox_prompt(qaug_t, kaug, v_t, m, bp, seq, heads, hd, tq)
        ob_s, lf_s, ks, vs = _fox_sample(zf, ff, b_fox_pad, c_tot, c_past, ck, cv, page_table, i,
                                         depth, mp, dseq, heads, hd, group, qcol, kvs)
        kvs = (ks, vs)

        o_c, st_p = _hgrn(zf, hgrn_lb, hnorm_r, zeros_state, i, 0, bp, seq, rheads, dk, rcol, m, BF16)
        oc_s, st_s = _hgrn(zf, hgrn_lb, hnorm_r, state_hgrn[i], i, mp, db, dseq, rheads, dk, rcol,
                           ms, F32)

        m_a, o_b, o_c = _patch_rows((m_a, o_b, o_c), (ma_s, ob_s, oc_s), mp)
        mix = _merge(m_a, o_b, o_c, w_a_b, w_b_b, w_c_b, zg, i, tm, 1024)
        x = _residual_matmul(mix, w_o_b, x, i, tm, d)

        act = _swiglu_up(x, norm2_r, w_gu_b, i, tm, _tile(w_gate_up.shape[2] // 2, 512, 2 * LANES))
        x = _residual_matmul(act, w_dn_b, x, i, tm_wide, 512, single_buffer_a=True)
        x = _ple(x, ple_norm_r, w_pg_b, p_all, w_pp_b, i, tm, d)

        outs["lfp"].append(lf_p[:, :, :heads])
        outs["lfs"].append(lf_s[:, :, :heads])
        outs["cp"].append(conv_p)
        outs["cs"].append(conv_s)
        outs["sp"].append(st_p)
        outs["ss"].append(st_s)

    fn = final_norm.reshape(1, d)
    y_p = _final_norm(x, fn, 0, mp, _tile(mp, 1024, 8))
    y_s = _final_norm(x, fn, mp, ms, ms)
    st = lambda k: jnp.stack(outs[k])
    return (y_p.reshape(bp, seq, d), y_s.reshape(db, dseq, d),
            kp.reshape(depth, bp, seq, heads, hd), vp.reshape(depth, bp, seq, heads, hd),
            st("lfp"), st("cp"), st("sp"),
            ks.reshape(depth, db, dseq, heads, hd), vs.reshape(depth, db, dseq, heads, hd),
            st("lfs"), st("cs"), st("ss"))
```

```python
import functools

import numpy as np
import jax
import jax.numpy as jnp
from jax import lax
from jax.experimental import pallas as pl
from jax.experimental.pallas import tpu as pltpu

F32 = jnp.float32
BF16 = jnp.bfloat16
RMS_EPS = 1e-6
LANES = 128
SUBLANES = 8
HGRN_CHUNK = 128
VMEM_LIMIT = 56 * 1024 * 1024


def _tile(n, target, align):
    best = None
    for t in range(align, min(n, target) + 1, align):
        if n % t == 0:
            best = t
    assert best is not None, (n, target, align)
    return best


def _params(*sem):
    return pltpu.CompilerParams(dimension_semantics=sem, vmem_limit_bytes=VMEM_LIMIT)


def _sigmoid(x):
    return 1.0 / (1.0 + jnp.exp(-x))


def _log_sigmoid(x):
    return jnp.minimum(x, 0.0) - jnp.log(1.0 + jnp.exp(-jnp.abs(x)))


def _rms_rows(x, g):
    ms = jnp.mean(x * x, axis=-1, keepdims=True)
    return x * lax.rsqrt(ms + RMS_EPS) * g


def _split3(x):
    hi = x.astype(BF16)
    r1 = x - hi.astype(F32)
    mid = r1.astype(BF16)
    lo = (r1 - mid.astype(F32)).astype(BF16)
    return hi, mid, lo


def _dot(a, b):
    return jnp.dot(a, b, preferred_element_type=F32)


def _dot_nt(a, b):
    return lax.dot_general(a, b, (((1,), (1,)), ((), ())), preferred_element_type=F32)


def _dot_tn(a, b):
    return lax.dot_general(a, b, (((0,), (0,)), ((), ())), preferred_element_type=F32)


def _pad_rows(a, n):
    if a.shape[0] == n:
        return a
    return jnp.concatenate([a, jnp.zeros((n - a.shape[0], a.shape[1]), a.dtype)], axis=0)


def _any_spec():
    return pl.BlockSpec(memory_space=pl.ANY)


def _copy_kernel(*refs, aliased):
    a_ref, o_ref = refs[1:] if aliased else refs
    o_ref[...] = a_ref[...]


def _stack_rows(a, b):
    nl, ra, c = a.shape
    rb = b.shape[1]
    assert ra % rb == 0
    ta = _tile(ra, 1024, 8)
    shape = jax.ShapeDtypeStruct((nl, ra + rb, c), a.dtype)
    out = pl.pallas_call(
        functools.partial(_copy_kernel, aliased=False),
        grid=(nl, ra // ta),
        in_specs=[pl.BlockSpec((None, ta, c), lambda l, i: (l, i, 0))],
        out_specs=pl.BlockSpec((None, ta, c), lambda l, i: (l, i, 0)),
        out_shape=shape,
        compiler_params=_params("arbitrary", "arbitrary"),
        name="stack_rows_a",
    )(a)
    return pl.pallas_call(
        functools.partial(_copy_kernel, aliased=True),
        grid=(nl,),
        in_specs=[_any_spec(), pl.BlockSpec((None, rb, c), lambda l: (l, 0, 0))],
        out_specs=pl.BlockSpec((None, rb, c), lambda l: (l, ra // rb, 0)),
        out_shape=shape,
        input_output_aliases={0: 0},
        compiler_params=_params("arbitrary"),
        name="stack_rows_b",
    )(out, b)


def _in_proj_kernel(x_ref, g_ref, w_ref, wff_ref, zf_ref, kvb_ref, zg_ref, ff_ref, h_ref,
                    *, bounds, scale):
    j = pl.program_id(1)
    jq, jk, jr, jg = bounds

    @pl.when(j == 0)
    def _():
        h = _rms_rows(x_ref[...], g_ref[...]).astype(BF16)
        h_ref[...] = h
        ff_ref[...] = _dot(h, wff_ref[...])

    @pl.when(j < jq)
    def _():
        zf_ref[...] = _dot(h_ref[...], w_ref[...])

    @pl.when((j >= jq) & (j < jk))
    def _():
        zf_ref[...] = _dot(h_ref[...], w_ref[...]) * scale

    @pl.when((j >= jk) & (j < jr))
    def _():
        z = _dot(h_ref[...], w_ref[...])
        zf_ref[...] = z
        kvb_ref[...] = z.astype(BF16)

    @pl.when((j >= jr) & (j < jg))
    def _():
        zf_ref[...] = _dot(h_ref[...], w_ref[...])

    @pl.when(j >= jg)
    def _():
        zg_ref[...] = _dot(h_ref[...], w_ref[...]).astype(BF16)


def _in_proj(x, norm, w_main, w_ff, layer, tm, tn, widths, hd):
    m, d = x.shape
    assert all(w % tn == 0 for w in widths) and sum(widths) == w_main.shape[2]
    cuts = np.cumsum([w // tn for w in widths])
    jq, jk, jr, jg, nj = (int(c) for c in cuts)
    kern = functools.partial(_in_proj_kernel, bounds=(jq, jk, jr, jg), scale=hd ** -0.5)
    return pl.pallas_call(
        kern,
        grid=(m // tm, nj),
        in_specs=[
            pl.BlockSpec((tm, d), lambda i, j: (i, 0)),
            pl.BlockSpec((None, 1, d), lambda i, j: (layer, 0, 0)),
            pl.BlockSpec((None, d, tn), lambda i, j: (layer, 0, j)),
            pl.BlockSpec((None, d, LANES), lambda i, j: (layer, 0, 0)),
        ],
        out_specs=[
            pl.BlockSpec((tm, tn), lambda i, j: (i, jnp.minimum(j, jg - 1))),
            pl.BlockSpec((tm, tn), lambda i, j: (i, jnp.clip(j - jk, 0, jr - jk - 1))),
            pl.BlockSpec((tm, tn), lambda i, j: (i, jnp.clip(j - jg, 0, nj - jg - 1))),
            pl.BlockSpec((tm, LANES), lambda i, j: (i, 0)),
        ],
        out_shape=[
            jax.ShapeDtypeStruct((m, jg * tn), F32),
            jax.ShapeDtypeStruct((m, widths[2]), BF16),
            jax.ShapeDtypeStruct((m, widths[4]), BF16),
            jax.ShapeDtypeStruct((m, LANES), F32),
        ],
        scratch_shapes=[pltpu.VMEM((tm, d), BF16)],
        compiler_params=_params("arbitrary", "arbitrary"),
        name="in_proj",
    )(x, norm, w_main, w_ff)


def _bias_operators(heads, width):
    hd = width // heads
    selq = np.zeros((3 * LANES, width), np.float32)
    selk = np.zeros((3 * LANES, width), np.float32)
    cq = np.zeros((1, width), np.float32)
    ck = np.zeros((1, width), np.float32)
    for h in range(heads):
        for r in range(3):
            selq[r * LANES + h, h * hd + r] = 1.0
            selk[r * LANES + h, h * hd + 3 + r] = -1.0
            cq[0, h * hd + 3 + r] = 1.0
            ck[0, h * hd + r] = 1.0
    return jnp.asarray(selq, BF16), jnp.asarray(selk, BF16), jnp.asarray(cq), jnp.asarray(ck)


def _logf_prompt_kernel(*refs, heads, hd, aliased):
    if aliased:
        refs = refs[2:]
    (ff_ref, b_ref, q_ref, k_ref, k32_ref, v32_ref, tri_ref, selq_ref, selk_ref, cq_ref, ck_ref,
     lf_ref, qat_ref, ka_ref, vt_ref, kp_ref, vp_ref, carry_ref) = refs
    blk = ff_ref.shape[0]
    for h in range(heads):
        kp_ref[pl.ds(h, blk, stride=heads), :] = k32_ref[:, h * hd:(h + 1) * hd]
        vp_ref[pl.ds(h, blk, stride=heads), :] = v32_ref[:, h * hd:(h + 1) * hd]

    @pl.when(pl.program_id(1) == 0)
    def _():
        carry_ref[...] = jnp.zeros(carry_ref.shape, F32)

    lf = _log_sigmoid(ff_ref[...] + b_ref[...])
    lf_ref[...] = lf
    cb = _dot(tri_ref[...], jnp.concatenate(_split3(lf), axis=0)) + carry_ref[...]
    carry_ref[...] = cb[blk - 1:blk, :]
    c3 = jnp.concatenate(_split3(cb), axis=1)
    qb = _dot(c3, selq_ref[...]) + cq_ref[...]
    kb = (_dot(c3, selk_ref[...]) + ck_ref[...]).astype(BF16)
    k = k_ref[...]
    for h in range(heads):
        hs = slice(h * hd, (h + 1) * hd)
        qat_ref[2 * h * hd:(2 * h + 1) * hd, :] = q_ref[:, hs].T.astype(BF16)
        qat_ref[(2 * h + 1) * hd:(2 * h + 2) * hd, :] = qb[:, hs].T.astype(BF16)
        vt_ref[hs, :] = v32_ref[:, hs].T.astype(BF16)
        ka_ref[:, 2 * h * hd:(2 * h + 1) * hd] = k[:, hs]
        ka_ref[:, (2 * h + 1) * hd:(2 * h + 2) * hd] = kb[:, hs]


def _logf_prompt(ff, b_fox_pad, zf, kvb, layer, depth, batch, seq, heads, hd, qcol, prev):
    width = heads * hd
    blk = _tile(seq, 512, LANES)
    nb = seq // blk
    tri = np.tril(np.ones((blk, blk), np.float32))
    tri3 = jnp.asarray(np.concatenate([tri, tri, tri], axis=1), BF16)
    ops = _bias_operators(heads, width)
    full = lambda a: pl.BlockSpec(a.shape, lambda b, i: (0,) * a.ndim)
    rows = lambda c, w: pl.BlockSpec((blk, w), lambda b, i: (b * nb + i, c))
    cache = pl.BlockSpec((None, blk * heads, hd), lambda b, i: (layer, b * nb + i, 0))
    aliased = prev is not None
    in_specs = [
        rows(0, LANES),
        pl.BlockSpec((None, 1, LANES), lambda b, i: (layer, 0, 0)),
        rows(qcol, width),
        rows(0, width),
        rows(qcol + 1, width),
        rows(qcol + 2, width),
        full(tri3), full(ops[0]), full(ops[1]), full(ops[2]), full(ops[3]),
    ]
    args = [ff, b_fox_pad, zf, kvb, zf, zf, tri3, *ops]
    if aliased:
        in_specs = [_any_spec(), _any_spec()] + in_specs
        args = list(prev) + args
    cache_shape = jax.ShapeDtypeStruct((depth, batch * seq * heads, hd), F32)
    return pl.pallas_call(
        functools.partial(_logf_prompt_kernel, heads=heads, hd=hd, aliased=aliased),
        grid=(batch, nb),
        in_specs=in_specs,
        out_specs=[
            pl.BlockSpec((None, blk, LANES), lambda b, i: (b, i, 0)),
            pl.BlockSpec((2 * width, blk), lambda b, i: (0, b * nb + i)),
            rows(0, 2 * width),
            pl.BlockSpec((width, blk), lambda b, i: (0, b * nb + i)),
            cache,
            cache,
        ],
        out_shape=[
            jax.ShapeDtypeStruct((batch, seq, LANES), F32),
            jax.ShapeDtypeStruct((2 * width, batch * seq), BF16),
            jax.ShapeDtypeStruct((batch * seq, 2 * width), BF16),
            jax.ShapeDtypeStruct((width, batch * seq), BF16),
            cache_shape,
            cache_shape,
        ],
        scratch_shapes=[pltpu.VMEM((1, LANES), F32)],
        input_output_aliases={0: 4, 1: 5} if aliased else {},
        compiler_params=_params("arbitrary", "arbitrary"),
        name="logf_prompt",
    )(*args)


def _conv_kernel(ab_ref, ac_ref, av_ref, w_ref, prev_ref, m_ref, newc_ref, carry_ref):
    i = pl.program_id(1)
    t = ac_ref.shape[0]

    @pl.when(i == 0)
    def _():
        carry_ref[...] = prev_ref[...]

    u = ac_ref[...] * av_ref[...]
    row = lax.broadcasted_iota(jnp.int32, u.shape, 0)
    p1 = carry_ref[1:2, :]
    p2 = carry_ref[0:1, :]
    u1 = jnp.where(row == 0, p1, pltpu.roll(u, 1, 0))
    u2 = pltpu.roll(u, 2, 0)
    u2 = jnp.where(row == 0, p2, jnp.where(row == 1, p1, u2))
    w = w_ref[...]
    conv = w[0:1, :] * u2 + w[1:2, :] * u1 + w[2:3, :] * u
    m_ref[...] = (ab_ref[...] * conv).astype(m_ref.dtype)
    last2 = u[t - 2:t, :]
    carry_ref[...] = last2
    newc_ref[...] = last2


def _conv(zf, conv_w, prev, layer, row0, n_seq, seq, tile, width, out_rows, out_dtype):
    nt = seq // tile
    rb0 = row0 // tile
    rows = lambda c: pl.BlockSpec((tile, width), lambda n, i: (rb0 + n * nt + i, c))
    return pl.pallas_call(
        _conv_kernel,
        grid=(n_seq, nt),
        in_specs=[rows(0), rows(1), rows(2),
                  pl.BlockSpec((None, 3, width), lambda n, i: (layer, 0, 0)),
                  pl.BlockSpec((None, 2, width), lambda n, i: (n, 0, 0))],
        out_specs=[
            pl.BlockSpec((tile, width), lambda n, i: (n * nt + i, 0)),
            pl.BlockSpec((None, 2, width), lambda n, i: (n, 0, 0)),
        ],
        out_shape=[
            jax.ShapeDtypeStruct((out_rows, width), out_dtype),
            jax.ShapeDtypeStruct((n_seq, 2, width), F32),
        ],
        scratch_shapes=[pltpu.VMEM((2, width), F32)],
        compiler_params=_params("arbitrary", "arbitrary"),
        name="gated_conv",
    )(zf, zf, zf, conv_w, prev)


def _fox_prompt_kernel(qat_ref, ka_ref, vt_ref, o_ref, m_ref, l_ref, acc_ref, *, heads, hd):
    qi = pl.program_id(1)
    ki = pl.program_id(2)
    tq = qat_ref.shape[1]
    tk = ka_ref.shape[0]

    @pl.when(ki == 0)
    def _():
        m_ref[...] = jnp.full(m_ref.shape, -jnp.inf, F32)
        l_ref[...] = jnp.zeros(l_ref.shape, F32)
        acc_ref[...] = jnp.zeros(acc_ref.shape, F32)

    def step(masked):
        if masked:
            krow = lax.broadcasted_iota(jnp.int32, (tk, tq), 0)
            qcol = lax.broadcasted_iota(jnp.int32, (tk, tq), 1)
            keep = krow <= qcol
        for h in range(heads):
            aug = slice(2 * h * hd, (2 * h + 2) * hd)
            st = _dot(ka_ref[:, aug], qat_ref[aug, :])
            if masked:
                st = jnp.where(keep, st, -jnp.inf)
            m_prev = m_ref[h]
            m_new = jnp.maximum(m_prev, jnp.max(st, axis=0, keepdims=True))
            alpha = jnp.exp(m_prev - m_new)
            p = jnp.exp(st - m_new)
            l_ref[h] = alpha * l_ref[h] + jnp.sum(p, axis=0, keepdims=True)
            acc_ref[h] = alpha * acc_ref[h] + _dot(vt_ref[h * hd:(h + 1) * hd, :], p.astype(BF16))
            m_ref[h] = m_new

    @pl.when(ki < qi)
    def _():
        step(False)

    @pl.when(ki == qi)
    def _():
        step(True)
        for h in range(heads):
            o_ref[:, h * hd:(h + 1) * hd] = (acc_ref[h] / l_ref[h]).T.astype(o_ref.dtype)


def _fox_prompt(qaug_t, kaug, v_t, m_rows, batch, seq, heads, hd, tq):
    nq = seq // tq
    width = heads * hd
    kern = functools.partial(_fox_prompt_kernel, heads=heads, hd=hd)
    qmap = lambda b, qi, ki: (b * nq + qi, 0)
    kmap = lambda b, qi, ki: (b * nq + jnp.minimum(ki, qi), 0)
    return pl.pallas_call(
        kern,
        grid=(batch, nq, nq),
        in_specs=[
            pl.BlockSpec((2 * width, tq), lambda b, qi, ki: (0, b * nq + qi)),
            pl.BlockSpec((tq, 2 * width), kmap),
            pl.BlockSpec((width, tq), lambda b, qi, ki: (0, b * nq + jnp.minimum(ki, qi))),
        ],
        out_specs=pl.BlockSpec((tq, width), qmap),
        out_shape=jax.ShapeDtypeStruct((m_rows, width), BF16),
        scratch_shapes=[
            pltpu.VMEM((heads, 1, tq), F32),
            pltpu.VMEM((heads, 1, tq), F32),
            pltpu.VMEM((heads, hd, tq), F32),
        ],
        compiler_params=_params("arbitrary", "arbitrary", "arbitrary"),
        name="fox_prompt",
    )(qaug_t, kaug, v_t)


def _gather_rows_kernel(pt_ref, *refs):
    n_in = len(refs) - 1
    out_ref = refs[-1]
    for r in range(n_in):
        out_ref[:, r, :] = refs[r][:, 0, :]


def _gather_logf_pages(cache_logf_flat, page_table, group):
    depth, _, _, width = cache_logf_flat.shape
    nb, n_pages = page_table.shape
    ng = n_pages // group

    def in_map(r):
        return lambda b, g, pt: (0, pt[b, g * group + r], 0, 0)

    return pl.pallas_call(
        _gather_rows_kernel,
        grid_spec=pltpu.PrefetchScalarGridSpec(
            num_scalar_prefetch=1,
            grid=(nb, ng),
            in_specs=[pl.BlockSpec((depth, None, 1, width), in_map(r)) for r in range(group)],
            out_specs=pl.BlockSpec((depth, None, group, width), lambda b, g, pt: (0, b, g, 0)),
        ),
        out_shape=jax.ShapeDtypeStruct((depth, nb, n_pages, width), F32),
        compiler_params=_params("arbitrary", "arbitrary"),
        name="gather_logf",
    )(page_table, *([cache_logf_flat] * group))


def _past_cumsum_kernel(x_ref, t3_ref, s3_ref, l3_ref, e3_ref, c_ref, tot_ref):
    x3 = jnp.concatenate(_split3(x_ref[...]), axis=1)
    y = _dot(x3, t3_ref[...])
    tot = _dot(x3, s3_ref[...])
    offs = _dot(l3_ref[...], jnp.concatenate(_split3(tot), axis=0))
    c_ref[...] = y + _dot(jnp.concatenate(_split3(offs), axis=1), e3_ref[...])
    n = tot.shape[0]
    tot_ref[...] = offs[n - 1:n, :] + tot[n - 1:n, :]


def _past_cumsum(lf_pages, heads):
    depth, nb, n_pages, width = lf_pages.shape
    pos = np.arange(width) // heads
    hd = np.arange(width) % heads
    t = ((hd[:, None] == hd[None, :]) & (pos[:, None] <= pos[None, :])).astype(np.float32)
    ssum = (hd[:, None] == np.arange(LANES)[None, :]).astype(np.float32)
    lstrict = np.tril(np.ones((n_pages, n_pages), np.float32), -1)
    e = (np.arange(LANES)[:, None] == hd[None, :]).astype(np.float32)
    t3 = jnp.asarray(np.concatenate([t, t, t], axis=0), BF16)
    s3 = jnp.asarray(np.concatenate([ssum, ssum, ssum], axis=0), BF16)
    l3 = jnp.asarray(np.concatenate([lstrict, lstrict, lstrict], axis=1), BF16)
    e3 = jnp.asarray(np.concatenate([e, e, e], axis=0), BF16)
    x = lf_pages.reshape(depth * nb, n_pages, width)
    full = lambda a: pl.BlockSpec(a.shape, lambda g: (0,) * a.ndim)
    return pl.pallas_call(
        _past_cumsum_kernel,
        grid=(depth * nb,),
        in_specs=[pl.BlockSpec((None, n_pages, width), lambda g: (g, 0, 0)),
                  full(t3), full(s3), full(l3), full(e3)],
        out_specs=[pl.BlockSpec((None, n_pages, width), lambda g: (g, 0, 0)),
                   pl.BlockSpec((None, 1, LANES), lambda g: (g, 0, 0))],
        out_shape=[jax.ShapeDtypeStruct((depth * nb, n_pages, width), F32),
                   jax.ShapeDtypeStruct((depth * nb, 1, LANES), F32)],
        compiler_params=_params("arbitrary"),
        name="past_cumsum",
    )(x, t3, s3, l3, e3)


def _fox_sample_kernel(pt_ref, *refs, group, heads, hd, nt, aliased):
    if aliased:
        refs = refs[2:]
    q_ref, kn_ref, vn_ref, ff_ref, b_ref, ctot_ref, cp_ref = refs[:7]
    refs = refs[7:]
    k_refs = refs[:group]
    v_refs = refs[group:2 * group]
    o_ref, lf_ref, ks_ref, vs_ref = refs[2 * group:2 * group + 4]
    qall_ref, cqm_ref, m_ref, l_ref, acc_ref = refs[2 * group + 4:]
    g = pl.program_id(1)
    ng = pl.num_programs(1)
    nrow = heads * nt
    prow = k_refs[0].shape[0]
    assert heads & (heads - 1) == 0 and nt & (nt - 1) == 0
    hbits = heads.bit_length() - 1
    tbits = nt.bit_length() - 1

    def head_of_col(cols):
        return jnp.bitwise_and(cols, heads - 1)

    @pl.when(g == 0)
    def _():
        lf = _log_sigmoid(ff_ref[...] + b_ref[...])
        lf_ref[...] = lf
        trow = lax.broadcasted_iota(jnp.int32, lf.shape, 0)
        tlane = lax.broadcasted_iota(jnp.int32, lf.shape, 1)
        cs = lf
        sh = 1
        while sh < nt:
            cs = cs + jnp.where(trow >= sh, pltpu.roll(cs, sh, 0), 0.0)
            sh *= 2
        cn = ctot_ref[...] + cs
        cq = jnp.concatenate([cn[:, h:h + 1] for h in range(heads)], axis=0)
        prows = lax.broadcasted_iota(jnp.int32, (nrow, prow), 0)
        pcols = lax.broadcasted_iota(jnp.int32, (nrow, prow), 1)
        cqm_ref[...] = jnp.where(head_of_col(pcols) == jnp.right_shift(prows, tbits), cq, -jnp.inf)
        qf = q_ref[...]
        qall = jnp.concatenate([qf[:, h * hd:(h + 1) * hd] for h in range(heads)], axis=0)
        qall_ref[...] = qall.astype(BF16)
        cnm = jnp.where(tlane < heads, cn, 0.0)
        ck = jnp.zeros((1, LANES), F32)
        for t in range(nt):
            rolled = pltpu.roll(cnm, t * heads, 1)
            ck = ck + jnp.sum(jnp.where(trow == t, rolled, 0.0), axis=0, keepdims=True)
        for h in range(heads):
            ks_ref[pl.ds(h, nt, stride=heads), :] = kn_ref[:, h * hd:(h + 1) * hd]
            vs_ref[pl.ds(h, nt, stride=heads), :] = vn_ref[:, h * hd:(h + 1) * hd]
        kn = _pad_rows(ks_ref[...], LANES).astype(BF16)
        vn = _pad_rows(vs_ref[...], LANES).astype(BF16)
        s = _dot_nt(qall_ref[...], kn) + cq - ck
        rows = lax.broadcasted_iota(jnp.int32, s.shape, 0)
        cols = lax.broadcasted_iota(jnp.int32, s.shape, 1)
        ok = (head_of_col(cols) == jnp.right_shift(rows, tbits))
        ok = ok & (jnp.right_shift(cols, hbits) <= jnp.bitwise_and(rows, nt - 1)) & (cols < nrow)
        s = jnp.where(ok, s, -jnp.inf)
        m0 = jnp.max(s, axis=-1, keepdims=True)
        p = jnp.exp(s - m0)
        m_ref[...] = m0
        l_ref[...] = jnp.sum(p, axis=-1, keepdims=True)
        acc_ref[...] = _dot(p.astype(BF16), vn)

    qall = qall_ref[...]
    cqm = cqm_ref[...]
    ss = []
    for r in range(group):
        kb = k_refs[r][...].astype(BF16)
        ss.append(_dot_nt(qall, kb) + (cqm - cp_ref[r:r + 1, :]))
    mx = ss[0]
    for s in ss[1:]:
        mx = jnp.maximum(mx, s)
    m_prev = m_ref[...]
    m_new = jnp.maximum(m_prev, jnp.max(mx, axis=-1, keepdims=True))
    alpha = jnp.exp(m_prev - m_new)
    acc = alpha * acc_ref[...]
    psum = jnp.zeros((nrow, prow), F32)
    for r in range(group):
        p = jnp.exp(ss[r] - m_new)
        psum = psum + p
        acc = acc + _dot(p.astype(BF16), v_refs[r][...].astype(BF16))
    l_ref[...] = alpha * l_ref[...] + jnp.sum(psum, axis=-1, keepdims=True)
    acc_ref[...] = acc
    m_ref[...] = m_new

    @pl.when(g == ng - 1)
    def _():
        o = acc_ref[...] / l_ref[...]
        for h in range(heads):
            o_ref[:, h * hd:(h + 1) * hd] = o[h * nt:(h + 1) * nt, :]


def _fox_sample(zf, ff, b_fox_pad, ctot, c_past, cache_k, cache_v, page_table, layer, depth,
                row0, nt, heads, hd, group, qcol, prev):
    nb, n_pages = page_table.shape
    prow = cache_k.shape[2]
    width = heads * hd
    ng = n_pages // group
    rb0 = row0 // nt
    nrow = heads * nt
    aliased = prev is not None

    def page_map(r):
        return lambda b, g, pt: (layer, pt[b, g * group + r], 0, 0)

    kern = functools.partial(_fox_sample_kernel, group=group, heads=heads, hd=hd, nt=nt, aliased=aliased)
    new_rows = pl.BlockSpec((None, nrow, hd), lambda b, g, pt: (layer, b, 0))
    new_shape = jax.ShapeDtypeStruct((depth, nb * nrow, hd), F32)
    in_specs = ([_any_spec(), _any_spec()] if aliased else []) + [
        pl.BlockSpec((nt, width), lambda b, g, pt: (rb0 + b, qcol)),
        pl.BlockSpec((nt, width), lambda b, g, pt: (rb0 + b, qcol + 1)),
        pl.BlockSpec((nt, width), lambda b, g, pt: (rb0 + b, qcol + 2)),
        pl.BlockSpec((nt, LANES), lambda b, g, pt: (rb0 + b, 0)),
        pl.BlockSpec((None, 1, LANES), lambda b, g, pt: (layer, 0, 0)),
        pl.BlockSpec((None, 1, LANES), lambda b, g, pt: (layer * nb + b, 0, 0)),
        pl.BlockSpec((None, group, c_past.shape[2]), lambda b, g, pt: (layer * nb + b, g, 0)),
    ]
    in_specs += [pl.BlockSpec((None, None, prow, hd), page_map(r)) for r in range(group)]
    in_specs += [pl.BlockSpec((None, None, prow, hd), page_map(r)) for r in range(group)]
    return pl.pallas_call(
        kern,
        grid_spec=pltpu.PrefetchScalarGridSpec(
            num_scalar_prefetch=1,
            grid=(nb, ng),
            in_specs=in_specs,
            out_specs=[pl.BlockSpec((nt, width), lambda b, g, pt: (b, 0)),
                       pl.BlockSpec((None, nt, LANES), lambda b, g, pt: (b, 0, 0)),
                       new_rows, new_rows],
            scratch_shapes=[
                pltpu.VMEM((nrow, hd), BF16),
                pltpu.VMEM((nrow, prow), F32),
                pltpu.VMEM((nrow, 1), F32),
                pltpu.VMEM((nrow, 1), F32),
                pltpu.VMEM((nrow, hd), F32),
            ],
        ),
        out_shape=[jax.ShapeDtypeStruct((nb * nt, width), F32),
                   jax.ShapeDtypeStruct((nb, nt, LANES), F32),
                   new_shape, new_shape],
        input_output_aliases={1: 2, 2: 3} if aliased else {},
        compiler_params=_params("arbitrary", "arbitrary"),
        name="fox_sample",
    )(page_table, *(prev if aliased else ()), zf, zf, zf, ff, b_fox_pad, ctot, c_past,
      *([cache_k] * group), *([cache_v] * group))


def _hgrn_operators(c, real):
    t = np.arange(c)[:, None]
    j = np.arange(c)[None, :]
    mats = [(j <= t)]
    masks = []
    m = 1
    while m < real:
        blk = t // (2 * m)
        mid = blk * (2 * m) + m - 1
        second = t > mid
        if m < SUBLANES:
            mats.append(np.where(second, (j > mid) & (j <= t), (j > t) & (j <= mid)))
        s = np.arange(c)[None, :]
        masks.append((blk == (s // (2 * m))) & second & (s <= (s // (2 * m)) * (2 * m) + m - 1))
        m *= 2
    a = np.concatenate(mats, axis=0).astype(np.float32)
    a3 = np.concatenate([a, a, a], axis=1)
    return jnp.asarray(a3, BF16), jnp.asarray(np.stack(masks).astype(np.float32)), len(masks)


def _hgrn_kernel(q_ref, f_ref, i_ref, g_ref, lbraw_ref, gn_ref, a3_ref, msk_ref, s0_ref,
                 o_ref, sfin_ref, st_ref, *, layer, heads, dk, n_lev, real):
    c = HGRN_CHUNK
    ci = pl.program_id(1)
    nc = pl.num_programs(1)

    @pl.when(ci == 0)
    def _():
        for h in range(heads):
            st_ref[h] = s0_ref[h].T

    raw = lbraw_ref[...]
    e = jnp.exp(raw - jnp.max(raw, axis=0, keepdims=True))
    sm = e / jnp.sum(e, axis=0, keepdims=True)
    cs = sm[0:1]
    for r in range(1, layer + 1):
        cs = cs + sm[r:r + 1]
    lb = cs - sm[0:1]
    log_lb = jnp.log(lb)
    log_1m = jnp.log1p(-lb)

    zf = _pad_rows(f_ref[...], c)
    q = _pad_rows(q_ref[...], c)
    v = _pad_rows(i_ref[...], c)
    b = log_1m + _log_sigmoid(zf)
    mx = jnp.maximum(log_lb, b)
    g = mx + jnp.log(1.0 + jnp.exp(-jnp.abs(log_lb - b)))
    kr = (1.0 - lb) * _sigmoid(-zf)
    if real < c:
        valid = lax.broadcasted_iota(jnp.int32, g.shape, 0) < real
        g = jnp.where(valid, g, 0.0)
        kr = jnp.where(valid, kr, 0.0)

    e = _dot(a3_ref[...], jnp.concatenate(_split3(g), axis=0))
    eq = e[0:c]
    xq = jnp.exp(eq)
    xk = jnp.exp(eq[c - 1:c] - eq)
    xlast = xq[c - 1:c]
    n_mx = e.shape[0] // c - 1

    def level_factor(l):
        if l < n_mx:
            return jnp.exp(e[(1 + l) * c:(2 + l) * c])
        m = 1 << l
        mids = [jnp.broadcast_to(eq[b0 + m - 1:b0 + m, :], (2 * m, eq.shape[1]))
                for b0 in range(0, c, 2 * m)]
        r = mids[0] if len(mids) == 1 else jnp.concatenate(mids, axis=0)
        return jnp.exp(-jnp.abs(eq - r))

    xls = [level_factor(l) for l in range(n_lev)]
    qe = (q * xq).astype(BF16)
    ke = (kr * xk).astype(BF16)
    vb = v.astype(BF16)
    for h in range(heads):
        hs = slice(h * dk, (h + 1) * dk)
        st = st_ref[h]
        o = _dot_nt(qe[:, hs], st.astype(BF16))
        att = jnp.zeros((c, c), F32)
        for l in range(n_lev):
            xl = xls[l][:, hs]
            p = _dot_nt((q[:, hs] * xl).astype(BF16), (kr[:, hs] * xl).astype(BF16))
            att = att + p * msk_ref[l]
        o = o + _dot(att.astype(BF16), vb[:, hs])
        o = o + jnp.sum(q[:, hs] * kr[:, hs], axis=-1, keepdims=True) * v[:, hs]
        st_ref[h] = st * xlast[:, hs] + _dot_tn(vb[:, hs], ke[:, hs])
        on = _rms_rows(o, gn_ref[:, hs])
        gg = g_ref[:, hs]
        o_ref[:, hs] = (on[:real] * (gg * _sigmoid(gg))).astype(o_ref.dtype)

    @pl.when(ci == nc - 1)
    def _():
        for h in range(heads):
            sfin_ref[h] = st_ref[h].T


def _hgrn(zf, hgrn_lb, hgrn_norm, s0, layer, row0, n_seq, seq, heads, dk, col0, out_rows, out_dtype):
    c = HGRN_CHUNK
    real = min(seq, c)
    assert seq % real == 0
    nc = seq // real
    width = heads * dk
    rb0 = row0 // real
    a3, masks, n_lev = _hgrn_operators(c, real)
    depth = hgrn_lb.shape[0]
    blk = lambda col: pl.BlockSpec((real, width), lambda n, i: (rb0 + n * nc + i, col))
    kern = functools.partial(_hgrn_kernel, layer=layer, heads=heads, dk=dk, n_lev=n_lev, real=real)
    in_specs = [
        blk(col0), blk(col0 + 1), blk(col0 + 2), blk(col0 + 3),
        pl.BlockSpec((depth, width), lambda n, i: (0, 0)),
        pl.BlockSpec((None, 1, width), lambda n, i: (layer, 0, 0)),
        pl.BlockSpec(a3.shape, lambda n, i: (0, 0)),
        pl.BlockSpec(masks.shape, lambda n, i: (0, 0, 0)),
        pl.BlockSpec((None, heads, dk, dk), lambda n, i: (n, 0, 0, 0)),
    ]
    return pl.pallas_call(
        kern,
        grid=(n_seq, nc),
        in_specs=in_specs,
        out_specs=[
            pl.BlockSpec((real, width), lambda n, i: (n * nc + i, 0)),
            pl.BlockSpec((None, heads, dk, dk), lambda n, i: (n, 0, 0, 0)),
        ],
        out_shape=[
            jax.ShapeDtypeStruct((out_rows, width), out_dtype),
            jax.ShapeDtypeStruct((n_seq, heads, dk, dk), F32),
        ],
        scratch_shapes=[pltpu.VMEM((heads, dk, dk), F32)],
        compiler_params=_params("arbitrary", "arbitrary"),
        name="hgrn",
    )(zf, zf, zf, zf, hgrn_lb, hgrn_norm, a3, masks, s0)


def _patch_rows_kernel(*refs):
    n = len(refs) // 3
    for small_ref, out_ref in zip(refs[n:2 * n], refs[2 * n:]):
        out_ref[...] = small_ref[...].astype(out_ref.dtype)


def _patch_rows(bigs, smalls, row0):
    rows = smalls[0].shape[0]
    assert row0 % rows == 0
    n = len(bigs)
    return pl.pallas_call(
        _patch_rows_kernel,
        grid=(1,),
        in_specs=[_any_spec()] * n + [pl.BlockSpec(s.shape, lambda i: (0, 0)) for s in smalls],
        out_specs=[pl.BlockSpec((rows, b.shape[1]), lambda i: (row0 // rows, 0)) for b in bigs],
        out_shape=[jax.ShapeDtypeStruct(b.shape, b.dtype) for b in bigs],
        input_output_aliases={i: i for i in range(n)},
        compiler_params=_params("arbitrary"),
        name="patch_rows",
    )(*bigs, *smalls)


def _merge_kernel(ma_ref, ob_ref, oc_ref, wa_ref, wb_ref, wc_ref, ga_ref, gb_ref, gc_ref, mix_ref):
    ya = _dot(ma_ref[...], wa_ref[...])
    yb = _dot(ob_ref[...], wb_ref[...])
    yc = _dot(oc_ref[...], wc_ref[...])
    gate = lambda r: _sigmoid(r[...].astype(F32))
    mix = gate(ga_ref) * ya + gate(gb_ref) * yb + gate(gc_ref) * yc
    mix_ref[...] = mix.astype(mix_ref.dtype)


def _merge(m_a, o_b, o_c, w_a, w_b, w_c, zg, layer, tm, tn):
    m, k = m_a.shape
    d = w_a.shape[2]
    nj = d // tn
    act = pl.BlockSpec((tm, k), lambda i, j: (i, 0))
    wsp = pl.BlockSpec((None, k, tn), lambda i, j: (layer, 0, j))
    gate = lambda r: pl.BlockSpec((tm, tn), lambda i, j: (i, r * nj + j))
    return pl.pallas_call(
        _merge_kernel,
        grid=(m // tm, nj),
        in_specs=[act, act, act, wsp, wsp, wsp, gate(0), gate(1), gate(2)],
        out_specs=pl.BlockSpec((tm, tn), lambda i, j: (i, j)),
        out_shape=jax.ShapeDtypeStruct((m, d), BF16),
        compiler_params=_params("arbitrary", "arbitrary"),
        name="merge",
    )(m_a, o_b, o_c, w_a, w_b, w_c, zg, zg, zg)


def _residual_matmul_kernel(a_ref, w_ref, x_ref, o_ref):
    o_ref[...] = x_ref[...] + _dot(a_ref[...], w_ref[...])


def _weight_spec(shape, index_map, resident):
    if resident:
        return pl.BlockSpec(shape, index_map, pipeline_mode=pl.Buffered(1))
    return pl.BlockSpec(shape, index_map)


def _residual_matmul(a, w, x, layer, tm, tn, single_buffer_a=False):
    m, k = a.shape
    d = w.shape[2]
    return pl.pallas_call(
        _residual_matmul_kernel,
        grid=(m // tm, d // tn),
        in_specs=[
            _weight_spec((tm, k), lambda i, j: (i, 0), single_buffer_a),
            _weight_spec((None, k, tn), lambda i, j: (layer, 0, j), tn == d),
            pl.BlockSpec((tm, tn), lambda i, j: (i, j)),
        ],
        out_specs=pl.BlockSpec((tm, tn), lambda i, j: (i, j)),
        out_shape=jax.ShapeDtypeStruct((m, d), F32),
        compiler_params=_params("arbitrary", "arbitrary"),
        name="residual_matmul",
    )(a, w, x)


def _swiglu_up_kernel(x_ref, g_ref, wg_ref, wu_ref, a_ref, h_ref):
    @pl.when(pl.program_id(1) == 0)
    def _():
        h_ref[...] = _rms_rows(x_ref[...], g_ref[...]).astype(BF16)

    h = h_ref[...]
    gate = _dot(h, wg_ref[...])
    up = _dot(h, wu_ref[...])
    a_ref[...] = (gate * _sigmoid(gate) * up).astype(a_ref.dtype)


def _swiglu_up(x, norm, w_gate_up, layer, tm, tn):
    m, d = x.shape
    f = w_gate_up.shape[2] // 2
    nj = f // tn
    return pl.pallas_call(
        _swiglu_up_kernel,
        grid=(m // tm, nj),
        in_specs=[
            pl.BlockSpec((tm, d), lambda i, j: (i, 0)),
            pl.BlockSpec((None, 1, d), lambda i, j: (layer, 0, 0)),
            pl.BlockSpec((None, d, tn), lambda i, j: (layer, 0, j)),
            pl.BlockSpec((None, d, tn), lambda i, j: (layer, 0, nj + j)),
        ],
        out_specs=pl.BlockSpec((tm, tn), lambda i, j: (i, j)),
        out_shape=jax.ShapeDtypeStruct((m, f), BF16),
        scratch_shapes=[pltpu.VMEM((tm, d), BF16)],
        compiler_params=_params("arbitrary", "arbitrary"),
        name="swiglu_up",
    )(x, norm, w_gate_up, w_gate_up)


def _ffn_kernel(x_ref, g_ref, wg_ref, wu_ref, wd_ref, o_ref, h_ref):
    @pl.when(pl.program_id(1) == 0)
    def _():
        x = x_ref[...]
        h_ref[...] = _rms_rows(x, g_ref[...]).astype(BF16)
        o_ref[...] = x

    h = h_ref[...]
    gate = _dot(h, wg_ref[...])
    up = _dot(h, wu_ref[...])
    a = (gate * _sigmoid(gate) * up).astype(BF16)
    o_ref[...] += _dot(a, wd_ref[...])


def _ffn(x, norm, w_gate_up, w_down, layer, tm, tf):
    m, d = x.shape
    f = w_down.shape[1]
    nj = f // tf
    return pl.pallas_call(
        _ffn_kernel,
        grid=(m // tm, nj),
        in_specs=[
            pl.BlockSpec((tm, d), lambda i, j: (i, 0)),
            pl.BlockSpec((None, 1, d), lambda i, j: (layer, 0, 0)),
            pl.BlockSpec((None, d, tf), lambda i, j: (layer, 0, j)),
            pl.BlockSpec((None, d, tf), lambda i, j: (layer, 0, nj + j)),
            pl.BlockSpec((None, tf, d), lambda i, j: (layer, j, 0)),
        ],
        out_specs=pl.BlockSpec((tm, d), lambda i, j: (i, 0)),
        out_shape=jax.ShapeDtypeStruct((m, d), F32),
        scratch_shapes=[pltpu.VMEM((tm, d), BF16)],
        compiler_params=_params("arbitrary", "arbitrary"),
        name="ffn",
    )(x, norm, w_gate_up, w_gate_up, w_down)


def _ple_kernel(x_ref, g_ref, wg_ref, p_ref, wp_ref, o_ref, h_ref, *, tn):
    j = pl.program_id(1)

    @pl.when(j == 0)
    def _():
        h_ref[...] = _rms_rows(x_ref[...], g_ref[...]).astype(BF16)

    gate = _sigmoid(_dot(h_ref[...], wg_ref[...]))
    emb = _dot(p_ref[...].astype(BF16), wp_ref[...])
    xs = x_ref[:, pl.ds(pl.multiple_of(j * tn, tn), tn)]
    o_ref[...] = xs + gate * emb


def _ple(x, norm, w_gate, p, w_proj, layer, tm, tn):
    m, d = x.shape
    pd = p.shape[2]
    return pl.pallas_call(
        functools.partial(_ple_kernel, tn=tn),
        grid=(m // tm, d // tn),
        in_specs=[
            pl.BlockSpec((tm, d), lambda i, j: (i, 0)),
            pl.BlockSpec((None, 1, d), lambda i, j: (layer, 0, 0)),
            _weight_spec((None, d, tn), lambda i, j: (layer, 0, j), tn == d),
            pl.BlockSpec((None, tm, pd), lambda i, j: (layer, i, 0)),
            _weight_spec((None, pd, tn), lambda i, j: (layer, 0, j), tn == d),
        ],
        out_specs=pl.BlockSpec((tm, tn), lambda i, j: (i, j)),
        out_shape=jax.ShapeDtypeStruct((m, d), F32),
        scratch_shapes=[pltpu.VMEM((tm, d), BF16)],
        compiler_params=_params("arbitrary", "arbitrary"),
        name="ple",
    )(x, norm, w_gate, p, w_proj)


def _final_norm_kernel(x_ref, g_ref, o_ref):
    o_ref[...] = _rms_rows(x_ref[...], g_ref[...])


def _final_norm(x, g, row0, rows, tile):
    d = x.shape[1]
    rb0 = row0 // tile
    return pl.pallas_call(
        _final_norm_kernel,
        grid=(rows // tile,),
        in_specs=[pl.BlockSpec((tile, d), lambda i: (rb0 + i, 0)), pl.BlockSpec((1, d), lambda i: (0, 0))],
        out_specs=pl.BlockSpec((tile, d), lambda i: (i, 0)),
        out_shape=jax.ShapeDtypeStruct((rows, d), F32),
        compiler_params=_params("arbitrary"),
        name="final_norm",
    )(x, g)


def kernel(x_prompt, x_sample, cache_k, cache_v, cache_logf, state_conv, state_hgrn, page_table,
           p_prompt, p_sample, norm1, w_in, b_fox, conv_w, hgrn_lb, hgrn_norm, w_a, w_b, w_c, w_o,
           norm2, w_gate_up, w_down, ple_norm, ple_gate, ple_proj, final_norm):
    bp, seq, d = x_prompt.shape
    db, dseq, _ = x_sample.shape
    depth = w_in.shape[0]
    heads, hd = cache_k.shape[3], cache_k.shape[4]
    page = cache_k.shape[2]
    n_pool = cache_k.shape[1]
    cw = w_a.shape[1]
    fw = w_b.shape[1]
    rw = w_c.shape[1]
    rheads = state_hgrn.shape[2]
    dk = state_hgrn.shape[3]
    assert fw == heads * hd and rw == rheads * dk and state_hgrn.shape[4] == dk and cw == fw == rw
    mp = bp * seq
    ms = db * dseq
    m = mp + ms

    o_ff = 3 * cw + 3 * fw
    widths = (3 * cw, fw, 2 * fw, 4 * rw, 3 * d)
    w_main = jnp.concatenate([w_in[:, :, :o_ff], w_in[:, :, o_ff + heads:]], axis=2).astype(BF16)
    w_ff = jnp.pad(w_in[:, :, o_ff:o_ff + heads], ((0, 0), (0, 0), (0, LANES - heads))).astype(BF16)
    assert w_main.shape[2] == sum(widths)
    tn_in = fw
    qcol = 3 * cw // tn_in
    rcol = qcol + 3

    bf = lambda a: a.astype(BF16)
    w_a_b, w_b_b, w_c_b, w_o_b = bf(w_a), bf(w_b), bf(w_c), bf(w_o)
    w_gu_b, w_dn_b, w_pg_b, w_pp_b = bf(w_gate_up), bf(w_down), bf(ple_gate), bf(ple_proj)
    row3 = lambda a: a.reshape(a.shape[0], 1, a.shape[1])
    norm1_r, norm2_r, ple_norm_r, hnorm_r = row3(norm1), row3(norm2), row3(ple_norm), row3(hgrn_norm)
    b_fox_pad = row3(jnp.pad(b_fox, ((0, 0), (0, LANES - heads))))

    x = _stack_rows(x_prompt.reshape(1, mp, d), x_sample.reshape(1, ms, d)).reshape(m, d)
    p_all = _stack_rows(p_prompt.reshape(depth, mp, -1), p_sample.reshape(depth, ms, -1))
    ck = cache_k.reshape(depth, n_pool, page * heads, hd)
    cv = cache_v.reshape(depth, n_pool, page * heads, hd)

    n_pages = page_table.shape[1]
    group = _tile(n_pages, 16, 1)
    lf_flat = cache_logf.reshape(depth, n_pool, 1, page * heads)
    lf_pages = _gather_logf_pages(lf_flat, page_table, group)
    c_past, c_tot = _past_cumsum(lf_pages, heads)

    tm = _tile(m, 704, 16)
    tq =_tile(seq, 512, LANES)
    conv_tile = _tile(seq, 512, 8)
    zeros_conv = jnp.zeros((bp, 2, cw), F32)
    zeros_state = jnp.zeros((bp, rheads, dk, dk), F32)

    outs = {k: [] for k in ("lfp", "cp", "sp", "lfs", "cs", "ss")}
    kvp = None
    kvs = None
    for i in range(depth):
        zf, kvb, zg, ff = _in_proj(x, norm1_r, w_main, w_ff, i, tm, tn_in, widths, hd)

        m_a, conv_p = _conv(zf, conv_w, zeros_conv, i, 0, bp, seq, conv_tile, cw, m, BF16)
        ma_s, conv_s = _conv(zf, conv_w, state_conv[i], i, mp, db, dseq, dseq, cw, ms, F32)

        lf_p, qaug_t, kaug, v_t, kp, vp = _logf_prompt(ff, b_fox_pad, zf, kvb, i, depth, bp, seq,
                                                       heads, hd, qcol, kvp)
        kvp = (kp, vp)
        o_b = _fox_prompt(qaug_t, kaug, v_t, m, bp, seq, heads, hd, tq)
        ob_s, lf_s, ks, vs = _fox_sample(zf, ff, b_fox_pad, c_tot, c_past, ck, cv, page_table, i,
                                         depth, mp, dseq, heads, hd, group, qcol, kvs)
        kvs = (ks, vs)

        o_c, st_p = _hgrn(zf, hgrn_lb, hnorm_r, zeros_state, i, 0, bp, seq, rheads, dk, rcol, m, BF16)
        oc_s, st_s = _hgrn(zf, hgrn_lb, hnorm_r, state_hgrn[i], i, mp, db, dseq, rheads, dk, rcol,
                           ms, F32)

        m_a, o_b, o_c = _patch_rows((m_a, o_b, o_c), (ma_s, ob_s, oc_s), mp)
        mix = _merge(m_a, o_b, o_c, w_a_b, w_b_b, w_c_b, zg, i, tm, 1024)
        x = _residual_matmul(mix, w_o_b, x, i, tm, d)

        x = _ffn(x, norm2_r, w_gu_b, w_dn_b, i, tm, _tile(w_down.shape[1], 512, 2 * LANES))
        x = _ple(x, ple_norm_r, w_pg_b, p_all, w_pp_b, i, tm, d)

        outs["lfp"].append(lf_p[:, :, :heads])
        outs["lfs"].append(lf_s[:, :, :heads])
        outs["cp"].append(conv_p)
        outs["cs"].append(conv_s)
        outs["sp"].append(st_p)
        outs["ss"].append(st_s)

    fn = final_norm.reshape(1, d)
    y_p = _final_norm(x, fn, 0, mp, _tile(mp, 1024, 8))
    y_s = _final_norm(x, fn, mp, ms, ms)
    st = lambda k: jnp.stack(outs[k])
    return (y_p.reshape(bp, seq, d), y_s.reshape(db, dseq, d),
            kp.reshape(depth, bp, seq, heads, hd), vp.reshape(depth, bp, seq, heads, hd),
            st("lfp"), st("cp"), st("sp"),
            ks.reshape(depth, db, dseq, heads, hd), vs.reshape(depth, db, dseq, heads, hd),
            st("lfs"), st("cs"), st("ss"))
```

```python
import functools

import numpy as np
import jax
import jax.numpy as jnp
from jax import lax
from jax.experimental import pallas as pl
from jax.experimental.pallas import tpu as pltpu

F32 = jnp.float32
BF16 = jnp.bfloat16
RMS_EPS = 1e-6
LANES = 128
SUBLANES = 8
HGRN_CHUNK = 128
VMEM_LIMIT = 56 * 1024 * 1024


def _tile(n, target, align):
    best = None
    for t in range(align, min(n, target) + 1, align):
        if n % t == 0:
            best = t
    assert best is not None, (n, target, align)
    return best


def _params(*sem):
    return pltpu.CompilerParams(dimension_semantics=sem, vmem_limit_bytes=VMEM_LIMIT)


def _sigmoid(x):
    return 1.0 / (1.0 + jnp.exp(-x))


def _log_sigmoid(x):
    return jnp.minimum(x, 0.0) - jnp.log(1.0 + jnp.exp(-jnp.abs(x)))


def _rms_rows(x, g):
    ms = jnp.mean(x * x, axis=-1, keepdims=True)
    return x * lax.rsqrt(ms + RMS_EPS) * g


def _split3(x):
    hi = x.astype(BF16)
    r1 = x - hi.astype(F32)
    mid = r1.astype(BF16)
    lo = (r1 - mid.astype(F32)).astype(BF16)
    return hi, mid, lo


def _dot(a, b):
    return jnp.dot(a, b, preferred_element_type=F32)


def _dot_nt(a, b):
    return lax.dot_general(a, b, (((1,), (1,)), ((), ())), preferred_element_type=F32)


def _dot_tn(a, b):
    return lax.dot_general(a, b, (((0,), (0,)), ((), ())), preferred_element_type=F32)


def _pad_rows(a, n):
    if a.shape[0] == n:
        return a
    return jnp.concatenate([a, jnp.zeros((n - a.shape[0], a.shape[1]), a.dtype)], axis=0)


def _any_spec():
    return pl.BlockSpec(memory_space=pl.ANY)


def _copy_kernel(*refs, aliased):
    a_ref, o_ref = refs[1:] if aliased else refs
    o_ref[...] = a_ref[...]


def _stack_rows(a, b):
    nl, ra, c = a.shape
    rb = b.shape[1]
    assert ra % rb == 0
    ta = _tile(ra, 1024, 8)
    shape = jax.ShapeDtypeStruct((nl, ra + rb, c), a.dtype)
    out = pl.pallas_call(
        functools.partial(_copy_kernel, aliased=False),
        grid=(nl, ra // ta),
        in_specs=[pl.BlockSpec((None, ta, c), lambda l, i: (l, i, 0))],
        out_specs=pl.BlockSpec((None, ta, c), lambda l, i: (l, i, 0)),
        out_shape=shape,
        compiler_params=_params("arbitrary", "arbitrary"),
        name="stack_rows_a",
    )(a)
    return pl.pallas_call(
        functools.partial(_copy_kernel, aliased=True),
        grid=(nl,),
        in_specs=[_any_spec(), pl.BlockSpec((None, rb, c), lambda l: (l, 0, 0))],
        out_specs=pl.BlockSpec((None, rb, c), lambda l: (l, ra // rb, 0)),
        out_shape=shape,
        input_output_aliases={0: 0},
        compiler_params=_params("arbitrary"),
        name="stack_rows_b",
    )(out, b)


def _norm_ff_kernel(x_ref, g_ref, wff_ref, h_ref, ff_ref):
    h = _rms_rows(x_ref[...], g_ref[...]).astype(BF16)
    h_ref[...] = h
    ff_ref[...] = _dot(h, wff_ref[...])


def _norm_ff(x, norm, w_ff, layer, tm):
    m, d = x.shape
    return pl.pallas_call(
        _norm_ff_kernel,
        grid=(m // tm,),
        in_specs=[
            pl.BlockSpec((tm, d), lambda i: (i, 0)),
            pl.BlockSpec((None, 1, d), lambda i: (layer, 0, 0)),
            pl.BlockSpec((None, d, LANES), lambda i: (layer, 0, 0)),
        ],
        out_specs=[pl.BlockSpec((tm, d), lambda i: (i, 0)), pl.BlockSpec((tm, LANES), lambda i: (i, 0))],
        out_shape=[jax.ShapeDtypeStruct((m, d), BF16), jax.ShapeDtypeStruct((m, LANES), F32)],
        compiler_params=_params("arbitrary"),
        name="norm_ff",
    )(x, norm, w_ff)


def _proj_kernel(*refs, scale_tile, scale, aliased):
    h_ref, w_ref, o_ref, wb_ref = refs[1:] if aliased else refs

    @pl.when(pl.program_id(1) == 0)
    def _():
        wb_ref[...] = w_ref[...].astype(BF16)

    z = _dot(h_ref[...], wb_ref[...])
    if scale_tile is not None:
        z = z * jnp.where(pl.program_id(0) == scale_tile, scale, 1.0)
    o_ref[...] = z.astype(o_ref.dtype)


def _proj(h, w32, layer, tm, tn, wcol0, nj, out_cols, ocol0, out_dtype, scale_tile=None, scale=1.0,
          base=None):
    m, d = h.shape
    aliased = base is not None
    kern = functools.partial(_proj_kernel, scale_tile=scale_tile, scale=scale, aliased=aliased)
    in_specs = [
        pl.BlockSpec((tm, d), lambda j, i: (i, 0)),
        pl.BlockSpec((None, d, tn), lambda j, i: (layer, 0, wcol0 + j)),
    ]
    args = [h, w32]
    if aliased:
        in_specs = [_any_spec()] + in_specs
        args = [base] + args
    return pl.pallas_call(
        kern,
        grid=(nj, m // tm),
        in_specs=in_specs,
        out_specs=pl.BlockSpec((tm, tn), lambda j, i: (i, ocol0 + j)),
        out_shape=jax.ShapeDtypeStruct((m, out_cols * tn), out_dtype),
        scratch_shapes=[pltpu.VMEM((d, tn), BF16)],
        input_output_aliases={0: 0} if aliased else {},
        compiler_params=_params("arbitrary", "arbitrary"),
        name="in_proj",
    )(*args)


def _bias_operators(heads, width):
    hd = width // heads
    selq = np.zeros((3 * LANES, width), np.float32)
    selk = np.zeros((3 * LANES, width), np.float32)
    cq = np.zeros((1, width), np.float32)
    ck = np.zeros((1, width), np.float32)
    for h in range(heads):
        for r in range(3):
            selq[r * LANES + h, h * hd + r] = 1.0
            selk[r * LANES + h, h * hd + 3 + r] = -1.0
            cq[0, h * hd + 3 + r] = 1.0
            ck[0, h * hd + r] = 1.0
    return jnp.asarray(selq, BF16), jnp.asarray(selk, BF16), jnp.asarray(cq), jnp.asarray(ck)


def _logf_prompt_kernel(*refs, heads, hd, aliased):
    if aliased:
        refs = refs[2:]
    (ff_ref, b_ref, q_ref, k32_ref, v32_ref, tri_ref, selq_ref, selk_ref, cq_ref, ck_ref,
     lf_ref, qat_ref, ka_ref, vt_ref, kp_ref, vp_ref, carry_ref) = refs
    blk = ff_ref.shape[0]
    for h in range(heads):
        kp_ref[pl.ds(h, blk, stride=heads), :] = k32_ref[:, h * hd:(h + 1) * hd]
        vp_ref[pl.ds(h, blk, stride=heads), :] = v32_ref[:, h * hd:(h + 1) * hd]

    @pl.when(pl.program_id(1) == 0)
    def _():
        carry_ref[...] = jnp.zeros(carry_ref.shape, F32)

    lf = _log_sigmoid(ff_ref[...] + b_ref[...])
    lf_ref[...] = lf
    cb = _dot(tri_ref[...], jnp.concatenate(_split3(lf), axis=0)) + carry_ref[...]
    carry_ref[...] = cb[blk - 1:blk, :]
    c3 = jnp.concatenate(_split3(cb), axis=1)
    qb = _dot(c3, selq_ref[...]) + cq_ref[...]
    kb = (_dot(c3, selk_ref[...]) + ck_ref[...]).astype(BF16)
    for h in range(heads):
        hs = slice(h * hd, (h + 1) * hd)
        qat_ref[2 * h * hd:(2 * h + 1) * hd, :] = q_ref[:, hs].T.astype(BF16)
        qat_ref[(2 * h + 1) * hd:(2 * h + 2) * hd, :] = qb[:, hs].T.astype(BF16)
        vt_ref[hs, :] = v32_ref[:, hs].T.astype(BF16)
        ka_ref[:, 2 * h * hd:(2 * h + 1) * hd] = k32_ref[:, hs].astype(BF16)
        ka_ref[:, (2 * h + 1) * hd:(2 * h + 2) * hd] = kb[:, hs]


def _logf_prompt(ff, b_fox_pad, zf, layer, depth, batch, seq, heads, hd, qcol, prev):
    width = heads * hd
    blk = _tile(seq, 512, LANES)
    nb = seq // blk
    tri = np.tril(np.ones((blk, blk), np.float32))
    tri3 = jnp.asarray(np.concatenate([tri, tri, tri], axis=1), BF16)
    ops = _bias_operators(heads, width)
    full = lambda a: pl.BlockSpec(a.shape, lambda b, i: (0,) * a.ndim)
    rows = lambda c, w: pl.BlockSpec((blk, w), lambda b, i: (b * nb + i, c))
    cache = pl.BlockSpec((None, blk * heads, hd), lambda b, i: (layer, b * nb + i, 0))
    aliased = prev is not None
    in_specs = [
        rows(0, LANES),
        pl.BlockSpec((None, 1, LANES), lambda b, i: (layer, 0, 0)),
        rows(qcol, width),
        rows(qcol + 1, width),
        rows(qcol + 2, width),
        full(tri3), full(ops[0]), full(ops[1]), full(ops[2]), full(ops[3]),
    ]
    args = [ff, b_fox_pad, zf, zf, zf, tri3, *ops]
    if aliased:
        in_specs = [_any_spec(), _any_spec()] + in_specs
        args = list(prev) + args
    cache_shape = jax.ShapeDtypeStruct((depth, batch * seq * heads, hd), F32)
    return pl.pallas_call(
        functools.partial(_logf_prompt_kernel, heads=heads, hd=hd, aliased=aliased),
        grid=(batch, nb),
        in_specs=in_specs,
        out_specs=[
            pl.BlockSpec((None, blk, LANES), lambda b, i: (b, i, 0)),
            pl.BlockSpec((2 * width, blk), lambda b, i: (0, b * nb + i)),
            rows(0, 2 * width),
            pl.BlockSpec((width, blk), lambda b, i: (0, b * nb + i)),
            cache,
            cache,
        ],
        out_shape=[
            jax.ShapeDtypeStruct((batch, seq, LANES), F32),
            jax.ShapeDtypeStruct((2 * width, batch * seq), BF16),
            jax.ShapeDtypeStruct((batch * seq, 2 * width), BF16),
            jax.ShapeDtypeStruct((width, batch * seq), BF16),
            cache_shape,
            cache_shape,
        ],
        scratch_shapes=[pltpu.VMEM((1, LANES), F32)],
        input_output_aliases={0: 4, 1: 5} if aliased else {},
        compiler_params=_params("arbitrary", "arbitrary"),
        name="logf_prompt",
    )(*args)


def _conv_kernel(ab_ref, ac_ref, av_ref, w_ref, prev_ref, m_ref, newc_ref, carry_ref):
    i = pl.program_id(1)
    t = ac_ref.shape[0]

    @pl.when(i == 0)
    def _():
        carry_ref[...] = prev_ref[...]

    u = ac_ref[...] * av_ref[...]
    row = lax.broadcasted_iota(jnp.int32, u.shape, 0)
    p1 = carry_ref[1:2, :]
    p2 = carry_ref[0:1, :]
    u1 = jnp.where(row == 0, p1, pltpu.roll(u, 1, 0))
    u2 = pltpu.roll(u, 2, 0)
    u2 = jnp.where(row == 0, p2, jnp.where(row == 1, p1, u2))
    w = w_ref[...]
    conv = w[0:1, :] * u2 + w[1:2, :] * u1 + w[2:3, :] * u
    m_ref[...] = (ab_ref[...] * conv).astype(m_ref.dtype)
    last2 = u[t - 2:t, :]
    carry_ref[...] = last2
    newc_ref[...] = last2


def _conv(zf, conv_w, prev, layer, row0, n_seq, seq, tile, width, out_rows, out_dtype):
    nt = seq // tile
    rb0 = row0 // tile
    rows = lambda c: pl.BlockSpec((tile, width), lambda n, i: (rb0 + n * nt + i, c))
    return pl.pallas_call(
        _conv_kernel,
        grid=(n_seq, nt),
        in_specs=[rows(0), rows(1), rows(2),
                  pl.BlockSpec((None, 3, width), lambda n, i: (layer, 0, 0)),
                  pl.BlockSpec((None, 2, width), lambda n, i: (n, 0, 0))],
        out_specs=[
            pl.BlockSpec((tile, width), lambda n, i: (n * nt + i, 0)),
            pl.BlockSpec((None, 2, width), lambda n, i: (n, 0, 0)),
        ],
        out_shape=[
            jax.ShapeDtypeStruct((out_rows, width), out_dtype),
            jax.ShapeDtypeStruct((n_seq, 2, width), F32),
        ],
        scratch_shapes=[pltpu.VMEM((2, width), F32)],
        compiler_params=_params("arbitrary", "arbitrary"),
        name="gated_conv",
    )(zf, zf, zf, conv_w, prev)


def _fox_prompt_kernel(qat_ref, ka_ref, vt_ref, o_ref, m_ref, l_ref, acc_ref, *, heads, hd):
    qi = pl.program_id(1)
    ki = pl.program_id(2)
    tq = qat_ref.shape[1]
    tk = ka_ref.shape[0]

    @pl.when(ki == 0)
    def _():
        m_ref[...] = jnp.full(m_ref.shape, -jnp.inf, F32)
        l_ref[...] = jnp.zeros(l_ref.shape, F32)
        acc_ref[...] = jnp.zeros(acc_ref.shape, F32)

    def step(masked):
        if masked:
            krow = lax.broadcasted_iota(jnp.int32, (tk, tq), 0)
            qcol = lax.broadcasted_iota(jnp.int32, (tk, tq), 1)
            keep = krow <= qcol
        for h in range(heads):
            aug = slice(2 * h * hd, (2 * h + 2) * hd)
            st = _dot(ka_ref[:, aug], qat_ref[aug, :])
            if masked:
                st = jnp.where(keep, st, -jnp.inf)
            m_prev = m_ref[h]
            m_new = jnp.maximum(m_prev, jnp.max(st, axis=0, keepdims=True))
            alpha = jnp.exp(m_prev - m_new)
            p = jnp.exp(st - m_new)
            l_ref[h] = alpha * l_ref[h] + jnp.sum(p, axis=0, keepdims=True)
            acc_ref[h] = alpha * acc_ref[h] + _dot(vt_ref[h * hd:(h + 1) * hd, :], p.astype(BF16))
            m_ref[h] = m_new

    @pl.when(ki < qi)
    def _():
        step(False)

    @pl.when(ki == qi)
    def _():
        step(True)
        for h in range(heads):
            o_ref[:, h * hd:(h + 1) * hd] = (acc_ref[h] / l_ref[h]).T.astype(o_ref.dtype)


def _fox_prompt(qaug_t, kaug, v_t, m_rows, batch, seq, heads, hd, tq):
    nq = seq // tq
    width = heads * hd
    kern = functools.partial(_fox_prompt_kernel, heads=heads, hd=hd)
    qmap = lambda b, qi, ki: (b * nq + qi, 0)
    kmap = lambda b, qi, ki: (b * nq + jnp.minimum(ki, qi), 0)
    return pl.pallas_call(
        kern,
        grid=(batch, nq, nq),
        in_specs=[
            pl.BlockSpec((2 * width, tq), lambda b, qi, ki: (0, b * nq + qi)),
            pl.BlockSpec((tq, 2 * width), kmap),
            pl.BlockSpec((width, tq), lambda b, qi, ki: (0, b * nq + jnp.minimum(ki, qi))),
        ],
        out_specs=pl.BlockSpec((tq, width), qmap),
        out_shape=jax.ShapeDtypeStruct((m_rows, width), BF16),
        scratch_shapes=[
            pltpu.VMEM((heads, 1, tq), F32),
            pltpu.VMEM((heads, 1, tq), F32),
            pltpu.VMEM((heads, hd, tq), F32),
        ],
        compiler_params=_params("arbitrary", "arbitrary", "arbitrary"),
        name="fox_prompt",
    )(qaug_t, kaug, v_t)


def _gather_rows_kernel(pt_ref, *refs):
    n_in = len(refs) - 1
    out_ref = refs[-1]
    for r in range(n_in):
        out_ref[:, r, :] = refs[r][:, 0, :]


def _gather_logf_pages(cache_logf_flat, page_table, group):
    depth, _, _, width = cache_logf_flat.shape
    nb, n_pages = page_table.shape
    ng = n_pages // group

    def in_map(r):
        return lambda b, g, pt: (0, pt[b, g * group + r], 0, 0)

    return pl.pallas_call(
        _gather_rows_kernel,
        grid_spec=pltpu.PrefetchScalarGridSpec(
            num_scalar_prefetch=1,
            grid=(nb, ng),
            in_specs=[pl.BlockSpec((depth, None, 1, width), in_map(r)) for r in range(group)],
            out_specs=pl.BlockSpec((depth, None, group, width), lambda b, g, pt: (0, b, g, 0)),
        ),
        out_shape=jax.ShapeDtypeStruct((depth, nb, n_pages, width), F32),
        compiler_params=_params("arbitrary", "arbitrary"),
        name="gather_logf",
    )(page_table, *([cache_logf_flat] * group))


def _past_cumsum_kernel(x_ref, t3_ref, s3_ref, l3_ref, e3_ref, c_ref, tot_ref):
    x3 = jnp.concatenate(_split3(x_ref[...]), axis=1)
    y = _dot(x3, t3_ref[...])
    tot = _dot(x3, s3_ref[...])
    offs = _dot(l3_ref[...], jnp.concatenate(_split3(tot), axis=0))
    c_ref[...] = y + _dot(jnp.concatenate(_split3(offs), axis=1), e3_ref[...])
    n = tot.shape[0]
    tot_ref[...] = offs[n - 1:n, :] + tot[n - 1:n, :]


def _past_cumsum(lf_pages, heads):
    depth, nb, n_pages, width = lf_pages.shape
    pos = np.arange(width) // heads
    hd = np.arange(width) % heads
    t = ((hd[:, None] == hd[None, :]) & (pos[:, None] <= pos[None, :])).astype(np.float32)
    ssum = (hd[:, None] == np.arange(LANES)[None, :]).astype(np.float32)
    lstrict = np.tril(np.ones((n_pages, n_pages), np.float32), -1)
    e = (np.arange(LANES)[:, None] == hd[None, :]).astype(np.float32)
    t3 = jnp.asarray(np.concatenate([t, t, t], axis=0), BF16)
    s3 = jnp.asarray(np.concatenate([ssum, ssum, ssum], axis=0), BF16)
    l3 = jnp.asarray(np.concatenate([lstrict, lstrict, lstrict], axis=1), BF16)
    e3 = jnp.asarray(np.concatenate([e, e, e], axis=0), BF16)
    x = lf_pages.reshape(depth * nb, n_pages, width)
    full = lambda a: pl.BlockSpec(a.shape, lambda g: (0,) * a.ndim)
    return pl.pallas_call(
        _past_cumsum_kernel,
        grid=(depth * nb,),
        in_specs=[pl.BlockSpec((None, n_pages, width), lambda g: (g, 0, 0)),
                  full(t3), full(s3), full(l3), full(e3)],
        out_specs=[pl.BlockSpec((None, n_pages, width), lambda g: (g, 0, 0)),
                   pl.BlockSpec((None, 1, LANES), lambda g: (g, 0, 0))],
        out_shape=[jax.ShapeDtypeStruct((depth * nb, n_pages, width), F32),
                   jax.ShapeDtypeStruct((depth * nb, 1, LANES), F32)],
        compiler_params=_params("arbitrary"),
        name="past_cumsum",
    )(x, t3, s3, l3, e3)


def _fox_sample_kernel(pt_ref, *refs, group, heads, hd, nt, aliased):
    if aliased:
        refs = refs[2:]
    q_ref, kn_ref, vn_ref, ff_ref, b_ref, ctot_ref, cp_ref = refs[:7]
    refs = refs[7:]
    k_refs = refs[:group]
    v_refs = refs[group:2 * group]
    o_ref, lf_ref, ks_ref, vs_ref = refs[2 * group:2 * group + 4]
    qall_ref, cqm_ref, m_ref, l_ref, acc_ref = refs[2 * group + 4:]
    g = pl.program_id(1)
    ng = pl.num_programs(1)
    nrow = heads * nt
    prow = k_refs[0].shape[0]
    assert heads & (heads - 1) == 0 and nt & (nt - 1) == 0
    hbits = heads.bit_length() - 1
    tbits = nt.bit_length() - 1

    def head_of_col(cols):
        return jnp.bitwise_and(cols, heads - 1)

    @pl.when(g == 0)
    def _():
        lf = _log_sigmoid(ff_ref[...] + b_ref[...])
        lf_ref[...] = lf
        trow = lax.broadcasted_iota(jnp.int32, lf.shape, 0)
        tlane = lax.broadcasted_iota(jnp.int32, lf.shape, 1)
        cs = lf
        sh = 1
        while sh < nt:
            cs = cs + jnp.where(trow >= sh, pltpu.roll(cs, sh, 0), 0.0)
            sh *= 2
        cn = ctot_ref[...] + cs
        cq = jnp.concatenate([cn[:, h:h + 1] for h in range(heads)], axis=0)
        prows = lax.broadcasted_iota(jnp.int32, (nrow, prow), 0)
        pcols = lax.broadcasted_iota(jnp.int32, (nrow, prow), 1)
        cqm_ref[...] = jnp.where(head_of_col(pcols) == jnp.right_shift(prows, tbits), cq, -jnp.inf)
        qf = q_ref[...]
        qall = jnp.concatenate([qf[:, h * hd:(h + 1) * hd] for h in range(heads)], axis=0)
        qall_ref[...] = qall.astype(BF16)
        cnm = jnp.where(tlane < heads, cn, 0.0)
        ck = jnp.zeros((1, LANES), F32)
        for t in range(nt):
            rolled = pltpu.roll(cnm, t * heads, 1)
            ck = ck + jnp.sum(jnp.where(trow == t, rolled, 0.0), axis=0, keepdims=True)
        for h in range(heads):
            ks_ref[pl.ds(h, nt, stride=heads), :] = kn_ref[:, h * hd:(h + 1) * hd]
            vs_ref[pl.ds(h, nt, stride=heads), :] = vn_ref[:, h * hd:(h + 1) * hd]
        kn = _pad_rows(ks_ref[...], LANES).astype(BF16)
        vn = _pad_rows(vs_ref[...], LANES).astype(BF16)
        s = _dot_nt(qall_ref[...], kn) + cq - ck
        rows = lax.broadcasted_iota(jnp.int32, s.shape, 0)
        cols = lax.broadcasted_iota(jnp.int32, s.shape, 1)
        ok = (head_of_col(cols) == jnp.right_shift(rows, tbits))
        ok = ok & (jnp.right_shift(cols, hbits) <= jnp.bitwise_and(rows, nt - 1)) & (cols < nrow)
        s = jnp.where(ok, s, -jnp.inf)
        m0 = jnp.max(s, axis=-1, keepdims=True)
        p = jnp.exp(s - m0)
        m_ref[...] = m0
        l_ref[...] = jnp.sum(p, axis=-1, keepdims=True)
        acc_ref[...] = _dot(p.astype(BF16), vn)

    qall = qall_ref[...]
    cqm = cqm_ref[...]
    ss = []
    for r in range(group):
        kb = k_refs[r][...].astype(BF16)
        ss.append(_dot_nt(qall, kb) + (cqm - cp_ref[r:r + 1, :]))
    mx = ss[0]
    for s in ss[1:]:
        mx = jnp.maximum(mx, s)
    m_prev = m_ref[...]
    m_new = jnp.maximum(m_prev, jnp.max(mx, axis=-1, keepdims=True))
    alpha = jnp.exp(m_prev - m_new)
    acc = alpha * acc_ref[...]
    psum = jnp.zeros((nrow, prow), F32)
    for r in range(group):
        p = jnp.exp(ss[r] - m_new)
        psum = psum + p
        acc = acc + _dot(p.astype(BF16), v_refs[r][...].astype(BF16))
    l_ref[...] = alpha * l_ref[...] + jnp.sum(psum, axis=-1, keepdims=True)
    acc_ref[...] = acc
    m_ref[...] = m_new

    @pl.when(g == ng - 1)
    def _():
        o = acc_ref[...] / l_ref[...]
        for h in range(heads):
            o_ref[:, h * hd:(h + 1) * hd] = o[h * nt:(h + 1) * nt, :]


def _fox_sample(zf, ff, b_fox_pad, ctot, c_past, cache_k, cache_v, page_table, layer, depth,
                row0, nt, heads, hd, group, qcol, prev):
    nb, n_pages = page_table.shape
    prow = cache_k.shape[2]
    width = heads * hd
    ng = n_pages // group
    rb0 = row0 // nt
    nrow = heads * nt
    aliased = prev is not None

    def page_map(r):
        return lambda b, g, pt: (layer, pt[b, g * group + r], 0, 0)

    kern = functools.partial(_fox_sample_kernel, group=group, heads=heads, hd=hd, nt=nt, aliased=aliased)
    new_rows = pl.BlockSpec((None, nrow, hd), lambda b, g, pt: (layer, b, 0))
    new_shape = jax.ShapeDtypeStruct((depth, nb * nrow, hd), F32)
    in_specs = ([_any_spec(), _any_spec()] if aliased else []) + [
        pl.BlockSpec((nt, width), lambda b, g, pt: (rb0 + b, qcol)),
        pl.BlockSpec((nt, width), lambda b, g, pt: (rb0 + b, qcol + 1)),
        pl.BlockSpec((nt, width), lambda b, g, pt: (rb0 + b, qcol + 2)),
        pl.BlockSpec((nt, LANES), lambda b, g, pt: (rb0 + b, 0)),
        pl.BlockSpec((None, 1, LANES), lambda b, g, pt: (layer, 0, 0)),
        pl.BlockSpec((None, 1, LANES), lambda b, g, pt: (layer * nb + b, 0, 0)),
        pl.BlockSpec((None, group, c_past.shape[2]), lambda b, g, pt: (layer * nb + b, g, 0)),
    ]
    in_specs += [pl.BlockSpec((None, None, prow, hd), page_map(r)) for r in range(group)]
    in_specs += [pl.BlockSpec((None, None, prow, hd), page_map(r)) for r in range(group)]
    return pl.pallas_call(
        kern,
        grid_spec=pltpu.PrefetchScalarGridSpec(
            num_scalar_prefetch=1,
            grid=(nb, ng),
            in_specs=in_specs,
            out_specs=[pl.BlockSpec((nt, width), lambda b, g, pt: (b, 0)),
                       pl.BlockSpec((None, nt, LANES), lambda b, g, pt: (b, 0, 0)),
                       new_rows, new_rows],
            scratch_shapes=[
                pltpu.VMEM((nrow, hd), BF16),
                pltpu.VMEM((nrow, prow), F32),
                pltpu.VMEM((nrow, 1), F32),
                pltpu.VMEM((nrow, 1), F32),
                pltpu.VMEM((nrow, hd), F32),
            ],
        ),
        out_shape=[jax.ShapeDtypeStruct((nb * nt, width), F32),
                   jax.ShapeDtypeStruct((nb, nt, LANES), F32),
                   new_shape, new_shape],
        input_output_aliases={1: 2, 2: 3} if aliased else {},
        compiler_params=_params("arbitrary", "arbitrary"),
        name="fox_sample",
    )(page_table, *(prev if aliased else ()), zf, zf, zf, ff, b_fox_pad, ctot, c_past,
      *([cache_k] * group), *([cache_v] * group))


def _hgrn_operators(c, real):
    t = np.arange(c)[:, None]
    j = np.arange(c)[None, :]
    mats = [(j <= t)]
    masks = []
    m = 1
    while m < real:
        blk = t // (2 * m)
        mid = blk * (2 * m) + m - 1
        second = t > mid
        if m < SUBLANES:
            mats.append(np.where(second, (j > mid) & (j <= t), (j > t) & (j <= mid)))
        s = np.arange(c)[None, :]
        masks.append((blk == (s // (2 * m))) & second & (s <= (s // (2 * m)) * (2 * m) + m - 1))
        m *= 2
    a = np.concatenate(mats, axis=0).astype(np.float32)
    a3 = np.concatenate([a, a, a], axis=1)
    return jnp.asarray(a3, BF16), jnp.asarray(np.stack(masks).astype(np.float32)), len(masks)


def _hgrn_kernel(q_ref, f_ref, i_ref, g_ref, lbraw_ref, gn_ref, a3_ref, msk_ref, s0_ref,
                 o_ref, sfin_ref, st_ref, *, layer, heads, dk, n_lev, real):
    c = HGRN_CHUNK
    ci = pl.program_id(1)
    nc = pl.num_programs(1)

    @pl.when(ci == 0)
    def _():
        for h in range(heads):
            st_ref[h] = s0_ref[h].T

    raw = lbraw_ref[...]
    e = jnp.exp(raw - jnp.max(raw, axis=0, keepdims=True))
    sm = e / jnp.sum(e, axis=0, keepdims=True)
    cs = sm[0:1]
    for r in range(1, layer + 1):
        cs = cs + sm[r:r + 1]
    lb = cs - sm[0:1]
    log_lb = jnp.log(lb)
    log_1m = jnp.log1p(-lb)

    zf = _pad_rows(f_ref[...], c)
    q = _pad_rows(q_ref[...], c)
    v = _pad_rows(i_ref[...], c)
    b = log_1m + _log_sigmoid(zf)
    mx = jnp.maximum(log_lb, b)
    g = mx + jnp.log(1.0 + jnp.exp(-jnp.abs(log_lb - b)))
    kr = (1.0 - lb) * _sigmoid(-zf)
    if real < c:
        valid = lax.broadcasted_iota(jnp.int32, g.shape, 0) < real
        g = jnp.where(valid, g, 0.0)
        kr = jnp.where(valid, kr, 0.0)

    e = _dot(a3_ref[...], jnp.concatenate(_split3(g), axis=0))
    eq = e[0:c]
    xq = jnp.exp(eq)
    xk = jnp.exp(eq[c - 1:c] - eq)
    xlast = xq[c - 1:c]
    n_mx = e.shape[0] // c - 1

    def level_factor(l):
        if l < n_mx:
            return jnp.exp(e[(1 + l) * c:(2 + l) * c])
        m = 1 << l
        mids = [jnp.broadcast_to(eq[b0 + m - 1:b0 + m, :], (2 * m, eq.shape[1]))
                for b0 in range(0, c, 2 * m)]
        r = mids[0] if len(mids) == 1 else jnp.concatenate(mids, axis=0)
        return jnp.exp(-jnp.abs(eq - r))

    xls = [level_factor(l) for l in range(n_lev)]
    qe = (q * xq).astype(BF16)
    ke = (kr * xk).astype(BF16)
    vb = v.astype(BF16)
    for h in range(heads):
        hs = slice(h * dk, (h + 1) * dk)
        st = st_ref[h]
        o = _dot_nt(qe[:, hs], st.astype(BF16))
        att = jnp.zeros((c, c), F32)
        for l in range(n_lev):
            xl = xls[l][:, hs]
            p = _dot_nt((q[:, hs] * xl).astype(BF16), (kr[:, hs] * xl).astype(BF16))
            att = att + p * msk_ref[l]
        o = o + _dot(att.astype(BF16), vb[:, hs])
        o = o + jnp.sum(q[:, hs] * kr[:, hs], axis=-1, keepdims=True) * v[:, hs]
        st_ref[h] = st * xlast[:, hs] + _dot_tn(vb[:, hs], ke[:, hs])
        on = _rms_rows(o, gn_ref[:, hs])
        gg = g_ref[:, hs]
        o_ref[:, hs] = (on[:real] * (gg * _sigmoid(gg))).astype(o_ref.dtype)

    @pl.when(ci == nc - 1)
    def _():
        for h in range(heads):
            sfin_ref[h] = st_ref[h].T


def _hgrn(zf, hgrn_lb, hgrn_norm, s0, layer, row0, n_seq, seq, heads, dk, col0, out_rows, out_dtype):
    c = HGRN_CHUNK
    real = min(seq, c)
    assert seq % real == 0
    nc = seq // real
    width = heads * dk
    rb0 = row0 // real
    a3, masks, n_lev = _hgrn_operators(c, real)
    depth = hgrn_lb.shape[0]
    blk = lambda col: pl.BlockSpec((real, width), lambda n, i: (rb0 + n * nc + i, col))
    kern = functools.partial(_hgrn_kernel, layer=layer, heads=heads, dk=dk, n_lev=n_lev, real=real)
    in_specs = [
        blk(col0), blk(col0 + 1), blk(col0 + 2), blk(col0 + 3),
        pl.BlockSpec((depth, width), lambda n, i: (0, 0)),
        pl.BlockSpec((None, 1, width), lambda n, i: (layer, 0, 0)),
        pl.BlockSpec(a3.shape, lambda n, i: (0, 0)),
        pl.BlockSpec(masks.shape, lambda n, i: (0, 0, 0)),
        pl.BlockSpec((None, heads, dk, dk), lambda n, i: (n, 0, 0, 0)),
    ]
    return pl.pallas_call(
        kern,
        grid=(n_seq, nc),
        in_specs=in_specs,
        out_specs=[
            pl.BlockSpec((real, width), lambda n, i: (n * nc + i, 0)),
            pl.BlockSpec((None, heads, dk, dk), lambda n, i: (n, 0, 0, 0)),
        ],
        out_shape=[
            jax.ShapeDtypeStruct((out_rows, width), out_dtype),
            jax.ShapeDtypeStruct((n_seq, heads, dk, dk), F32),
        ],
        scratch_shapes=[pltpu.VMEM((heads, dk, dk), F32)],
        compiler_params=_params("arbitrary", "arbitrary"),
        name="hgrn",
    )(zf, zf, zf, zf, hgrn_lb, hgrn_norm, a3, masks, s0)


def _patch_rows_kernel(*refs):
    n = len(refs) // 3
    for small_ref, out_ref in zip(refs[n:2 * n], refs[2 * n:]):
        out_ref[...] = small_ref[...].astype(out_ref.dtype)


def _patch_rows(bigs, smalls, row0):
    rows = smalls[0].shape[0]
    assert row0 % rows == 0
    n = len(bigs)
    return pl.pallas_call(
        _patch_rows_kernel,
        grid=(1,),
        in_specs=[_any_spec()] * n + [pl.BlockSpec(s.shape, lambda i: (0, 0)) for s in smalls],
        out_specs=[pl.BlockSpec((rows, b.shape[1]), lambda i: (row0 // rows, 0)) for b in bigs],
        out_shape=[jax.ShapeDtypeStruct(b.shape, b.dtype) for b in bigs],
        input_output_aliases={i: i for i in range(n)},
        compiler_params=_params("arbitrary"),
        name="patch_rows",
    )(*bigs, *smalls)


def _merge_kernel(ma_ref, ob_ref, oc_ref, wa_ref, wb_ref, wc_ref, ga_ref, gb_ref, gc_ref, mix_ref):
    ya = _dot(ma_ref[...], wa_ref[...])
    yb = _dot(ob_ref[...], wb_ref[...])
    yc = _dot(oc_ref[...], wc_ref[...])
    gate = lambda r: _sigmoid(r[...].astype(F32))
    mix = gate(ga_ref) * ya + gate(gb_ref) * yb + gate(gc_ref) * yc
    mix_ref[...] = mix.astype(mix_ref.dtype)


def _merge(m_a, o_b, o_c, w_a, w_b, w_c, zg, layer, tm, tn):
    m, k = m_a.shape
    d = w_a.shape[2]
    nj = d // tn
    act = pl.BlockSpec((tm, k), lambda i, j: (i, 0))
    wsp = pl.BlockSpec((None, k, tn), lambda i, j: (layer, 0, j))
    gate = lambda r: pl.BlockSpec((tm, tn), lambda i, j: (i, r * nj + j))
    return pl.pallas_call(
        _merge_kernel,
        grid=(m // tm, nj),
        in_specs=[act, act, act, wsp, wsp, wsp, gate(0), gate(1), gate(2)],
        out_specs=pl.BlockSpec((tm, tn), lambda i, j: (i, j)),
        out_shape=jax.ShapeDtypeStruct((m, d), BF16),
        compiler_params=_params("arbitrary", "arbitrary"),
        name="merge",
    )(m_a, o_b, o_c, w_a, w_b, w_c, zg, zg, zg)


def _residual_matmul_kernel(a_ref, w_ref, x_ref, o_ref):
    o_ref[...] = x_ref[...] + _dot(a_ref[...], w_ref[...])


def _weight_spec(shape, index_map, resident):
    if resident:
        return pl.BlockSpec(shape, index_map, pipeline_mode=pl.Buffered(1))
    return pl.BlockSpec(shape, index_map)


def _residual_matmul(a, w, x, layer, tm, tn):
    m, k = a.shape
    d = w.shape[2]
    return pl.pallas_call(
        _residual_matmul_kernel,
        grid=(m // tm, d // tn),
        in_specs=[
            pl.BlockSpec((tm, k), lambda i, j: (i, 0)),
            _weight_spec((None, k, tn), lambda i, j: (layer, 0, j), tn == d),
            pl.BlockSpec((tm, tn), lambda i, j: (i, j)),
        ],
        out_specs=pl.BlockSpec((tm, tn), lambda i, j: (i, j)),
        out_shape=jax.ShapeDtypeStruct((m, d), F32),
        compiler_params=_params("arbitrary", "arbitrary"),
        name="residual_matmul",
    )(a, w, x)


def _ffn_kernel(x_ref, g_ref, wg_ref, wu_ref, wd_ref, o_ref, h_ref):
    @pl.when(pl.program_id(1) == 0)
    def _():
        x = x_ref[...]
        h_ref[...] = _rms_rows(x, g_ref[...]).astype(BF16)
        o_ref[...] = x

    h = h_ref[...]
    gate = _dot(h, wg_ref[...])
    up = _dot(h, wu_ref[...])
    a = (gate * _sigmoid(gate) * up).astype(BF16)
    o_ref[...] += _dot(a, wd_ref[...])


def _ffn(x, norm, w_gate_up, w_down, layer, tm, tf):
    m, d = x.shape
    f = w_down.shape[1]
    nj = f // tf
    return pl.pallas_call(
        _ffn_kernel,
        grid=(m // tm, nj),
        in_specs=[
            pl.BlockSpec((tm, d), lambda i, j: (i, 0)),
            pl.BlockSpec((None, 1, d), lambda i, j: (layer, 0, 0)),
            pl.BlockSpec((None, d, tf), lambda i, j: (layer, 0, j)),
            pl.BlockSpec((None, d, tf), lambda i, j: (layer, 0, nj + j)),
            pl.BlockSpec((None, tf, d), lambda i, j: (layer, j, 0)),
        ],
        out_specs=pl.BlockSpec((tm, d), lambda i, j: (i, 0)),
        out_shape=jax.ShapeDtypeStruct((m, d), F32),
        scratch_shapes=[pltpu.VMEM((tm, d), BF16)],
        compiler_params=_params("arbitrary", "arbitrary"),
        name="ffn",
    )(x, norm, w_gate_up, w_gate_up, w_down)


def _ple_kernel(x_ref, g_ref, wg_ref, p_ref, wp_ref, o_ref, h_ref, *, tn):
    j = pl.program_id(1)

    @pl.when(j == 0)
    def _():
        h_ref[...] = _rms_rows(x_ref[...], g_ref[...]).astype(BF16)

    gate = _sigmoid(_dot(h_ref[...], wg_ref[...]))
    emb = _dot(p_ref[...].astype(BF16), wp_ref[...])
    xs = x_ref[:, pl.ds(pl.multiple_of(j * tn, tn), tn)]
    o_ref[...] = xs + gate * emb


def _ple(x, norm, w_gate, p, w_proj, layer, tm, tn):
    m, d = x.shape
    pd = p.shape[2]
    return pl.pallas_call(
        functools.partial(_ple_kernel, tn=tn),
        grid=(m // tm, d // tn),
        in_specs=[
            pl.BlockSpec((tm, d), lambda i, j: (i, 0)),
            pl.BlockSpec((None, 1, d), lambda i, j: (layer, 0, 0)),
            _weight_spec((None, d, tn), lambda i, j: (layer, 0, j), tn == d),
            pl.BlockSpec((None, tm, pd), lambda i, j: (layer, i, 0)),
            _weight_spec((None, pd, tn), lambda i, j: (layer, 0, j), tn == d),
        ],
        out_specs=pl.BlockSpec((tm, tn), lambda i, j: (i, j)),
        out_shape=jax.ShapeDtypeStruct((m, d), F32),
        scratch_shapes=[pltpu.VMEM((tm, d), BF16)],
        compiler_params=_params("arbitrary", "arbitrary"),
        name="ple",
    )(x, norm, w_gate, p, w_proj)


def _final_norm_kernel(x_ref, g_ref, o_ref):
    o_ref[...] = _rms_rows(x_ref[...], g_ref[...])


def _final_norm(x, g, row0, rows, tile):
    d = x.shape[1]
    rb0 = row0 // tile
    return pl.pallas_call(
        _final_norm_kernel,
        grid=(rows // tile,),
        in_specs=[pl.BlockSpec((tile, d), lambda i: (rb0 + i, 0)), pl.BlockSpec((1, d), lambda i: (0, 0))],
        out_specs=pl.BlockSpec((tile, d), lambda i: (i, 0)),
        out_shape=jax.ShapeDtypeStruct((rows, d), F32),
        compiler_params=_params("arbitrary"),
        name="final_norm",
    )(x, g)


def kernel(x_prompt, x_sample, cache_k, cache_v, cache_logf, state_conv, state_hgrn, page_table,
           p_prompt, p_sample, norm1, w_in, b_fox, conv_w, hgrn_lb, hgrn_norm, w_a, w_b, w_c, w_o,
           norm2, w_gate_up, w_down, ple_norm, ple_gate, ple_proj, final_norm):
    bp, seq, d = x_prompt.shape
    db, dseq, _ = x_sample.shape
    depth = w_in.shape[0]
    heads, hd = cache_k.shape[3], cache_k.shape[4]
    page = cache_k.shape[2]
    n_pool = cache_k.shape[1]
    cw = w_a.shape[1]
    fw = w_b.shape[1]
    rw = w_c.shape[1]
    rheads = state_hgrn.shape[2]
    dk = state_hgrn.shape[3]
    assert fw == heads * hd and rw == rheads * dk and state_hgrn.shape[4] == dk and cw == fw == rw
    mp = bp * seq
    ms = db * dseq
    m = mp + ms

    o_ff = 3 * cw + 3 * fw
    w_lo = w_in[:, :, :o_ff]
    w_hi = w_in[:, :, o_ff + heads:]
    w_ff = jnp.pad(w_in[:, :, o_ff:o_ff + heads], ((0, 0), (0, 0), (0, LANES - heads))).astype(BF16)
    tn_in = fw
    assert o_ff % tn_in == 0 and (4 * rw) % tn_in == 0 and (3 * d) % tn_in == 0
    n_lo = o_ff // tn_in
    n_rec = 4 * rw // tn_in
    n_gate = 3 * d // tn_in
    qcol = 3 * cw // tn_in
    rcol = n_lo

    bf = lambda a: a.astype(BF16)
    w_a_b, w_b_b, w_c_b, w_o_b = bf(w_a), bf(w_b), bf(w_c), bf(w_o)
    w_gu_b, w_dn_b, w_pg_b, w_pp_b = bf(w_gate_up), bf(w_down), bf(ple_gate), bf(ple_proj)
    row3 = lambda a: a.reshape(a.shape[0], 1, a.shape[1])
    norm1_r, norm2_r, ple_norm_r, hnorm_r = row3(norm1), row3(norm2), row3(ple_norm), row3(hgrn_norm)
    b_fox_pad = row3(jnp.pad(b_fox, ((0, 0), (0, LANES - heads))))

    x = _stack_rows(x_prompt.reshape(1, mp, d), x_sample.reshape(1, ms, d)).reshape(m, d)
    p_all = _stack_rows(p_prompt.reshape(depth, mp, -1), p_sample.reshape(depth, ms, -1))
    ck = cache_k.reshape(depth, n_pool, page * heads, hd)
    cv = cache_v.reshape(depth, n_pool, page * heads, hd)

    n_pages = page_table.shape[1]
    group = _tile(n_pages, 16, 1)
    lf_flat = cache_logf.reshape(depth, n_pool, 1, page * heads)
    lf_pages = _gather_logf_pages(lf_flat, page_table, group)
    c_past, c_tot = _past_cumsum(lf_pages, heads)

    tm = _tile(m, 704, 16)
    tq =_tile(seq, 512, LANES)
    conv_tile = _tile(seq, 512, 8)
    zeros_conv = jnp.zeros((bp, 2, cw), F32)
    zeros_state = jnp.zeros((bp, rheads, dk, dk), F32)

    outs = {k: [] for k in ("lfp", "cp", "sp", "lfs", "cs", "ss")}
    kvp = None
    kvs = None
    for i in range(depth):
        h, ff = _norm_ff(x, norm1_r, w_ff, i, tm)
        zf = _proj(h, w_lo, i, tm, tn_in, 0, n_lo, n_lo + n_rec, 0, F32, scale_tile=qcol, scale=hd ** -0.5)
        zf = _proj(h, w_hi, i, tm, tn_in, 0, n_rec, n_lo + n_rec, n_lo, F32, base=zf)
        zg = _proj(h, w_hi, i, tm, tn_in, n_rec, n_gate, n_gate, 0, BF16)

        m_a, conv_p = _conv(zf, conv_w, zeros_conv, i, 0, bp, seq, conv_tile, cw, m, BF16)
        ma_s, conv_s = _conv(zf, conv_w, state_conv[i], i, mp, db, dseq, dseq, cw, ms, F32)

        lf_p, qaug_t, kaug, v_t, kp, vp = _logf_prompt(ff, b_fox_pad, zf, i, depth, bp, seq,
                                                       heads, hd, qcol, kvp)
        kvp = (kp, vp)
        o_b = _fox_prompt(qaug_t, kaug, v_t, m, bp, seq, heads, hd, tq)
        ob_s, lf_s, ks, vs = _fox_sample(zf, ff, b_fox_pad, c_tot, c_past, ck, cv, page_table, i,
                                         depth, mp, dseq, heads, hd, group, qcol, kvs)
        kvs = (ks, vs)

        o_c, st_p = _hgrn(zf, hgrn_lb, hnorm_r, zeros_state, i, 0, bp, seq, rheads, dk, rcol, m, BF16)
        oc_s, st_s = _hgrn(zf, hgrn_lb, hnorm_r, state_hgrn[i], i, mp, db, dseq, rheads, dk, rcol,
                           ms, F32)

        m_a, o_b, o_c = _patch_rows((m_a, o_b, o_c), (ma_s, ob_s, oc_s), mp)
        mix = _merge(m_a, o_b, o_c, w_a_b, w_b_b, w_c_b, zg, i, tm, 1024)
        x = _residual_matmul(mix, w_o_b, x, i, tm, d)

        x = _ffn(x, norm2_r, w_gu_b, w_dn_b, i, tm, _tile(w_down.shape[1], 512, 2 * LANES))
        x = _ple(x, ple_norm_r, w_pg_b, p_all, w_pp_b, i, tm, d)

        outs["lfp"].append(lf_p[:, :, :heads])
        outs["lfs"].append(lf_s[:, :, :heads])
        outs["cp"].append(conv_p)
        outs["cs"].append(conv_s)
        outs["sp"].append(st_p)
        outs["ss"].append(st_s)

    fn = final_norm.reshape(1, d)
    y_p = _final_norm(x, fn, 0, mp, _tile(mp, 1024, 8))
    y_s = _final_norm(x, fn, mp, ms, ms)
    st = lambda k: jnp.stack(outs[k])
    return (y_p.reshape(bp, seq, d), y_s.reshape(db, dseq, d),
            kp.reshape(depth, bp, seq, heads, hd), vp.reshape(depth, bp, seq, heads, hd),
            st("lfp"), st("cp"), st("sp"),
            ks.reshape(depth, db, dseq, heads, hd), vs.reshape(depth, db, dseq, heads, hd),
            st("lfs"), st("cs"), st("ss"))
```

```python
import functools

import numpy as np
import jax
import jax.numpy as jnp
from jax import lax
from jax.experimental import pallas as pl
from jax.experimental.pallas import tpu as pltpu

F32 = jnp.float32
BF16 = jnp.bfloat16
RMS_EPS = 1e-6
LANES = 128
SUBLANES = 8
HGRN_CHUNK = 128
VMEM_LIMIT = 56 * 1024 * 1024


def _tile(n, target, align):
    best = None
    for t in range(align, min(n, target) + 1, align):
        if n % t == 0:
            best = t
    assert best is not None, (n, target, align)
    return best


def _params(*sem):
    return pltpu.CompilerParams(dimension_semantics=sem, vmem_limit_bytes=VMEM_LIMIT)


def _sigmoid(x):
    return 1.0 / (1.0 + jnp.exp(-x))


def _log_sigmoid(x):
    return jnp.minimum(x, 0.0) - jnp.log(1.0 + jnp.exp(-jnp.abs(x)))


def _rms_rows(x, g):
    ms = jnp.mean(x * x, axis=-1, keepdims=True)
    return x * lax.rsqrt(ms + RMS_EPS) * g


def _split3(x):
    hi = x.astype(BF16)
    r1 = x - hi.astype(F32)
    mid = r1.astype(BF16)
    lo = (r1 - mid.astype(F32)).astype(BF16)
    return hi, mid, lo


def _dot(a, b):
    return jnp.dot(a, b, preferred_element_type=F32)


def _dot_nt(a, b):
    return lax.dot_general(a, b, (((1,), (1,)), ((), ())), preferred_element_type=F32)


def _dot_tn(a, b):
    return lax.dot_general(a, b, (((0,), (0,)), ((), ())), preferred_element_type=F32)


def _pad_rows(a, n):
    if a.shape[0] == n:
        return a
    return jnp.concatenate([a, jnp.zeros((n - a.shape[0], a.shape[1]), a.dtype)], axis=0)


def _any_spec():
    return pl.BlockSpec(memory_space=pl.ANY)


def _copy_kernel(*refs, aliased):
    a_ref, o_ref = refs[1:] if aliased else refs
    o_ref[...] = a_ref[...]


def _stack_rows(a, b):
    nl, ra, c = a.shape
    rb = b.shape[1]
    assert ra % rb == 0
    ta = _tile(ra, 1024, 8)
    shape = jax.ShapeDtypeStruct((nl, ra + rb, c), a.dtype)
    out = pl.pallas_call(
        functools.partial(_copy_kernel, aliased=False),
        grid=(nl, ra // ta),
        in_specs=[pl.BlockSpec((None, ta, c), lambda l, i: (l, i, 0))],
        out_specs=pl.BlockSpec((None, ta, c), lambda l, i: (l, i, 0)),
        out_shape=shape,
        compiler_params=_params("arbitrary", "arbitrary"),
        name="stack_rows_a",
    )(a)
    return pl.pallas_call(
        functools.partial(_copy_kernel, aliased=True),
        grid=(nl,),
        in_specs=[_any_spec(), pl.BlockSpec((None, rb, c), lambda l: (l, 0, 0))],
        out_specs=pl.BlockSpec((None, rb, c), lambda l: (l, ra // rb, 0)),
        out_shape=shape,
        input_output_aliases={0: 0},
        compiler_params=_params("arbitrary"),
        name="stack_rows_b",
    )(out, b)


def _in_proj_kernel(x_ref, g_ref, w_ref, wff_ref, zf_ref, zg_ref, ff_ref, h_ref, *, jq, jk, jg, scale):
    j = pl.program_id(1)

    @pl.when(j == 0)
    def _():
        h = _rms_rows(x_ref[...], g_ref[...]).astype(BF16)
        h_ref[...] = h
        ff_ref[...] = _dot(h, wff_ref[...])

    @pl.when((j < jq) | ((j >= jk) & (j < jg)))
    def _():
        zf_ref[...] = _dot(h_ref[...], w_ref[...])

    @pl.when((j >= jq) & (j < jk))
    def _():
        zf_ref[...] = _dot(h_ref[...], w_ref[...]) * scale

    @pl.when(j >= jg)
    def _():
        zg_ref[...] = _dot(h_ref[...], w_ref[...]).astype(BF16)


def _in_proj(x, norm, w_main, w_ff, layer, tm, tn, widths, hd):
    m, d = x.shape
    assert all(w % tn == 0 for w in widths) and sum(widths) == w_main.shape[2]
    cuts = np.cumsum([w // tn for w in widths])
    jq, jk, jg, nj = (int(c) for c in cuts)
    kern = functools.partial(_in_proj_kernel, jq=jq, jk=jk, jg=jg, scale=hd ** -0.5)
    return pl.pallas_call(
        kern,
        grid=(m // tm, nj),
        in_specs=[
            pl.BlockSpec((tm, d), lambda i, j: (i, 0)),
            pl.BlockSpec((None, 1, d), lambda i, j: (layer, 0, 0)),
            pl.BlockSpec((None, d, tn), lambda i, j: (layer, 0, j)),
            pl.BlockSpec((None, d, LANES), lambda i, j: (layer, 0, 0)),
        ],
        out_specs=[
            pl.BlockSpec((tm, tn), lambda i, j: (i, jnp.minimum(j, jg - 1))),
            pl.BlockSpec((tm, tn), lambda i, j: (i, jnp.clip(j - jg, 0, nj - jg - 1))),
            pl.BlockSpec((tm, LANES), lambda i, j: (i, 0)),
        ],
        out_shape=[
            jax.ShapeDtypeStruct((m, jg * tn), F32),
            jax.ShapeDtypeStruct((m, widths[3]), BF16),
            jax.ShapeDtypeStruct((m, LANES), F32),
        ],
        scratch_shapes=[pltpu.VMEM((tm, d), BF16)],
        compiler_params=_params("arbitrary", "arbitrary"),
        name="in_proj",
    )(x, norm, w_main, w_ff)


def _bias_operators(heads, width):
    hd = width // heads
    selq = np.zeros((3 * LANES, width), np.float32)
    selk = np.zeros((3 * LANES, width), np.float32)
    cq = np.zeros((1, width), np.float32)
    ck = np.zeros((1, width), np.float32)
    for h in range(heads):
        for r in range(3):
            selq[r * LANES + h, h * hd + r] = 1.0
            selk[r * LANES + h, h * hd + 3 + r] = -1.0
            cq[0, h * hd + 3 + r] = 1.0
            ck[0, h * hd + r] = 1.0
    return jnp.asarray(selq, BF16), jnp.asarray(selk, BF16), jnp.asarray(cq), jnp.asarray(ck)


def _logf_prompt_kernel(*refs, heads, hd, aliased):
    if aliased:
        refs = refs[2:]
    (ff_ref, b_ref, q_ref, k32_ref, v32_ref, tri_ref, selq_ref, selk_ref, cq_ref, ck_ref,
     lf_ref, qat_ref, ka_ref, vt_ref, kp_ref, vp_ref, carry_ref) = refs
    blk = ff_ref.shape[0]
    for h in range(heads):
        kp_ref[pl.ds(h, blk, stride=heads), :] = k32_ref[:, h * hd:(h + 1) * hd]
        vp_ref[pl.ds(h, blk, stride=heads), :] = v32_ref[:, h * hd:(h + 1) * hd]

    @pl.when(pl.program_id(1) == 0)
    def _():
        carry_ref[...] = jnp.zeros(carry_ref.shape, F32)

    lf = _log_sigmoid(ff_ref[...] + b_ref[...])
    lf_ref[...] = lf
    cb = _dot(tri_ref[...], jnp.concatenate(_split3(lf), axis=0)) + carry_ref[...]
    carry_ref[...] = cb[blk - 1:blk, :]
    c3 = jnp.concatenate(_split3(cb), axis=1)
    qb = _dot(c3, selq_ref[...]) + cq_ref[...]
    kb = (_dot(c3, selk_ref[...]) + ck_ref[...]).astype(BF16)
    for h in range(heads):
        hs = slice(h * hd, (h + 1) * hd)
        qat_ref[2 * h * hd:(2 * h + 1) * hd, :] = q_ref[:, hs].T.astype(BF16)
        qat_ref[(2 * h + 1) * hd:(2 * h + 2) * hd, :] = qb[:, hs].T.astype(BF16)
        vt_ref[hs, :] = v32_ref[:, hs].T.astype(BF16)
        ka_ref[:, 2 * h * hd:(2 * h + 1) * hd] = k32_ref[:, hs].astype(BF16)
        ka_ref[:, (2 * h + 1) * hd:(2 * h + 2) * hd] = kb[:, hs]


def _logf_prompt(ff, b_fox_pad, zf, layer, depth, batch, seq, heads, hd, qcol, prev):
    width = heads * hd
    blk = _tile(seq, 512, LANES)
    nb = seq // blk
    tri = np.tril(np.ones((blk, blk), np.float32))
    tri3 = jnp.asarray(np.concatenate([tri, tri, tri], axis=1), BF16)
    ops = _bias_operators(heads, width)
    full = lambda a: pl.BlockSpec(a.shape, lambda b, i: (0,) * a.ndim)
    rows = lambda c, w: pl.BlockSpec((blk, w), lambda b, i: (b * nb + i, c))
    cache = pl.BlockSpec((None, blk * heads, hd), lambda b, i: (layer, b * nb + i, 0))
    aliased = prev is not None
    in_specs = [
        rows(0, LANES),
        pl.BlockSpec((None, 1, LANES), lambda b, i: (layer, 0, 0)),
        rows(qcol, width),
        rows(qcol + 1, width),
        rows(qcol + 2, width),
        full(tri3), full(ops[0]), full(ops[1]), full(ops[2]), full(ops[3]),
    ]
    args = [ff, b_fox_pad, zf, zf, zf, tri3, *ops]
    if aliased:
        in_specs = [_any_spec(), _any_spec()] + in_specs
        args = list(prev) + args
    cache_shape = jax.ShapeDtypeStruct((depth, batch * seq * heads, hd), F32)
    return pl.pallas_call(
        functools.partial(_logf_prompt_kernel, heads=heads, hd=hd, aliased=aliased),
        grid=(batch, nb),
        in_specs=in_specs,
        out_specs=[
            pl.BlockSpec((None, blk, LANES), lambda b, i: (b, i, 0)),
            pl.BlockSpec((2 * width, blk), lambda b, i: (0, b * nb + i)),
            rows(0, 2 * width),
            pl.BlockSpec((width, blk), lambda b, i: (0, b * nb + i)),
            cache,
            cache,
        ],
        out_shape=[
            jax.ShapeDtypeStruct((batch, seq, LANES), F32),
            jax.ShapeDtypeStruct((2 * width, batch * seq), BF16),
            jax.ShapeDtypeStruct((batch * seq, 2 * width), BF16),
            jax.ShapeDtypeStruct((width, batch * seq), BF16),
            cache_shape,
            cache_shape,
        ],
        scratch_shapes=[pltpu.VMEM((1, LANES), F32)],
        input_output_aliases={0: 4, 1: 5} if aliased else {},
        compiler_params=_params("arbitrary", "arbitrary"),
        name="logf_prompt",
    )(*args)


def _conv_kernel(ab_ref, ac_ref, av_ref, w_ref, prev_ref, m_ref, newc_ref, carry_ref):
    i = pl.program_id(1)
    t = ac_ref.shape[0]

    @pl.when(i == 0)
    def _():
        carry_ref[...] = prev_ref[...]

    u = ac_ref[...] * av_ref[...]
    row = lax.broadcasted_iota(jnp.int32, u.shape, 0)
    p1 = carry_ref[1:2, :]
    p2 = carry_ref[0:1, :]
    u1 = jnp.where(row == 0, p1, pltpu.roll(u, 1, 0))
    u2 = pltpu.roll(u, 2, 0)
    u2 = jnp.where(row == 0, p2, jnp.where(row == 1, p1, u2))
    w = w_ref[...]
    conv = w[0:1, :] * u2 + w[1:2, :] * u1 + w[2:3, :] * u
    m_ref[...] = (ab_ref[...] * conv).astype(m_ref.dtype)
    last2 = u[t - 2:t, :]
    carry_ref[...] = last2
    newc_ref[...] = last2


def _conv(zf, conv_w, prev, layer, row0, n_seq, seq, tile, width, out_rows, out_dtype):
    nt = seq // tile
    rb0 = row0 // tile
    rows = lambda c: pl.BlockSpec((tile, width), lambda n, i: (rb0 + n * nt + i, c))
    return pl.pallas_call(
        _conv_kernel,
        grid=(n_seq, nt),
        in_specs=[rows(0), rows(1), rows(2),
                  pl.BlockSpec((None, 3, width), lambda n, i: (layer, 0, 0)),
                  pl.BlockSpec((None, 2, width), lambda n, i: (n, 0, 0))],
        out_specs=[
            pl.BlockSpec((tile, width), lambda n, i: (n * nt + i, 0)),
            pl.BlockSpec((None, 2, width), lambda n, i: (n, 0, 0)),
        ],
        out_shape=[
            jax.ShapeDtypeStruct((out_rows, width), out_dtype),
            jax.ShapeDtypeStruct((n_seq, 2, width), F32),
        ],
        scratch_shapes=[pltpu.VMEM((2, width), F32)],
        compiler_params=_params("arbitrary", "arbitrary"),
        name="gated_conv",
    )(zf, zf, zf, conv_w, prev)


def _fox_prompt_kernel(qat_ref, ka_ref, vt_ref, o_ref, m_ref, l_ref, acc_ref, *, heads, hd):
    qi = pl.program_id(1)
    ki = pl.program_id(2)
    tq = qat_ref.shape[1]
    tk = ka_ref.shape[0]

    @pl.when(ki == 0)
    def _():
        m_ref[...] = jnp.full(m_ref.shape, -jnp.inf, F32)
        l_ref[...] = jnp.zeros(l_ref.shape, F32)
        acc_ref[...] = jnp.zeros(acc_ref.shape, F32)

    def step(masked):
        if masked:
            krow = lax.broadcasted_iota(jnp.int32, (tk, tq), 0)
            qcol = lax.broadcasted_iota(jnp.int32, (tk, tq), 1)
            keep = krow <= qcol
        for h in range(heads):
            aug = slice(2 * h * hd, (2 * h + 2) * hd)
            st = _dot(ka_ref[:, aug], qat_ref[aug, :])
            if masked:
                st = jnp.where(keep, st, -jnp.inf)
            m_prev = m_ref[h]
            m_new = jnp.maximum(m_prev, jnp.max(st, axis=0, keepdims=True))
            alpha = jnp.exp(m_prev - m_new)
            p = jnp.exp(st - m_new)
            l_ref[h] = alpha * l_ref[h] + jnp.sum(p, axis=0, keepdims=True)
            acc_ref[h] = alpha * acc_ref[h] + _dot(vt_ref[h * hd:(h + 1) * hd, :], p.astype(BF16))
            m_ref[h] = m_new

    @pl.when(ki < qi)
    def _():
        step(False)

    @pl.when(ki == qi)
    def _():
        step(True)
        for h in range(heads):
            o_ref[:, h * hd:(h + 1) * hd] = (acc_ref[h] / l_ref[h]).T.astype(o_ref.dtype)


def _fox_prompt(qaug_t, kaug, v_t, m_rows, batch, seq, heads, hd, tq):
    nq = seq // tq
    width = heads * hd
    kern = functools.partial(_fox_prompt_kernel, heads=heads, hd=hd)
    qmap = lambda b, qi, ki: (b * nq + qi, 0)
    kmap = lambda b, qi, ki: (b * nq + jnp.minimum(ki, qi), 0)
    return pl.pallas_call(
        kern,
        grid=(batch, nq, nq),
        in_specs=[
            pl.BlockSpec((2 * width, tq), lambda b, qi, ki: (0, b * nq + qi)),
            pl.BlockSpec((tq, 2 * width), kmap),
            pl.BlockSpec((width, tq), lambda b, qi, ki: (0, b * nq + jnp.minimum(ki, qi))),
        ],
        out_specs=pl.BlockSpec((tq, width), qmap),
        out_shape=jax.ShapeDtypeStruct((m_rows, width), BF16),
        scratch_shapes=[
            pltpu.VMEM((heads, 1, tq), F32),
            pltpu.VMEM((heads, 1, tq), F32),
            pltpu.VMEM((heads, hd, tq), F32),
        ],
        compiler_params=_params("arbitrary", "arbitrary", "arbitrary"),
        name="fox_prompt",
    )(qaug_t, kaug, v_t)


def _gather_rows_kernel(pt_ref, *refs):
    n_in = len(refs) - 1
    out_ref = refs[-1]
    for r in range(n_in):
        out_ref[:, r, :] = refs[r][:, 0, :]


def _gather_logf_pages(cache_logf_flat, page_table, group):
    depth, _, _, width = cache_logf_flat.shape
    nb, n_pages = page_table.shape
    ng = n_pages // group

    def in_map(r):
        return lambda b, g, pt: (0, pt[b, g * group + r], 0, 0)

    return pl.pallas_call(
        _gather_rows_kernel,
        grid_spec=pltpu.PrefetchScalarGridSpec(
            num_scalar_prefetch=1,
            grid=(nb, ng),
            in_specs=[pl.BlockSpec((depth, None, 1, width), in_map(r)) for r in range(group)],
            out_specs=pl.BlockSpec((depth, None, group, width), lambda b, g, pt: (0, b, g, 0)),
        ),
        out_shape=jax.ShapeDtypeStruct((depth, nb, n_pages, width), F32),
        compiler_params=_params("arbitrary", "arbitrary"),
        name="gather_logf",
    )(page_table, *([cache_logf_flat] * group))


def _past_cumsum_kernel(x_ref, t3_ref, s3_ref, l3_ref, e3_ref, c_ref, tot_ref):
    x3 = jnp.concatenate(_split3(x_ref[...]), axis=1)
    y = _dot(x3, t3_ref[...])
    tot = _dot(x3, s3_ref[...])
    offs = _dot(l3_ref[...], jnp.concatenate(_split3(tot), axis=0))
    c_ref[...] = y + _dot(jnp.concatenate(_split3(offs), axis=1), e3_ref[...])
    n = tot.shape[0]
    tot_ref[...] = offs[n - 1:n, :] + tot[n - 1:n, :]


def _past_cumsum(lf_pages, heads):
    depth, nb, n_pages, width = lf_pages.shape
    pos = np.arange(width) // heads
    hd = np.arange(width) % heads
    t = ((hd[:, None] == hd[None, :]) & (pos[:, None] <= pos[None, :])).astype(np.float32)
    ssum = (hd[:, None] == np.arange(LANES)[None, :]).astype(np.float32)
    lstrict = np.tril(np.ones((n_pages, n_pages), np.float32), -1)
    e = (np.arange(LANES)[:, None] == hd[None, :]).astype(np.float32)
    t3 = jnp.asarray(np.concatenate([t, t, t], axis=0), BF16)
    s3 = jnp.asarray(np.concatenate([ssum, ssum, ssum], axis=0), BF16)
    l3 = jnp.asarray(np.concatenate([lstrict, lstrict, lstrict], axis=1), BF16)
    e3 = jnp.asarray(np.concatenate([e, e, e], axis=0), BF16)
    x = lf_pages.reshape(depth * nb, n_pages, width)
    full = lambda a: pl.BlockSpec(a.shape, lambda g: (0,) * a.ndim)
    return pl.pallas_call(
        _past_cumsum_kernel,
        grid=(depth * nb,),
        in_specs=[pl.BlockSpec((None, n_pages, width), lambda g: (g, 0, 0)),
                  full(t3), full(s3), full(l3), full(e3)],
        out_specs=[pl.BlockSpec((None, n_pages, width), lambda g: (g, 0, 0)),
                   pl.BlockSpec((None, 1, LANES), lambda g: (g, 0, 0))],
        out_shape=[jax.ShapeDtypeStruct((depth * nb, n_pages, width), F32),
                   jax.ShapeDtypeStruct((depth * nb, 1, LANES), F32)],
        compiler_params=_params("arbitrary"),
        name="past_cumsum",
    )(x, t3, s3, l3, e3)


def _fox_sample_kernel(pt_ref, *refs, group, heads, hd, nt, aliased):
    if aliased:
        refs = refs[2:]
    q_ref, kn_ref, vn_ref, ff_ref, b_ref, ctot_ref, cp_ref = refs[:7]
    refs = refs[7:]
    k_refs = refs[:group]
    v_refs = refs[group:2 * group]
    o_ref, lf_ref, ks_ref, vs_ref = refs[2 * group:2 * group + 4]
    qall_ref, cqm_ref, m_ref, l_ref, acc_ref = refs[2 * group + 4:]
    g = pl.program_id(1)
    ng = pl.num_programs(1)
    nrow = heads * nt
    prow = k_refs[0].shape[0]
    assert heads & (heads - 1) == 0 and nt & (nt - 1) == 0
    hbits = heads.bit_length() - 1
    tbits = nt.bit_length() - 1

    def head_of_col(cols):
        return jnp.bitwise_and(cols, heads - 1)

    @pl.when(g == 0)
    def _():
        lf = _log_sigmoid(ff_ref[...] + b_ref[...])
        lf_ref[...] = lf
        trow = lax.broadcasted_iota(jnp.int32, lf.shape, 0)
        tlane = lax.broadcasted_iota(jnp.int32, lf.shape, 1)
        cs = lf
        sh = 1
        while sh < nt:
            cs = cs + jnp.where(trow >= sh, pltpu.roll(cs, sh, 0), 0.0)
            sh *= 2
        cn = ctot_ref[...] + cs
        cq = jnp.concatenate([cn[:, h:h + 1] for h in range(heads)], axis=0)
        prows = lax.broadcasted_iota(jnp.int32, (nrow, prow), 0)
        pcols = lax.broadcasted_iota(jnp.int32, (nrow, prow), 1)
        cqm_ref[...] = jnp.where(head_of_col(pcols) == jnp.right_shift(prows, tbits), cq, -jnp.inf)
        qf = q_ref[...]
        qall = jnp.concatenate([qf[:, h * hd:(h + 1) * hd] for h in range(heads)], axis=0)
        qall_ref[...] = qall.astype(BF16)
        cnm = jnp.where(tlane < heads, cn, 0.0)
        ck = jnp.zeros((1, LANES), F32)
        for t in range(nt):
            rolled = pltpu.roll(cnm, t * heads, 1)
            ck = ck + jnp.sum(jnp.where(trow == t, rolled, 0.0), axis=0, keepdims=True)
        for h in range(heads):
            ks_ref[pl.ds(h, nt, stride=heads), :] = kn_ref[:, h * hd:(h + 1) * hd]
            vs_ref[pl.ds(h, nt, stride=heads), :] = vn_ref[:, h * hd:(h + 1) * hd]
        kn = _pad_rows(ks_ref[...], LANES).astype(BF16)
        vn = _pad_rows(vs_ref[...], LANES).astype(BF16)
        s = _dot_nt(qall_ref[...], kn) + cq - ck
        rows = lax.broadcasted_iota(jnp.int32, s.shape, 0)
        cols = lax.broadcasted_iota(jnp.int32, s.shape, 1)
        ok = (head_of_col(cols) == jnp.right_shift(rows, tbits))
        ok = ok & (jnp.right_shift(cols, hbits) <= jnp.bitwise_and(rows, nt - 1)) & (cols < nrow)
        s = jnp.where(ok, s, -jnp.inf)
        m0 = jnp.max(s, axis=-1, keepdims=True)
        p = jnp.exp(s - m0)
        m_ref[...] = m0
        l_ref[...] = jnp.sum(p, axis=-1, keepdims=True)
        acc_ref[...] = _dot(p.astype(BF16), vn)

    qall = qall_ref[...]
    cqm = cqm_ref[...]
    ss = []
    for r in range(group):
        kb = k_refs[r][...].astype(BF16)
        ss.append(_dot_nt(qall, kb) + (cqm - cp_ref[r:r + 1, :]))
    mx = ss[0]
    for s in ss[1:]:
        mx = jnp.maximum(mx, s)
    m_prev = m_ref[...]
    m_new = jnp.maximum(m_prev, jnp.max(mx, axis=-1, keepdims=True))
    alpha = jnp.exp(m_prev - m_new)
    acc = alpha * acc_ref[...]
    psum = jnp.zeros((nrow, prow), F32)
    for r in range(group):
        p = jnp.exp(ss[r] - m_new)
        psum = psum + p
        acc = acc + _dot(p.astype(BF16), v_refs[r][...].astype(BF16))
    l_ref[...] = alpha * l_ref[...] + jnp.sum(psum, axis=-1, keepdims=True)
    acc_ref[...] = acc
    m_ref[...] = m_new

    @pl.when(g == ng - 1)
    def _():
        o = acc_ref[...] / l_ref[...]
        for h in range(heads):
            o_ref[:, h * hd:(h + 1) * hd] = o[h * nt:(h + 1) * nt, :]


def _fox_sample(zf, ff, b_fox_pad, ctot, c_past, cache_k, cache_v, page_table, layer, depth,
                row0, nt, heads, hd, group, qcol, prev):
    nb, n_pages = page_table.shape
    prow = cache_k.shape[2]
    width = heads * hd
    ng = n_pages // group
    rb0 = row0 // nt
    nrow = heads * nt
    aliased = prev is not None

    def page_map(r):
        return lambda b, g, pt: (layer, pt[b, g * group + r], 0, 0)

    kern = functools.partial(_fox_sample_kernel, group=group, heads=heads, hd=hd, nt=nt, aliased=aliased)
    new_rows = pl.BlockSpec((None, nrow, hd), lambda b, g, pt: (layer, b, 0))
    new_shape = jax.ShapeDtypeStruct((depth, nb * nrow, hd), F32)
    in_specs = ([_any_spec(), _any_spec()] if aliased else []) + [
        pl.BlockSpec((nt, width), lambda b, g, pt: (rb0 + b, qcol)),
        pl.BlockSpec((nt, width), lambda b, g, pt: (rb0 + b, qcol + 1)),
        pl.BlockSpec((nt, width), lambda b, g, pt: (rb0 + b, qcol + 2)),
        pl.BlockSpec((nt, LANES), lambda b, g, pt: (rb0 + b, 0)),
        pl.BlockSpec((None, 1, LANES), lambda b, g, pt: (layer, 0, 0)),
        pl.BlockSpec((None, 1, LANES), lambda b, g, pt: (layer * nb + b, 0, 0)),
        pl.BlockSpec((None, group, c_past.shape[2]), lambda b, g, pt: (layer * nb + b, g, 0)),
    ]
    in_specs += [pl.BlockSpec((None, None, prow, hd), page_map(r)) for r in range(group)]
    in_specs += [pl.BlockSpec((None, None, prow, hd), page_map(r)) for r in range(group)]
    return pl.pallas_call(
        kern,
        grid_spec=pltpu.PrefetchScalarGridSpec(
            num_scalar_prefetch=1,
            grid=(nb, ng),
            in_specs=in_specs,
            out_specs=[pl.BlockSpec((nt, width), lambda b, g, pt: (b, 0)),
                       pl.BlockSpec((None, nt, LANES), lambda b, g, pt: (b, 0, 0)),
                       new_rows, new_rows],
            scratch_shapes=[
                pltpu.VMEM((nrow, hd), BF16),
                pltpu.VMEM((nrow, prow), F32),
                pltpu.VMEM((nrow, 1), F32),
                pltpu.VMEM((nrow, 1), F32),
                pltpu.VMEM((nrow, hd), F32),
            ],
        ),
        out_shape=[jax.ShapeDtypeStruct((nb * nt, width), F32),
                   jax.ShapeDtypeStruct((nb, nt, LANES), F32),
                   new_shape, new_shape],
        input_output_aliases={1: 2, 2: 3} if aliased else {},
        compiler_params=_params("arbitrary", "arbitrary"),
        name="fox_sample",
    )(page_table, *(prev if aliased else ()), zf, zf, zf, ff, b_fox_pad, ctot, c_past,
      *([cache_k] * group), *([cache_v] * group))


def _hgrn_operators(c, real):
    t = np.arange(c)[:, None]
    j = np.arange(c)[None, :]
    mats = [(j <= t)]
    masks = []
    m = 1
    while m < real:
        blk = t // (2 * m)
        mid = blk * (2 * m) + m - 1
        second = t > mid
        if m < SUBLANES:
            mats.append(np.where(second, (j > mid) & (j <= t), (j > t) & (j <= mid)))
        s = np.arange(c)[None, :]
        masks.append((blk == (s // (2 * m))) & second & (s <= (s // (2 * m)) * (2 * m) + m - 1))
        m *= 2
    a = np.concatenate(mats, axis=0).astype(np.float32)
    a3 = np.concatenate([a, a, a], axis=1)
    return jnp.asarray(a3, BF16), jnp.asarray(np.stack(masks).astype(np.float32)), len(masks)


def _hgrn_kernel(q_ref, f_ref, i_ref, g_ref, lbraw_ref, gn_ref, a3_ref, msk_ref, s0_ref,
                 o_ref, sfin_ref, st_ref, *, layer, heads, dk, n_lev, real):
    c = HGRN_CHUNK
    ci = pl.program_id(1)
    nc = pl.num_programs(1)

    @pl.when(ci == 0)
    def _():
        for h in range(heads):
            st_ref[h] = s0_ref[h].T

    raw = lbraw_ref[...]
    e = jnp.exp(raw - jnp.max(raw, axis=0, keepdims=True))
    sm = e / jnp.sum(e, axis=0, keepdims=True)
    cs = sm[0:1]
    for r in range(1, layer + 1):
        cs = cs + sm[r:r + 1]
    lb = cs - sm[0:1]
    log_lb = jnp.log(lb)
    log_1m = jnp.log1p(-lb)

    zf = _pad_rows(f_ref[...], c)
    q = _pad_rows(q_ref[...], c)
    v = _pad_rows(i_ref[...], c)
    b = log_1m + _log_sigmoid(zf)
    mx = jnp.maximum(log_lb, b)
    g = mx + jnp.log(1.0 + jnp.exp(-jnp.abs(log_lb - b)))
    kr = (1.0 - lb) * _sigmoid(-zf)
    if real < c:
        valid = lax.broadcasted_iota(jnp.int32, g.shape, 0) < real
        g = jnp.where(valid, g, 0.0)
        kr = jnp.where(valid, kr, 0.0)

    e = _dot(a3_ref[...], jnp.concatenate(_split3(g), axis=0))
    eq = e[0:c]
    xq = jnp.exp(eq)
    xk = jnp.exp(eq[c - 1:c] - eq)
    xlast = xq[c - 1:c]
    n_mx = e.shape[0] // c - 1

    def level_factor(l):
        if l < n_mx:
            return jnp.exp(e[(1 + l) * c:(2 + l) * c])
        m = 1 << l
        mids = [jnp.broadcast_to(eq[b0 + m - 1:b0 + m, :], (2 * m, eq.shape[1]))
                for b0 in range(0, c, 2 * m)]
        r = mids[0] if len(mids) == 1 else jnp.concatenate(mids, axis=0)
        return jnp.exp(-jnp.abs(eq - r))

    xls = [level_factor(l) for l in range(n_lev)]
    qe = (q * xq).astype(BF16)
    ke = (kr * xk).astype(BF16)
    vb = v.astype(BF16)
    for h in range(heads):
        hs = slice(h * dk, (h + 1) * dk)
        st = st_ref[h]
        o = _dot_nt(qe[:, hs], st.astype(BF16))
        att = jnp.zeros((c, c), F32)
        for l in range(n_lev):
            xl = xls[l][:, hs]
            p = _dot_nt((q[:, hs] * xl).astype(BF16), (kr[:, hs] * xl).astype(BF16))
            att = att + p * msk_ref[l]
        o = o + _dot(att.astype(BF16), vb[:, hs])
        o = o + jnp.sum(q[:, hs] * kr[:, hs], axis=-1, keepdims=True) * v[:, hs]
        st_ref[h] = st * xlast[:, hs] + _dot_tn(vb[:, hs], ke[:, hs])
        on = _rms_rows(o, gn_ref[:, hs])
        gg = g_ref[:, hs]
        o_ref[:, hs] = (on[:real] * (gg * _sigmoid(gg))).astype(o_ref.dtype)

    @pl.when(ci == nc - 1)
    def _():
        for h in range(heads):
            sfin_ref[h] = st_ref[h].T


def _hgrn(zf, hgrn_lb, hgrn_norm, s0, layer, row0, n_seq, seq, heads, dk, col0, out_rows, out_dtype):
    c = HGRN_CHUNK
    real = min(seq, c)
    assert seq % real == 0
    nc = seq // real
    width = heads * dk
    rb0 = row0 // real
    a3, masks, n_lev = _hgrn_operators(c, real)
    depth = hgrn_lb.shape[0]
    blk = lambda col: pl.BlockSpec((real, width), lambda n, i: (rb0 + n * nc + i, col))
    kern = functools.partial(_hgrn_kernel, layer=layer, heads=heads, dk=dk, n_lev=n_lev, real=real)
    in_specs = [
        blk(col0), blk(col0 + 1), blk(col0 + 2), blk(col0 + 3),
        pl.BlockSpec((depth, width), lambda n, i: (0, 0)),
        pl.BlockSpec((None, 1, width), lambda n, i: (layer, 0, 0)),
        pl.BlockSpec(a3.shape, lambda n, i: (0, 0)),
        pl.BlockSpec(masks.shape, lambda n, i: (0, 0, 0)),
        pl.BlockSpec((None, heads, dk, dk), lambda n, i: (n, 0, 0, 0)),
    ]
    return pl.pallas_call(
        kern,
        grid=(n_seq, nc),
        in_specs=in_specs,
        out_specs=[
            pl.BlockSpec((real, width), lambda n, i: (n * nc + i, 0)),
            pl.BlockSpec((None, heads, dk, dk), lambda n, i: (n, 0, 0, 0)),
        ],
        out_shape=[
            jax.ShapeDtypeStruct((out_rows, width), out_dtype),
            jax.ShapeDtypeStruct((n_seq, heads, dk, dk), F32),
        ],
        scratch_shapes=[pltpu.VMEM((heads, dk, dk), F32)],
        compiler_params=_params("arbitrary", "arbitrary"),
        name="hgrn",
    )(zf, zf, zf, zf, hgrn_lb, hgrn_norm, a3, masks, s0)


def _patch_rows_kernel(*refs):
    n = len(refs) // 3
    for small_ref, out_ref in zip(refs[n:2 * n], refs[2 * n:]):
        out_ref[...] = small_ref[...].astype(out_ref.dtype)


def _patch_rows(bigs, smalls, row0):
    rows = smalls[0].shape[0]
    assert row0 % rows == 0
    n = len(bigs)
    return pl.pallas_call(
        _patch_rows_kernel,
        grid=(1,),
        in_specs=[_any_spec()] * n + [pl.BlockSpec(s.shape, lambda i: (0, 0)) for s in smalls],
        out_specs=[pl.BlockSpec((rows, b.shape[1]), lambda i: (row0 // rows, 0)) for b in bigs],
        out_shape=[jax.ShapeDtypeStruct(b.shape, b.dtype) for b in bigs],
        input_output_aliases={i: i for i in range(n)},
        compiler_params=_params("arbitrary"),
        name="patch_rows",
    )(*bigs, *smalls)


def _merge_kernel(ma_ref, ob_ref, oc_ref, wa_ref, wb_ref, wc_ref, ga_ref, gb_ref, gc_ref, mix_ref):
    ya = _dot(ma_ref[...], wa_ref[...])
    yb = _dot(ob_ref[...], wb_ref[...])
    yc = _dot(oc_ref[...], wc_ref[...])
    gate = lambda r: _sigmoid(r[...].astype(F32))
    mix = gate(ga_ref) * ya + gate(gb_ref) * yb + gate(gc_ref) * yc
    mix_ref[...] = mix.astype(mix_ref.dtype)


def _merge(m_a, o_b, o_c, w_a, w_b, w_c, zg, layer, tm, tn):
    m, k = m_a.shape
    d = w_a.shape[2]
    nj = d // tn
    act = pl.BlockSpec((tm, k), lambda i, j: (i, 0))
    wsp = pl.BlockSpec((None, k, tn), lambda i, j: (layer, 0, j))
    gate = lambda r: pl.BlockSpec((tm, tn), lambda i, j: (i, r * nj + j))
    return pl.pallas_call(
        _merge_kernel,
        grid=(m // tm, nj),
        in_specs=[act, act, act, wsp, wsp, wsp, gate(0), gate(1), gate(2)],
        out_specs=pl.BlockSpec((tm, tn), lambda i, j: (i, j)),
        out_shape=jax.ShapeDtypeStruct((m, d), BF16),
        compiler_params=_params("arbitrary", "arbitrary"),
        name="merge",
    )(m_a, o_b, o_c, w_a, w_b, w_c, zg, zg, zg)


def _residual_matmul_kernel(a_ref, w_ref, x_ref, o_ref):
    o_ref[...] = x_ref[...] + _dot(a_ref[...], w_ref[...])


def _weight_spec(shape, index_map, resident):
    if resident:
        return pl.BlockSpec(shape, index_map, pipeline_mode=pl.Buffered(1))
    return pl.BlockSpec(shape, index_map)


def _residual_matmul(a, w, x, layer, tm, tn):
    m, k = a.shape
    d = w.shape[2]
    return pl.pallas_call(
        _residual_matmul_kernel,
        grid=(m // tm, d // tn),
        in_specs=[
            pl.BlockSpec((tm, k), lambda i, j: (i, 0)),
            _weight_spec((None, k, tn), lambda i, j: (layer, 0, j), tn == d),
            pl.BlockSpec((tm, tn), lambda i, j: (i, j)),
        ],
        out_specs=pl.BlockSpec((tm, tn), lambda i, j: (i, j)),
        out_shape=jax.ShapeDtypeStruct((m, d), F32),
        compiler_params=_params("arbitrary", "arbitrary"),
        name="residual_matmul",
    )(a, w, x)


def _ffn_kernel(x_ref, g_ref, wg_ref, wu_ref, wd_ref, o_ref, h_ref):
    @pl.when(pl.program_id(1) == 0)
    def _():
        x = x_ref[...]
        h_ref[...] = _rms_rows(x, g_ref[...]).astype(BF16)
        o_ref[...] = x

    h = h_ref[...]
    gate = _dot(h, wg_ref[...])
    up = _dot(h, wu_ref[...])
    a = (gate * _sigmoid(gate) * up).astype(BF16)
    o_ref[...] += _dot(a, wd_ref[...])


def _ffn(x, norm, w_gate_up, w_down, layer, tm, tf):
    m, d = x.shape
    f = w_down.shape[1]
    nj = f // tf
    return pl.pallas_call(
        _ffn_kernel,
        grid=(m // tm, nj),
        in_specs=[
            pl.BlockSpec((tm, d), lambda i, j: (i, 0)),
            pl.BlockSpec((None, 1, d), lambda i, j: (layer, 0, 0)),
            pl.BlockSpec((None, d, tf), lambda i, j: (layer, 0, j)),
            pl.BlockSpec((None, d, tf), lambda i, j: (layer, 0, nj + j)),
            pl.BlockSpec((None, tf, d), lambda i, j: (layer, j, 0)),
        ],
        out_specs=pl.BlockSpec((tm, d), lambda i, j: (i, 0)),
        out_shape=jax.ShapeDtypeStruct((m, d), F32),
        scratch_shapes=[pltpu.VMEM((tm, d), BF16)],
        compiler_params=_params("arbitrary", "arbitrary"),
        name="ffn",
    )(x, norm, w_gate_up, w_gate_up, w_down)


def _ple_kernel(x_ref, g_ref, wg_ref, p_ref, wp_ref, o_ref, h_ref, *, tn):
    j = pl.program_id(1)

    @pl.when(j == 0)
    def _():
        h_ref[...] = _rms_rows(x_ref[...], g_ref[...]).astype(BF16)

    gate = _sigmoid(_dot(h_ref[...], wg_ref[...]))
    emb = _dot(p_ref[...].astype(BF16), wp_ref[...])
    xs = x_ref[:, pl.ds(pl.multiple_of(j * tn, tn), tn)]
    o_ref[...] = xs + gate * emb


def _ple(x, norm, w_gate, p, w_proj, layer, tm, tn):
    m, d = x.shape
    pd = p.shape[2]
    return pl.pallas_call(
        functools.partial(_ple_kernel, tn=tn),
        grid=(m // tm, d // tn),
        in_specs=[
            pl.BlockSpec((tm, d), lambda i, j: (i, 0)),
            pl.BlockSpec((None, 1, d), lambda i, j: (layer, 0, 0)),
            _weight_spec((None, d, tn), lambda i, j: (layer, 0, j), tn == d),
            pl.BlockSpec((None, tm, pd), lambda i, j: (layer, i, 0)),
            _weight_spec((None, pd, tn), lambda i, j: (layer, 0, j), tn == d),
        ],
        out_specs=pl.BlockSpec((tm, tn), lambda i, j: (i, j)),
        out_shape=jax.ShapeDtypeStruct((m, d), F32),
        scratch_shapes=[pltpu.VMEM((tm, d), BF16)],
        compiler_params=_params("arbitrary", "arbitrary"),
        name="ple",
    )(x, norm, w_gate, p, w_proj)


def _final_norm_kernel(x_ref, g_ref, o_ref):
    o_ref[...] = _rms_rows(x_ref[...], g_ref[...])


def _final_norm(x, g, row0, rows, tile):
    d = x.shape[1]
    rb0 = row0 // tile
    return pl.pallas_call(
        _final_norm_kernel,
        grid=(rows // tile,),
        in_specs=[pl.BlockSpec((tile, d), lambda i: (rb0 + i, 0)), pl.BlockSpec((1, d), lambda i: (0, 0))],
        out_specs=pl.BlockSpec((tile, d), lambda i: (i, 0)),
        out_shape=jax.ShapeDtypeStruct((rows, d), F32),
        compiler_params=_params("arbitrary"),
        name="final_norm",
    )(x, g)


def kernel(x_prompt, x_sample, cache_k, cache_v, cache_logf, state_conv, state_hgrn, page_table,
           p_prompt, p_sample, norm1, w_in, b_fox, conv_w, hgrn_lb, hgrn_norm, w_a, w_b, w_c, w_o,
           norm2, w_gate_up, w_down, ple_norm, ple_gate, ple_proj, final_norm):
    bp, seq, d = x_prompt.shape
    db, dseq, _ = x_sample.shape
    depth = w_in.shape[0]
    heads, hd = cache_k.shape[3], cache_k.shape[4]
    page = cache_k.shape[2]
    n_pool = cache_k.shape[1]
    cw = w_a.shape[1]
    fw = w_b.shape[1]
    rw = w_c.shape[1]
    rheads = state_hgrn.shape[2]
    dk = state_hgrn.shape[3]
    assert fw == heads * hd and rw == rheads * dk and state_hgrn.shape[4] == dk and cw == fw == rw
    mp = bp * seq
    ms = db * dseq
    m = mp + ms

    o_ff = 3 * cw + 3 * fw
    widths = (3 * cw, fw, 2 * fw + 4 * rw, 3 * d)
    w_main = jnp.concatenate([w_in[:, :, :o_ff], w_in[:, :, o_ff + heads:]], axis=2).astype(BF16)
    w_ff = jnp.pad(w_in[:, :, o_ff:o_ff + heads], ((0, 0), (0, 0), (0, LANES - heads))).astype(BF16)
    assert w_main.shape[2] == sum(widths)
    tn_in = fw
    qcol = 3 * cw // tn_in
    rcol = qcol + 3

    bf = lambda a: a.astype(BF16)
    w_a_b, w_b_b, w_c_b, w_o_b = bf(w_a), bf(w_b), bf(w_c), bf(w_o)
    w_gu_b, w_dn_b, w_pg_b, w_pp_b = bf(w_gate_up), bf(w_down), bf(ple_gate), bf(ple_proj)
    row3 = lambda a: a.reshape(a.shape[0], 1, a.shape[1])
    norm1_r, norm2_r, ple_norm_r, hnorm_r = row3(norm1), row3(norm2), row3(ple_norm), row3(hgrn_norm)
    b_fox_pad = row3(jnp.pad(b_fox, ((0, 0), (0, LANES - heads))))

    x = _stack_rows(x_prompt.reshape(1, mp, d), x_sample.reshape(1, ms, d)).reshape(m, d)
    p_all = _stack_rows(p_prompt.reshape(depth, mp, -1), p_sample.reshape(depth, ms, -1))
    ck = cache_k.reshape(depth, n_pool, page * heads, hd)
    cv = cache_v.reshape(depth, n_pool, page * heads, hd)

    n_pages = page_table.shape[1]
    group = _tile(n_pages, 16, 1)
    lf_flat = cache_logf.reshape(depth, n_pool, 1, page * heads)
    lf_pages = _gather_logf_pages(lf_flat, page_table, group)
    c_past, c_tot = _past_cumsum(lf_pages, heads)

    tm = _tile(m, 704, 16)
    tq =_tile(seq, 512, LANES)
    conv_tile = _tile(seq, 512, 8)
    zeros_conv = jnp.zeros((bp, 2, cw), F32)
    zeros_state = jnp.zeros((bp, rheads, dk, dk), F32)

    outs = {k: [] for k in ("lfp", "cp", "sp", "lfs", "cs", "ss")}
    kvp = None
    kvs = None
    for i in range(depth):
        zf, zg, ff = _in_proj(x, norm1_r, w_main, w_ff, i, tm, tn_in, widths, hd)

        m_a, conv_p = _conv(zf, conv_w, zeros_conv, i, 0, bp, seq, conv_tile, cw, m, BF16)
        ma_s, conv_s = _conv(zf, conv_w, state_conv[i], i, mp, db, dseq, dseq, cw, ms, F32)

        lf_p, qaug_t, kaug, v_t, kp, vp = _logf_prompt(ff, b_fox_pad, zf, i, depth, bp, seq,
                                                       heads, hd, qcol, kvp)
        kvp = (kp, vp)
        o_b = _fox_prompt(qaug_t, kaug, v_t, m, bp, seq, heads, hd, tq)
        ob_s, lf_s, ks, vs = _fox_sample(zf, ff, b_fox_pad, c_tot, c_past, ck, cv, page_table, i,
                                         depth, mp, dseq, heads, hd, group, qcol, kvs)
        kvs = (ks, vs)

        o_c, st_p = _hgrn(zf, hgrn_lb, hnorm_r, zeros_state, i, 0, bp, seq, rheads, dk, rcol, m, BF16)
        oc_s, st_s = _hgrn(zf, hgrn_lb, hnorm_r, state_hgrn[i], i, mp, db, dseq, rheads, dk, rcol,
                           ms, F32)

        m_a, o_b, o_c = _patch_rows((m_a, o_b, o_c), (ma_s, ob_s, oc_s), mp)
        mix = _merge(m_a, o_b, o_c, w_a_b, w_b_b, w_c_b, zg, i, tm, 1024)
        x = _residual_matmul(mix, w_o_b, x, i, tm, d)

        x = _ffn(x, norm2_r, w_gu_b, w_dn_b, i, tm, _tile(w_down.shape[1], 512, 2 * LANES))
        x = _ple(x, ple_norm_r, w_pg_b, p_all, w_pp_b, i, tm, d)

        outs["lfp"].append(lf_p[:, :, :heads])
        outs["lfs"].append(lf_s[:, :, :heads])
        outs["cp"].append(conv_p)
        outs["cs"].append(conv_s)
        outs["sp"].append(st_p)
        outs["ss"].append(st_s)

    fn = final_norm.reshape(1, d)
    y_p = _final_norm(x, fn, 0, mp, _tile(mp, 1024, 8))
    y_s = _final_norm(x, fn, mp, ms, ms)
    st = lambda k: jnp.stack(outs[k])
    return (y_p.reshape(bp, seq, d), y_s.reshape(db, dseq, d),
            kp.reshape(depth, bp, seq, heads, hd), vp.reshape(depth, bp, seq, heads, hd),
            st("lfp"), st("cp"), st("sp"),
            ks.reshape(depth, db, dseq, heads, hd), vs.reshape(depth, db, dseq, heads, hd),
            st("lfs"), st("cs"), st("ss"))
```

```python
import functools

import numpy as np
import jax
import jax.numpy as jnp
from jax import lax
from jax.experimental import pallas as pl
from jax.experimental.pallas import tpu as pltpu

F32 = jnp.float32
BF16 = jnp.bfloat16
RMS_EPS = 1e-6
LANES = 128
SUBLANES = 8
HGRN_CHUNK = 128
VMEM_LIMIT = 56 * 1024 * 1024


def _tile(n, target, align):
    best = None
    for t in range(align, min(n, target) + 1, align):
        if n % t == 0:
            best = t
    assert best is not None, (n, target, align)
    return best


def _params(*sem):
    return pltpu.CompilerParams(dimension_semantics=sem, vmem_limit_bytes=VMEM_LIMIT)


def _sigmoid(x):
    return 1.0 / (1.0 + jnp.exp(-x))


def _log_sigmoid(x):
    return jnp.minimum(x, 0.0) - jnp.log(1.0 + jnp.exp(-jnp.abs(x)))


def _rms_rows(x, g):
    ms = jnp.mean(x * x, axis=-1, keepdims=True)
    return x * lax.rsqrt(ms + RMS_EPS) * g


def _split3(x):
    hi = x.astype(BF16)
    r1 = x - hi.astype(F32)
    mid = r1.astype(BF16)
    lo = (r1 - mid.astype(F32)).astype(BF16)
    return hi, mid, lo


def _dot(a, b):
    return jnp.dot(a, b, preferred_element_type=F32)


def _dot_nt(a, b):
    return lax.dot_general(a, b, (((1,), (1,)), ((), ())), preferred_element_type=F32)


def _dot_tn(a, b):
    return lax.dot_general(a, b, (((0,), (0,)), ((), ())), preferred_element_type=F32)


def _pad_rows(a, n):
    if a.shape[0] == n:
        return a
    return jnp.concatenate([a, jnp.zeros((n - a.shape[0], a.shape[1]), a.dtype)], axis=0)


def _any_spec():
    return pl.BlockSpec(memory_space=pl.ANY)


def _copy_kernel(*refs, aliased):
    a_ref, o_ref = refs[1:] if aliased else refs
    o_ref[...] = a_ref[...]


def _stack_rows(a, b):
    nl, ra, c = a.shape
    rb = b.shape[1]
    assert ra % rb == 0
    ta = _tile(ra, 1024, 8)
    shape = jax.ShapeDtypeStruct((nl, ra + rb, c), a.dtype)
    out = pl.pallas_call(
        functools.partial(_copy_kernel, aliased=False),
        grid=(nl, ra // ta),
        in_specs=[pl.BlockSpec((None, ta, c), lambda l, i: (l, i, 0))],
        out_specs=pl.BlockSpec((None, ta, c), lambda l, i: (l, i, 0)),
        out_shape=shape,
        compiler_params=_params("arbitrary", "arbitrary"),
        name="stack_rows_a",
    )(a)
    return pl.pallas_call(
        functools.partial(_copy_kernel, aliased=True),
        grid=(nl,),
        in_specs=[_any_spec(), pl.BlockSpec((None, rb, c), lambda l: (l, 0, 0))],
        out_specs=pl.BlockSpec((None, rb, c), lambda l: (l, ra // rb, 0)),
        out_shape=shape,
        input_output_aliases={0: 0},
        compiler_params=_params("arbitrary"),
        name="stack_rows_b",
    )(out, b)


def _in_proj_kernel(x_ref, g_ref, w_ref, wff_ref, zf_ref, zg_ref, ff_ref, h_ref, *, jq, jk, jg, scale):
    j = pl.program_id(1)

    @pl.when(j == 0)
    def _():
        h = _rms_rows(x_ref[...], g_ref[...]).astype(BF16)
        h_ref[...] = h
        ff_ref[...] = _dot(h, wff_ref[...])

    @pl.when((j < jq) | ((j >= jk) & (j < jg)))
    def _():
        zf_ref[...] = _dot(h_ref[...], w_ref[...])

    @pl.when((j >= jq) & (j < jk))
    def _():
        zf_ref[...] = _dot(h_ref[...], w_ref[...]) * scale

    @pl.when(j >= jg)
    def _():
        zg_ref[...] = _dot(h_ref[...], w_ref[...]).astype(BF16)


def _in_proj(x, norm, w_main, w_ff, layer, tm, tn, widths, hd):
    m, d = x.shape
    assert all(w % tn == 0 for w in widths) and sum(widths) == w_main.shape[2]
    cuts = np.cumsum([w // tn for w in widths])
    jq, jk, jg, nj = (int(c) for c in cuts)
    kern = functools.partial(_in_proj_kernel, jq=jq, jk=jk, jg=jg, scale=hd ** -0.5)
    return pl.pallas_call(
        kern,
        grid=(m // tm, nj),
        in_specs=[
            pl.BlockSpec((tm, d), lambda i, j: (i, 0)),
            pl.BlockSpec((None, 1, d), lambda i, j: (layer, 0, 0)),
            pl.BlockSpec((None, d, tn), lambda i, j: (layer, 0, j)),
            pl.BlockSpec((None, d, LANES), lambda i, j: (layer, 0, 0)),
        ],
        out_specs=[
            pl.BlockSpec((tm, tn), lambda i, j: (i, jnp.minimum(j, jg - 1))),
            pl.BlockSpec((tm, tn), lambda i, j: (i, jnp.clip(j - jg, 0, nj - jg - 1))),
            pl.BlockSpec((tm, LANES), lambda i, j: (i, 0)),
        ],
        out_shape=[
            jax.ShapeDtypeStruct((m, jg * tn), F32),
            jax.ShapeDtypeStruct((m, widths[3]), BF16),
            jax.ShapeDtypeStruct((m, LANES), F32),
        ],
        scratch_shapes=[pltpu.VMEM((tm, d), BF16)],
        compiler_params=_params("arbitrary", "arbitrary"),
        name="in_proj",
    )(x, norm, w_main, w_ff)


def _bias_operators(heads, width):
    hd = width // heads
    selq = np.zeros((3 * LANES, width), np.float32)
    selk = np.zeros((3 * LANES, width), np.float32)
    cq = np.zeros((1, width), np.float32)
    ck = np.zeros((1, width), np.float32)
    for h in range(heads):
        for r in range(3):
            selq[r * LANES + h, h * hd + r] = 1.0
            selk[r * LANES + h, h * hd + 3 + r] = -1.0
            cq[0, h * hd + 3 + r] = 1.0
            ck[0, h * hd + r] = 1.0
    return jnp.asarray(selq, BF16), jnp.asarray(selk, BF16), jnp.asarray(cq), jnp.asarray(ck)


def _logf_prompt_kernel(*refs, heads, hd, aliased):
    if aliased:
        refs = refs[2:]
    (ff_ref, b_ref, q_ref, k32_ref, v32_ref, tri_ref, selq_ref, selk_ref, cq_ref, ck_ref,
     lf_ref, qat_ref, ka_ref, vt_ref, kp_ref, vp_ref, carry_ref) = refs
    blk = ff_ref.shape[0]
    for h in range(heads):
        kp_ref[pl.ds(h, blk, stride=heads), :] = k32_ref[:, h * hd:(h + 1) * hd]
        vp_ref[pl.ds(h, blk, stride=heads), :] = v32_ref[:, h * hd:(h + 1) * hd]

    @pl.when(pl.program_id(1) == 0)
    def _():
        carry_ref[...] = jnp.zeros(carry_ref.shape, F32)

    lf = _log_sigmoid(ff_ref[...] + b_ref[...])
    lf_ref[...] = lf
    cb = _dot(tri_ref[...], jnp.concatenate(_split3(lf), axis=0)) + carry_ref[...]
    carry_ref[...] = cb[blk - 1:blk, :]
    c3 = jnp.concatenate(_split3(cb), axis=1)
    qb = _dot(c3, selq_ref[...]) + cq_ref[...]
    kb = (_dot(c3, selk_ref[...]) + ck_ref[...]).astype(BF16)
    for h in range(heads):
        hs = slice(h * hd, (h + 1) * hd)
        qat_ref[2 * h * hd:(2 * h + 1) * hd, :] = q_ref[:, hs].T.astype(BF16)
        qat_ref[(2 * h + 1) * hd:(2 * h + 2) * hd, :] = qb[:, hs].T.astype(BF16)
        vt_ref[hs, :] = v32_ref[:, hs].T.astype(BF16)
        ka_ref[:, 2 * h * hd:(2 * h + 1) * hd] = k32_ref[:, hs].astype(BF16)
        ka_ref[:, (2 * h + 1) * hd:(2 * h + 2) * hd] = kb[:, hs]


def _logf_prompt(ff, b_fox_pad, zf, layer, depth, batch, seq, heads, hd, qcol, prev):
    width = heads * hd
    blk = _tile(seq, 512, LANES)
    nb = seq // blk
    tri = np.tril(np.ones((blk, blk), np.float32))
    tri3 = jnp.asarray(np.concatenate([tri, tri, tri], axis=1), BF16)
    ops = _bias_operators(heads, width)
    full = lambda a: pl.BlockSpec(a.shape, lambda b, i: (0,) * a.ndim)
    rows = lambda c, w: pl.BlockSpec((blk, w), lambda b, i: (b * nb + i, c))
    cache = pl.BlockSpec((None, blk * heads, hd), lambda b, i: (layer, b * nb + i, 0))
    aliased = prev is not None
    in_specs = [
        rows(0, LANES),
        pl.BlockSpec((None, 1, LANES), lambda b, i: (layer, 0, 0)),
        rows(qcol, width),
        rows(qcol + 1, width),
        rows(qcol + 2, width),
        full(tri3), full(ops[0]), full(ops[1]), full(ops[2]), full(ops[3]),
    ]
    args = [ff, b_fox_pad, zf, zf, zf, tri3, *ops]
    if aliased:
        in_specs = [_any_spec(), _any_spec()] + in_specs
        args = list(prev) + args
    cache_shape = jax.ShapeDtypeStruct((depth, batch * seq * heads, hd), F32)
    return pl.pallas_call(
        functools.partial(_logf_prompt_kernel, heads=heads, hd=hd, aliased=aliased),
        grid=(batch, nb),
        in_specs=in_specs,
        out_specs=[
            pl.BlockSpec((None, blk, LANES), lambda b, i: (b, i, 0)),
            pl.BlockSpec((2 * width, blk), lambda b, i: (0, b * nb + i)),
            rows(0, 2 * width),
            pl.BlockSpec((width, blk), lambda b, i: (0, b * nb + i)),
            cache,
            cache,
        ],
        out_shape=[
            jax.ShapeDtypeStruct((batch, seq, LANES), F32),
            jax.ShapeDtypeStruct((2 * width, batch * seq), BF16),
            jax.ShapeDtypeStruct((batch * seq, 2 * width), BF16),
            jax.ShapeDtypeStruct((width, batch * seq), BF16),
            cache_shape,
            cache_shape,
        ],
        scratch_shapes=[pltpu.VMEM((1, LANES), F32)],
        input_output_aliases={0: 4, 1: 5} if aliased else {},
        compiler_params=_params("arbitrary", "arbitrary"),
        name="logf_prompt",
    )(*args)


def _conv_kernel(ab_ref, ac_ref, av_ref, w_ref, prev_ref, m_ref, newc_ref, carry_ref):
    i = pl.program_id(1)
    t = ac_ref.shape[0]

    @pl.when(i == 0)
    def _():
        carry_ref[...] = prev_ref[...]

    u = ac_ref[...] * av_ref[...]
    row = lax.broadcasted_iota(jnp.int32, u.shape, 0)
    p1 = carry_ref[1:2, :]
    p2 = carry_ref[0:1, :]
    u1 = jnp.where(row == 0, p1, pltpu.roll(u, 1, 0))
    u2 = pltpu.roll(u, 2, 0)
    u2 = jnp.where(row == 0, p2, jnp.where(row == 1, p1, u2))
    w = w_ref[...]
    conv = w[0:1, :] * u2 + w[1:2, :] * u1 + w[2:3, :] * u
    m_ref[...] = (ab_ref[...] * conv).astype(m_ref.dtype)
    last2 = u[t - 2:t, :]
    carry_ref[...] = last2
    newc_ref[...] = last2


def _conv(zf, conv_w, prev, layer, row0, n_seq, seq, tile, width, out_rows, out_dtype):
    nt = seq // tile
    rb0 = row0 // tile
    rows = lambda c: pl.BlockSpec((tile, width), lambda n, i: (rb0 + n * nt + i, c))
    return pl.pallas_call(
        _conv_kernel,
        grid=(n_seq, nt),
        in_specs=[rows(0), rows(1), rows(2),
                  pl.BlockSpec((None, 3, width), lambda n, i: (layer, 0, 0)),
                  pl.BlockSpec((None, 2, width), lambda n, i: (n, 0, 0))],
        out_specs=[
            pl.BlockSpec((tile, width), lambda n, i: (n * nt + i, 0)),
            pl.BlockSpec((None, 2, width), lambda n, i: (n, 0, 0)),
        ],
        out_shape=[
            jax.ShapeDtypeStruct((out_rows, width), out_dtype),
            jax.ShapeDtypeStruct((n_seq, 2, width), F32),
        ],
        scratch_shapes=[pltpu.VMEM((2, width), F32)],
        compiler_params=_params("arbitrary", "arbitrary"),
        name="gated_conv",
    )(zf, zf, zf, conv_w, prev)


def _fox_prompt_kernel(qat_ref, ka_ref, vt_ref, o_ref, m_ref, l_ref, acc_ref, *, heads, hd):
    qi = pl.program_id(1)
    ki = pl.program_id(2)
    tq = qat_ref.shape[1]
    tk = ka_ref.shape[0]

    @pl.when(ki == 0)
    def _():
        m_ref[...] = jnp.full(m_ref.shape, -jnp.inf, F32)
        l_ref[...] = jnp.zeros(l_ref.shape, F32)
        acc_ref[...] = jnp.zeros(acc_ref.shape, F32)

    def step(masked):
        if masked:
            krow = lax.broadcasted_iota(jnp.int32, (tk, tq), 0)
            qcol = lax.broadcasted_iota(jnp.int32, (tk, tq), 1)
            keep = krow <= qcol
        for h in range(heads):
            aug = slice(2 * h * hd, (2 * h + 2) * hd)
            st = _dot(ka_ref[:, aug], qat_ref[aug, :])
            if masked:
                st = jnp.where(keep, st, -jnp.inf)
            m_prev = m_ref[h]
            m_new = jnp.maximum(m_prev, jnp.max(st, axis=0, keepdims=True))
            alpha = jnp.exp(m_prev - m_new)
            p = jnp.exp(st - m_new)
            l_ref[h] = alpha * l_ref[h] + jnp.sum(p, axis=0, keepdims=True)
            acc_ref[h] = alpha * acc_ref[h] + _dot(vt_ref[h * hd:(h + 1) * hd, :], p.astype(BF16))
            m_ref[h] = m_new

    @pl.when(ki < qi)
    def _():
        step(False)

    @pl.when(ki == qi)
    def _():
        step(True)
        for h in range(heads):
            o_ref[:, h * hd:(h + 1) * hd] = (acc_ref[h] / l_ref[h]).T.astype(o_ref.dtype)


def _fox_prompt(qaug_t, kaug, v_t, m_rows, batch, seq, heads, hd, tq):
    nq = seq // tq
    width = heads * hd
    kern = functools.partial(_fox_prompt_kernel, heads=heads, hd=hd)
    qmap = lambda b, qi, ki: (b * nq + qi, 0)
    kmap = lambda b, qi, ki: (b * nq + jnp.minimum(ki, qi), 0)
    return pl.pallas_call(
        kern,
        grid=(batch, nq, nq),
        in_specs=[
            pl.BlockSpec((2 * width, tq), lambda b, qi, ki: (0, b * nq + qi)),
            pl.BlockSpec((tq, 2 * width), kmap),
            pl.BlockSpec((width, tq), lambda b, qi, ki: (0, b * nq + jnp.minimum(ki, qi))),
        ],
        out_specs=pl.BlockSpec((tq, width), qmap),
        out_shape=jax.ShapeDtypeStruct((m_rows, width), BF16),
        scratch_shapes=[
            pltpu.VMEM((heads, 1, tq), F32),
            pltpu.VMEM((heads, 1, tq), F32),
            pltpu.VMEM((heads, hd, tq), F32),
        ],
        compiler_params=_params("arbitrary", "arbitrary", "arbitrary"),
        name="fox_prompt",
    )(qaug_t, kaug, v_t)


def _gather_rows_kernel(pt_ref, *refs):
    n_in = len(refs) - 1
    out_ref = refs[-1]
    for r in range(n_in):
        out_ref[:, r, :] = refs[r][:, 0, :]


def _gather_logf_pages(cache_logf_flat, page_table, group):
    depth, _, _, width = cache_logf_flat.shape
    nb, n_pages = page_table.shape
    ng = n_pages // group

    def in_map(r):
        return lambda b, g, pt: (0, pt[b, g * group + r], 0, 0)

    return pl.pallas_call(
        _gather_rows_kernel,
        grid_spec=pltpu.PrefetchScalarGridSpec(
            num_scalar_prefetch=1,
            grid=(nb, ng),
            in_specs=[pl.BlockSpec((depth, None, 1, width), in_map(r)) for r in range(group)],
            out_specs=pl.BlockSpec((depth, None, group, width), lambda b, g, pt: (0, b, g, 0)),
        ),
        out_shape=jax.ShapeDtypeStruct((depth, nb, n_pages, width), F32),
        compiler_params=_params("arbitrary", "arbitrary"),
        name="gather_logf",
    )(page_table, *([cache_logf_flat] * group))


def _past_cumsum_kernel(x_ref, t3_ref, s3_ref, l3_ref, e3_ref, c_ref, tot_ref):
    x3 = jnp.concatenate(_split3(x_ref[...]), axis=1)
    y = _dot(x3, t3_ref[...])
    tot = _dot(x3, s3_ref[...])
    offs = _dot(l3_ref[...], jnp.concatenate(_split3(tot), axis=0))
    c_ref[...] = y + _dot(jnp.concatenate(_split3(offs), axis=1), e3_ref[...])
    n = tot.shape[0]
    tot_ref[...] = offs[n - 1:n, :] + tot[n - 1:n, :]


def _past_cumsum(lf_pages, heads):
    depth, nb, n_pages, width = lf_pages.shape
    pos = np.arange(width) // heads
    hd = np.arange(width) % heads
    t = ((hd[:, None] == hd[None, :]) & (pos[:, None] <= pos[None, :])).astype(np.float32)
    ssum = (hd[:, None] == np.arange(LANES)[None, :]).astype(np.float32)
    lstrict = np.tril(np.ones((n_pages, n_pages), np.float32), -1)
    e = (np.arange(LANES)[:, None] == hd[None, :]).astype(np.float32)
    t3 = jnp.asarray(np.concatenate([t, t, t], axis=0), BF16)
    s3 = jnp.asarray(np.concatenate([ssum, ssum, ssum], axis=0), BF16)
    l3 = jnp.asarray(np.concatenate([lstrict, lstrict, lstrict], axis=1), BF16)
    e3 = jnp.asarray(np.concatenate([e, e, e], axis=0), BF16)
    x = lf_pages.reshape(depth * nb, n_pages, width)
    full = lambda a: pl.BlockSpec(a.shape, lambda g: (0,) * a.ndim)
    return pl.pallas_call(
        _past_cumsum_kernel,
        grid=(depth * nb,),
        in_specs=[pl.BlockSpec((None, n_pages, width), lambda g: (g, 0, 0)),
                  full(t3), full(s3), full(l3), full(e3)],
        out_specs=[pl.BlockSpec((None, n_pages, width), lambda g: (g, 0, 0)),
                   pl.BlockSpec((None, 1, LANES), lambda g: (g, 0, 0))],
        out_shape=[jax.ShapeDtypeStruct((depth * nb, n_pages, width), F32),
                   jax.ShapeDtypeStruct((depth * nb, 1, LANES), F32)],
        compiler_params=_params("arbitrary"),
        name="past_cumsum",
    )(x, t3, s3, l3, e3)


def _fox_sample_kernel(pt_ref, *refs, group, heads, hd, nt, aliased):
    if aliased:
        refs = refs[2:]
    q_ref, kn_ref, vn_ref, ff_ref, b_ref, ctot_ref, cp_ref = refs[:7]
    refs = refs[7:]
    k_refs = refs[:group]
    v_refs = refs[group:2 * group]
    o_ref, lf_ref, ks_ref, vs_ref = refs[2 * group:2 * group + 4]
    qall_ref, cqm_ref, m_ref, l_ref, acc_ref = refs[2 * group + 4:]
    g = pl.program_id(1)
    ng = pl.num_programs(1)
    nrow = heads * nt
    prow = k_refs[0].shape[0]
    assert heads & (heads - 1) == 0 and nt & (nt - 1) == 0
    hbits = heads.bit_length() - 1
    tbits = nt.bit_length() - 1

    def head_of_col(cols):
        return jnp.bitwise_and(cols, heads - 1)

    @pl.when(g == 0)
    def _():
        lf = _log_sigmoid(ff_ref[...] + b_ref[...])
        lf_ref[...] = lf
        trow = lax.broadcasted_iota(jnp.int32, lf.shape, 0)
        tlane = lax.broadcasted_iota(jnp.int32, lf.shape, 1)
        cs = lf
        sh = 1
        while sh < nt:
            cs = cs + jnp.where(trow >= sh, pltpu.roll(cs, sh, 0), 0.0)
            sh *= 2
        cn = ctot_ref[...] + cs
        cq = jnp.concatenate([cn[:, h:h + 1] for h in range(heads)], axis=0)
        prows = lax.broadcasted_iota(jnp.int32, (nrow, prow), 0)
        pcols = lax.broadcasted_iota(jnp.int32, (nrow, prow), 1)
        cqm_ref[...] = jnp.where(head_of_col(pcols) == jnp.right_shift(prows, tbits), cq, -jnp.inf)
        qf = q_ref[...]
        qall = jnp.concatenate([qf[:, h * hd:(h + 1) * hd] for h in range(heads)], axis=0)
        qall_ref[...] = qall.astype(BF16)
        cnm = jnp.where(tlane < heads, cn, 0.0)
        ck = jnp.zeros((1, LANES), F32)
        for t in range(nt):
            rolled = pltpu.roll(cnm, t * heads, 1)
            ck = ck + jnp.sum(jnp.where(trow == t, rolled, 0.0), axis=0, keepdims=True)
        for h in range(heads):
            ks_ref[pl.ds(h, nt, stride=heads), :] = kn_ref[:, h * hd:(h + 1) * hd]
            vs_ref[pl.ds(h, nt, stride=heads), :] = vn_ref[:, h * hd:(h + 1) * hd]
        kn = _pad_rows(ks_ref[...], LANES).astype(BF16)
        vn = _pad_rows(vs_ref[...], LANES).astype(BF16)
        s = _dot_nt(qall_ref[...], kn) + cq - ck
        rows = lax.broadcasted_iota(jnp.int32, s.shape, 0)
        cols = lax.broadcasted_iota(jnp.int32, s.shape, 1)
        ok = (head_of_col(cols) == jnp.right_shift(rows, tbits))
        ok = ok & (jnp.right_shift(cols, hbits) <= jnp.bitwise_and(rows, nt - 1)) & (cols < nrow)
        s = jnp.where(ok, s, -jnp.inf)
        m0 = jnp.max(s, axis=-1, keepdims=True)
        p = jnp.exp(s - m0)
        m_ref[...] = m0
        l_ref[...] = jnp.sum(p, axis=-1, keepdims=True)
        acc_ref[...] = _dot(p.astype(BF16), vn)

    qall = qall_ref[...]
    cqm = cqm_ref[...]
    ss = []
    for r in range(group):
        kb = k_refs[r][...].astype(BF16)
        ss.append(_dot_nt(qall, kb) + (cqm - cp_ref[r:r + 1, :]))
    mx = ss[0]
    for s in ss[1:]:
        mx = jnp.maximum(mx, s)
    m_prev = m_ref[...]
    m_new = jnp.maximum(m_prev, jnp.max(mx, axis=-1, keepdims=True))
    alpha = jnp.exp(m_prev - m_new)
    acc = alpha * acc_ref[...]
    psum = jnp.zeros((nrow, prow), F32)
    for r in range(group):
        p = jnp.exp(ss[r] - m_new)
        psum = psum + p
        acc = acc + _dot(p.astype(BF16), v_refs[r][...].astype(BF16))
    l_ref[...] = alpha * l_ref[...] + jnp.sum(psum, axis=-1, keepdims=True)
    acc_ref[...] = acc
    m_ref[...] = m_new

    @pl.when(g == ng - 1)
    def _():
        o = acc_ref[...] / l_ref[...]
        for h in range(heads):
            o_ref[:, h * hd:(h + 1) * hd] = o[h * nt:(h + 1) * nt, :]


def _fox_sample(zf, ff, b_fox_pad, ctot, c_past, cache_k, cache_v, page_table, layer, depth,
                row0, nt, heads, hd, group, qcol, prev):
    nb, n_pages = page_table.shape
    prow = cache_k.shape[2]
    width = heads * hd
    ng = n_pages // group
    rb0 = row0 // nt
    nrow = heads * nt
    aliased = prev is not None

    def page_map(r):
        return lambda b, g, pt: (layer, pt[b, g * group + r], 0, 0)

    kern = functools.partial(_fox_sample_kernel, group=group, heads=heads, hd=hd, nt=nt, aliased=aliased)
    new_rows = pl.BlockSpec((None, nrow, hd), lambda b, g, pt: (layer, b, 0))
    new_shape = jax.ShapeDtypeStruct((depth, nb * nrow, hd), F32)
    in_specs = ([_any_spec(), _any_spec()] if aliased else []) + [
        pl.BlockSpec((nt, width), lambda b, g, pt: (rb0 + b, qcol)),
        pl.BlockSpec((nt, width), lambda b, g, pt: (rb0 + b, qcol + 1)),
        pl.BlockSpec((nt, width), lambda b, g, pt: (rb0 + b, qcol + 2)),
        pl.BlockSpec((nt, LANES), lambda b, g, pt: (rb0 + b, 0)),
        pl.BlockSpec((None, 1, LANES), lambda b, g, pt: (layer, 0, 0)),
        pl.BlockSpec((None, 1, LANES), lambda b, g, pt: (layer * nb + b, 0, 0)),
        pl.BlockSpec((None, group, c_past.shape[2]), lambda b, g, pt: (layer * nb + b, g, 0)),
    ]
    in_specs += [pl.BlockSpec((None, None, prow, hd), page_map(r)) for r in range(group)]
    in_specs += [pl.BlockSpec((None, None, prow, hd), page_map(r)) for r in range(group)]
    return pl.pallas_call(
        kern,
        grid_spec=pltpu.PrefetchScalarGridSpec(
            num_scalar_prefetch=1,
            grid=(nb, ng),
            in_specs=in_specs,
            out_specs=[pl.BlockSpec((nt, width), lambda b, g, pt: (b, 0)),
                       pl.BlockSpec((None, nt, LANES), lambda b, g, pt: (b, 0, 0)),
                       new_rows, new_rows],
            scratch_shapes=[
                pltpu.VMEM((nrow, hd), BF16),
                pltpu.VMEM((nrow, prow), F32),
                pltpu.VMEM((nrow, 1), F32),
                pltpu.VMEM((nrow, 1), F32),
                pltpu.VMEM((nrow, hd), F32),
            ],
        ),
        out_shape=[jax.ShapeDtypeStruct((nb * nt, width), F32),
                   jax.ShapeDtypeStruct((nb, nt, LANES), F32),
                   new_shape, new_shape],
        input_output_aliases={1: 2, 2: 3} if aliased else {},
        compiler_params=_params("arbitrary", "arbitrary"),
        name="fox_sample",
    )(page_table, *(prev if aliased else ()), zf, zf, zf, ff, b_fox_pad, ctot, c_past,
      *([cache_k] * group), *([cache_v] * group))


def _hgrn_operators(c, real):
    t = np.arange(c)[:, None]
    j = np.arange(c)[None, :]
    mats = [(j <= t)]
    masks = []
    m = 1
    while m < real:
        blk = t // (2 * m)
        mid = blk * (2 * m) + m - 1
        second = t > mid
        if m < SUBLANES:
            mats.append(np.where(second, (j > mid) & (j <= t), (j > t) & (j <= mid)))
        s = np.arange(c)[None, :]
        masks.append((blk == (s // (2 * m))) & second & (s <= (s // (2 * m)) * (2 * m) + m - 1))
        m *= 2
    a = np.concatenate(mats, axis=0).astype(np.float32)
    a3 = np.concatenate([a, a, a], axis=1)
    return jnp.asarray(a3, BF16), jnp.asarray(np.stack(masks).astype(np.float32)), len(masks)


def _hgrn_kernel(q_ref, f_ref, i_ref, g_ref, lbraw_ref, gn_ref, a3_ref, msk_ref, s0_ref,
                 o_ref, sfin_ref, st_ref, *, layer, heads, dk, n_lev, real):
    c = HGRN_CHUNK
    ci = pl.program_id(1)
    nc = pl.num_programs(1)

    @pl.when(ci == 0)
    def _():
        for h in range(heads):
            st_ref[h] = s0_ref[h].T

    raw = lbraw_ref[...]
    e = jnp.exp(raw - jnp.max(raw, axis=0, keepdims=True))
    sm = e / jnp.sum(e, axis=0, keepdims=True)
    cs = sm[0:1]
    for r in range(1, layer + 1):
        cs = cs + sm[r:r + 1]
    lb = cs - sm[0:1]
    log_lb = jnp.log(lb)
    log_1m = jnp.log1p(-lb)

    zf = _pad_rows(f_ref[...], c)
    q = _pad_rows(q_ref[...], c)
    v = _pad_rows(i_ref[...], c)
    b = log_1m + _log_sigmoid(zf)
    mx = jnp.maximum(log_lb, b)
    g = mx + jnp.log(1.0 + jnp.exp(-jnp.abs(log_lb - b)))
    kr = (1.0 - lb) * _sigmoid(-zf)
    if real < c:
        valid = lax.broadcasted_iota(jnp.int32, g.shape, 0) < real
        g = jnp.where(valid, g, 0.0)
        kr = jnp.where(valid, kr, 0.0)

    e = _dot(a3_ref[...], jnp.concatenate(_split3(g), axis=0))
    eq = e[0:c]
    xq = jnp.exp(eq)
    xk = jnp.exp(eq[c - 1:c] - eq)
    xlast = xq[c - 1:c]
    n_mx = e.shape[0] // c - 1

    def level_factor(l):
        if l < n_mx:
            return jnp.exp(e[(1 + l) * c:(2 + l) * c])
        m = 1 << l
        mids = [jnp.broadcast_to(eq[b0 + m - 1:b0 + m, :], (2 * m, eq.shape[1]))
                for b0 in range(0, c, 2 * m)]
        r = mids[0] if len(mids) == 1 else jnp.concatenate(mids, axis=0)
        return jnp.exp(-jnp.abs(eq - r))

    xls = [level_factor(l) for l in range(n_lev)]
    qe = (q * xq).astype(BF16)
    ke = (kr * xk).astype(BF16)
    vb = v.astype(BF16)
    for h in range(heads):
        hs = slice(h * dk, (h + 1) * dk)
        st = st_ref[h]
        o = _dot_nt(qe[:, hs], st.astype(BF16))
        att = jnp.zeros((c, c), F32)
        for l in range(n_lev):
            xl = xls[l][:, hs]
            p = _dot_nt((q[:, hs] * xl).astype(BF16), (kr[:, hs] * xl).astype(BF16))
            att = att + p * msk_ref[l]
        o = o + _dot(att.astype(BF16), vb[:, hs])
        o = o + jnp.sum(q[:, hs] * kr[:, hs], axis=-1, keepdims=True) * v[:, hs]
        st_ref[h] = st * xlast[:, hs] + _dot_tn(vb[:, hs], ke[:, hs])
        on = _rms_rows(o, gn_ref[:, hs])
        gg = g_ref[:, hs]
        o_ref[:, hs] = (on[:real] * (gg * _sigmoid(gg))).astype(o_ref.dtype)

    @pl.when(ci == nc - 1)
    def _():
        for h in range(heads):
            sfin_ref[h] = st_ref[h].T


def _hgrn(zf, hgrn_lb, hgrn_norm, s0, layer, row0, n_seq, seq, heads, dk, col0, out_rows, out_dtype):
    c = HGRN_CHUNK
    real = min(seq, c)
    assert seq % real == 0
    nc = seq // real
    width = heads * dk
    rb0 = row0 // real
    a3, masks, n_lev = _hgrn_operators(c, real)
    depth = hgrn_lb.shape[0]
    blk = lambda col: pl.BlockSpec((real, width), lambda n, i: (rb0 + n * nc + i, col))
    kern = functools.partial(_hgrn_kernel, layer=layer, heads=heads, dk=dk, n_lev=n_lev, real=real)
    in_specs = [
        blk(col0), blk(col0 + 1), blk(col0 + 2), blk(col0 + 3),
        pl.BlockSpec((depth, width), lambda n, i: (0, 0)),
        pl.BlockSpec((None, 1, width), lambda n, i: (layer, 0, 0)),
        pl.BlockSpec(a3.shape, lambda n, i: (0, 0)),
        pl.BlockSpec(masks.shape, lambda n, i: (0, 0, 0)),
        pl.BlockSpec((None, heads, dk, dk), lambda n, i: (n, 0, 0, 0)),
    ]
    return pl.pallas_call(
        kern,
        grid=(n_seq, nc),
        in_specs=in_specs,
        out_specs=[
            pl.BlockSpec((real, width), lambda n, i: (n * nc + i, 0)),
            pl.BlockSpec((None, heads, dk, dk), lambda n, i: (n, 0, 0, 0)),
        ],
        out_shape=[
            jax.ShapeDtypeStruct((out_rows, width), out_dtype),
            jax.ShapeDtypeStruct((n_seq, heads, dk, dk), F32),
        ],
        scratch_shapes=[pltpu.VMEM((heads, dk, dk), F32)],
        compiler_params=_params("arbitrary", "arbitrary"),
        name="hgrn",
    )(zf, zf, zf, zf, hgrn_lb, hgrn_norm, a3, masks, s0)


def _patch_rows_kernel(*refs):
    n = len(refs) // 3
    for small_ref, out_ref in zip(refs[n:2 * n], refs[2 * n:]):
        out_ref[...] = small_ref[...].astype(out_ref.dtype)


def _patch_rows(bigs, smalls, row0):
    rows = smalls[0].shape[0]
    assert row0 % rows == 0
    n = len(bigs)
    return pl.pallas_call(
        _patch_rows_kernel,
        grid=(1,),
        in_specs=[_any_spec()] * n + [pl.BlockSpec(s.shape, lambda i: (0, 0)) for s in smalls],
        out_specs=[pl.BlockSpec((rows, b.shape[1]), lambda i: (row0 // rows, 0)) for b in bigs],
        out_shape=[jax.ShapeDtypeStruct(b.shape, b.dtype) for b in bigs],
        input_output_aliases={i: i for i in range(n)},
        compiler_params=_params("arbitrary"),
        name="patch_rows",
    )(*bigs, *smalls)


def _merge_kernel(ma_ref, ob_ref, oc_ref, wa_ref, wb_ref, wc_ref, ga_ref, gb_ref, gc_ref, mix_ref):
    ya = _dot(ma_ref[...], wa_ref[...])
    yb = _dot(ob_ref[...], wb_ref[...])
    yc = _dot(oc_ref[...], wc_ref[...])
    gate = lambda r: _sigmoid(r[...].astype(F32))
    mix = gate(ga_ref) * ya + gate(gb_ref) * yb + gate(gc_ref) * yc
    mix_ref[...] = mix.astype(mix_ref.dtype)


def _merge(m_a, o_b, o_c, w_a, w_b, w_c, zg, layer, tm, tn):
    m, k = m_a.shape
    d = w_a.shape[2]
    nj = d // tn
    act = pl.BlockSpec((tm, k), lambda j, i: (i, 0))
    wsp = pl.BlockSpec((None, k, tn), lambda j, i: (layer, 0, j))
    gate = lambda r: pl.BlockSpec((tm, tn), lambda j, i: (i, r * nj + j))
    return pl.pallas_call(
        _merge_kernel,
        grid=(nj, m // tm),
        in_specs=[act, act, act, wsp, wsp, wsp, gate(0), gate(1), gate(2)],
        out_specs=pl.BlockSpec((tm, tn), lambda j, i: (i, j)),
        out_shape=jax.ShapeDtypeStruct((m, d), BF16),
        compiler_params=_params("arbitrary", "arbitrary"),
        name="merge",
    )(m_a, o_b, o_c, w_a, w_b, w_c, zg, zg, zg)


def _residual_matmul_kernel(a_ref, w_ref, x_ref, o_ref):
    o_ref[...] = x_ref[...] + _dot(a_ref[...], w_ref[...])


def _weight_spec(shape, index_map, resident):
    if resident:
        return pl.BlockSpec(shape, index_map, pipeline_mode=pl.Buffered(1))
    return pl.BlockSpec(shape, index_map)


def _residual_matmul(a, w, x, layer, tm, tn):
    m, k = a.shape
    d = w.shape[2]
    return pl.pallas_call(
        _residual_matmul_kernel,
        grid=(m // tm, d // tn),
        in_specs=[
            pl.BlockSpec((tm, k), lambda i, j: (i, 0)),
            _weight_spec((None, k, tn), lambda i, j: (layer, 0, j), tn == d),
            pl.BlockSpec((tm, tn), lambda i, j: (i, j)),
        ],
        out_specs=pl.BlockSpec((tm, tn), lambda i, j: (i, j)),
        out_shape=jax.ShapeDtypeStruct((m, d), F32),
        compiler_params=_params("arbitrary", "arbitrary"),
        name="residual_matmul",
    )(a, w, x)


def _ffn_kernel(x_ref, g_ref, wg_ref, wu_ref, wd_ref, o_ref, h_ref):
    @pl.when(pl.program_id(1) == 0)
    def _():
        x = x_ref[...]
        h_ref[...] = _rms_rows(x, g_ref[...]).astype(BF16)
        o_ref[...] = x

    h = h_ref[...]
    gate = _dot(h, wg_ref[...])
    up = _dot(h, wu_ref[...])
    a = (gate * _sigmoid(gate) * up).astype(BF16)
    o_ref[...] += _dot(a, wd_ref[...])


def _ffn(x, norm, w_gate_up, w_down, layer, tm, tf):
    m, d = x.shape
    f = w_down.shape[1]
    nj = f // tf
    return pl.pallas_call(
        _ffn_kernel,
        grid=(m // tm, nj),
        in_specs=[
            pl.BlockSpec((tm, d), lambda i, j: (i, 0)),
            pl.BlockSpec((None, 1, d), lambda i, j: (layer, 0, 0)),
            pl.BlockSpec((None, d, tf), lambda i, j: (layer, 0, j)),
            pl.BlockSpec((None, d, tf), lambda i, j: (layer, 0, nj + j)),
            pl.BlockSpec((None, tf, d), lambda i, j: (layer, j, 0)),
        ],
        out_specs=pl.BlockSpec((tm, d), lambda i, j: (i, 0)),
        out_shape=jax.ShapeDtypeStruct((m, d), F32),
        scratch_shapes=[pltpu.VMEM((tm, d), BF16)],
        compiler_params=_params("arbitrary", "arbitrary"),
        name="ffn",
    )(x, norm, w_gate_up, w_gate_up, w_down)


def _ple_kernel(x_ref, g_ref, wg_ref, p_ref, wp_ref, o_ref, h_ref, *, tn):
    j = pl.program_id(1)

    @pl.when(j == 0)
    def _():
        h_ref[...] = _rms_rows(x_ref[...], g_ref[...]).astype(BF16)

    gate = _sigmoid(_dot(h_ref[...], wg_ref[...]))
    emb = _dot(p_ref[...].astype(BF16), wp_ref[...])
    xs = x_ref[:, pl.ds(pl.multiple_of(j * tn, tn), tn)]
    o_ref[...] = xs + gate * emb


def _ple(x, norm, w_gate, p, w_proj, layer, tm, tn):
    m, d = x.shape
    pd = p.shape[2]
    return pl.pallas_call(
        functools.partial(_ple_kernel, tn=tn),
        grid=(m // tm, d // tn),
        in_specs=[
            pl.BlockSpec((tm, d), lambda i, j: (i, 0)),
            pl.BlockSpec((None, 1, d), lambda i, j: (layer, 0, 0)),
            _weight_spec((None, d, tn), lambda i, j: (layer, 0, j), tn == d),
            pl.BlockSpec((None, tm, pd), lambda i, j: (layer, i, 0)),
            _weight_spec((None, pd, tn), lambda i, j: (layer, 0, j), tn == d),
        ],
        out_specs=pl.BlockSpec((tm, tn), lambda i, j: (i, j)),
        out_shape=jax.ShapeDtypeStruct((m, d), F32),
        scratch_shapes=[pltpu.VMEM((tm, d), BF16)],
        compiler_params=_params("arbitrary", "arbitrary"),
        name="ple",
    )(x, norm, w_gate, p, w_proj)


def _final_norm_kernel(x_ref, g_ref, o_ref):
    o_ref[...] = _rms_rows(x_ref[...], g_ref[...])


def _final_norm(x, g, row0, rows, tile):
    d = x.shape[1]
    rb0 = row0 // tile
    return pl.pallas_call(
        _final_norm_kernel,
        grid=(rows // tile,),
        in_specs=[pl.BlockSpec((tile, d), lambda i: (rb0 + i, 0)), pl.BlockSpec((1, d), lambda i: (0, 0))],
        out_specs=pl.BlockSpec((tile, d), lambda i: (i, 0)),
        out_shape=jax.ShapeDtypeStruct((rows, d), F32),
        compiler_params=_params("arbitrary"),
        name="final_norm",
    )(x, g)


def kernel(x_prompt, x_sample, cache_k, cache_v, cache_logf, state_conv, state_hgrn, page_table,
           p_prompt, p_sample, norm1, w_in, b_fox, conv_w, hgrn_lb, hgrn_norm, w_a, w_b, w_c, w_o,
           norm2, w_gate_up, w_down, ple_norm, ple_gate, ple_proj, final_norm):
    bp, seq, d = x_prompt.shape
    db, dseq, _ = x_sample.shape
    depth = w_in.shape[0]
    heads, hd = cache_k.shape[3], cache_k.shape[4]
    page = cache_k.shape[2]
    n_pool = cache_k.shape[1]
    cw = w_a.shape[1]
    fw = w_b.shape[1]
    rw = w_c.shape[1]
    rheads = state_hgrn.shape[2]
    dk = state_hgrn.shape[3]
    assert fw == heads * hd and rw == rheads * dk and state_hgrn.shape[4] == dk and cw == fw == rw
    mp = bp * seq
    ms = db * dseq
    m = mp + ms

    o_ff = 3 * cw + 3 * fw
    widths = (3 * cw, fw, 2 * fw + 4 * rw, 3 * d)
    w_main = jnp.concatenate([w_in[:, :, :o_ff], w_in[:, :, o_ff + heads:]], axis=2).astype(BF16)
    w_ff = jnp.pad(w_in[:, :, o_ff:o_ff + heads], ((0, 0), (0, 0), (0, LANES - heads))).astype(BF16)
    assert w_main.shape[2] == sum(widths)
    tn_in = fw
    qcol = 3 * cw // tn_in
    rcol = qcol + 3

    bf = lambda a: a.astype(BF16)
    w_a_b, w_b_b, w_c_b, w_o_b = bf(w_a), bf(w_b), bf(w_c), bf(w_o)
    w_gu_b, w_dn_b, w_pg_b, w_pp_b = bf(w_gate_up), bf(w_down), bf(ple_gate), bf(ple_proj)
    row3 = lambda a: a.reshape(a.shape[0], 1, a.shape[1])
    norm1_r, norm2_r, ple_norm_r, hnorm_r = row3(norm1), row3(norm2), row3(ple_norm), row3(hgrn_norm)
    b_fox_pad = row3(jnp.pad(b_fox, ((0, 0), (0, LANES - heads))))

    x = _stack_rows(x_prompt.reshape(1, mp, d), x_sample.reshape(1, ms, d)).reshape(m, d)
    p_all = _stack_rows(p_prompt.reshape(depth, mp, -1), p_sample.reshape(depth, ms, -1))
    ck = cache_k.reshape(depth, n_pool, page * heads, hd)
    cv = cache_v.reshape(depth, n_pool, page * heads, hd)

    n_pages = page_table.shape[1]
    group = _tile(n_pages, 16, 1)
    lf_flat = cache_logf.reshape(depth, n_pool, 1, page * heads)
    lf_pages = _gather_logf_pages(lf_flat, page_table, group)
    c_past, c_tot = _past_cumsum(lf_pages, heads)

    tm = _tile(m, 704, 16)
    tq =_tile(seq, 512, LANES)
    conv_tile = _tile(seq, 512, 8)
    zeros_conv = jnp.zeros((bp, 2, cw), F32)
    zeros_state = jnp.zeros((bp, rheads, dk, dk), F32)

    outs = {k: [] for k in ("lfp", "cp", "sp", "lfs", "cs", "ss")}
    kvp = None
    kvs = None
    for i in range(depth):
        zf, zg, ff = _in_proj(x, norm1_r, w_main, w_ff, i, tm, tn_in, widths, hd)

        m_a, conv_p = _conv(zf, conv_w, zeros_conv, i, 0, bp, seq, conv_tile, cw, m, BF16)
        ma_s, conv_s = _conv(zf, conv_w, state_conv[i], i, mp, db, dseq, dseq, cw, ms, F32)

        lf_p, qaug_t, kaug, v_t, kp, vp = _logf_prompt(ff, b_fox_pad, zf, i, depth, bp, seq,
                                                       heads, hd, qcol, kvp)
        kvp = (kp, vp)
        o_b = _fox_prompt(qaug_t, kaug, v_t, m, bp, seq, heads, hd, tq)
        ob_s, lf_s, ks, vs = _fox_sample(zf, ff, b_fox_pad, c_tot, c_past, ck, cv, page_table, i,
                                         depth, mp, dseq, heads, hd, group, qcol, kvs)
        kvs = (ks, vs)

        o_c, st_p = _hgrn(zf, hgrn_lb, hnorm_r, zeros_state, i, 0, bp, seq, rheads, dk, rcol, m, BF16)
        oc_s, st_s = _hgrn(zf, hgrn_lb, hnorm_r, state_hgrn[i], i, mp, db, dseq, rheads, dk, rcol,
                           ms, F32)

        m_a, o_b, o_c = _patch_rows((m_a, o_b, o_c), (ma_s, ob_s, oc_s), mp)
        mix = _merge(m_a, o_b, o_c, w_a_b, w_b_b, w_c_b, zg, i, tm, 1024)
        x = _residual_matmul(mix, w_o_b, x, i, tm, d)

        x = _ffn(x, norm2_r, w_gu_b, w_dn_b, i, tm, _tile(w_down.shape[1], 512, 2 * LANES))
        x = _ple(x, ple_norm_r, w_pg_b, p_all, w_pp_b, i, tm, d)

        outs["lfp"].append(lf_p[:, :, :heads])
        outs["lfs"].append(lf_s[:, :, :heads])
        outs["cp"].append(conv_p)
        outs["cs"].append(conv_s)
        outs["sp"].append(st_p)
        outs["ss"].append(st_s)

    fn = final_norm.reshape(1, d)
    y_p = _final_norm(x, fn, 0, mp, _tile(mp, 1024, 8))
    y_s = _final_norm(x, fn, mp, ms, ms)
    st = lambda k: jnp.stack(outs[k])
    return (y_p.reshape(bp, seq, d), y_s.reshape(db, dseq, d),
            kp.reshape(depth, bp, seq, heads, hd), vp.reshape(depth, bp, seq, heads, hd),
            st("lfp"), st("cp"), st("sp"),
            ks.reshape(depth, db, dseq, heads, hd), vs.reshape(depth, db, dseq, heads, hd),
            st("lfs"), st("cs"), st("ss"))
```
